```python
import jax, jax.numpy as jnp
from jax import lax
import numpy as np

D_MODEL = 2048
BATCH = 2
SEQ = 4096
DEPTH = 4

GRID_W = 64
CTX_LEN = 256
NA_HEADS = 8
NA_HEAD_DIM = 128
NA_WIDTH = NA_HEADS * NA_HEAD_DIM
WIN_R = 8
WIN_C = 16
ROPE_BASE = 10000.0
SSD_HEADS = 16
SSD_HEAD_DIM = 64
SSD_INNER = SSD_HEADS * SSD_HEAD_DIM
SSD_GROUPS = 2
SSD_STATE = 128
SSD_CHUNK = 128
CONV_K = 5
CONV_CH = SSD_INNER + 2 * SSD_GROUPS * SSD_STATE
IN_W = 3 * NA_WIDTH + SSD_INNER + CONV_CH + 2 * SSD_HEADS
MIX_W = NA_WIDTH + SSD_INNER
F_GROUPS = 4
D_FF = 5632
N_MOD = 9
N_EVEN = (DEPTH + 1) // 2
N_ODD = DEPTH // 2
EPS = 1e-6

kernel_name = 'hybrid_natten_ssd_fnet_macaron_dit'


def rms_norm(x, g):
    xf = x.astype(jnp.float32)
    y = xf * lax.rsqrt(jnp.mean(xf * xf, axis=-1, keepdims=True) + EPS)
    return (y * g.astype(jnp.float32)).astype(x.dtype)


def modulate(x, g, shift, scale):
    return rms_norm(x, g) * (1 + scale) + shift


def swiglu(h, w1, w3, w2):
    return (jax.nn.silu(h @ w1) * (h @ w3)) @ w2


def axial_rope(x, row, col):
    quarter = x.shape[-1] // 4
    inv_freq = ROPE_BASE ** (-jnp.arange(quarter, dtype=jnp.float32) / quarter)

    def rotate(xa, pos):
        ang = pos.astype(jnp.float32)[:, None] * inv_freq[None, :]
        cos = jnp.cos(ang)[None, :, None, :]
        sin = jnp.sin(ang)[None, :, None, :]
        x1, x2 = jnp.split(xa.astype(jnp.float32), 2, axis=-1)
        return jnp.concatenate([x1 * cos - x2 * sin, x2 * cos + x1 * sin], axis=-1)

    xr, xc = jnp.split(x, 2, axis=-1)
    return jnp.concatenate([rotate(xr, row), rotate(xc, col)], axis=-1).astype(x.dtype)


def neighbourhood_attention(q, k, v, kc, vc, rpb, row, col):
    bsz, L, H, dh = q.shape
    rows = L // GRID_W
    wr = min(WIN_R, rows)
    scale = dh ** -0.5
    qg = axial_rope(q, row, col).reshape(bsz, rows, GRID_W, H, dh)
    kg = axial_rope(k, row, col).reshape(bsz, rows, GRID_W, H, dh)
    vg = v.reshape(bsz, rows, GRID_W, H, dh)
    r = jnp.arange(rows)
    r0 = jnp.clip(r - wr // 2, 0, rows - wr)
    row_idx = r0[:, None] + jnp.arange(wr)[None, :]
    kb = kg[:, row_idx]
    vb = vg[:, row_idx]
    s_win = jnp.einsum('brqhd,brwkhd->bhrqwk', qg, kb).astype(jnp.float32) * scale
    cq = jnp.arange(GRID_W)
    c0 = jnp.clip(cq - WIN_C // 2, 0, GRID_W - WIN_C)
    col_ok = (cq[None, :] >= c0[:, None]) & (cq[None, :] < c0[:, None] + WIN_C)
    dr = row_idx - r[:, None] + (WIN_R - 1)
    dc = jnp.clip(cq[None, :] - cq[:, None], -(WIN_C - 1), WIN_C - 1) + (WIN_C - 1)
    bias = rpb.astype(jnp.float32)[:, dr[:, None, :, None], dc[None, :, None, :]]
    s_win = jnp.where(col_ok[:, None, :], s_win + bias, -jnp.inf)
    q_plain = q.reshape(bsz, rows, GRID_W, H, dh)
    s_ctx = jnp.einsum('brqhd,bchd->bhrqc', q_plain, kc).astype(jnp.float32) * scale
    n_win = wr * GRID_W
    logits = jnp.concatenate([s_win.reshape(bsz, H, rows, GRID_W, n_win), s_ctx], axis=-1)
    p = jax.nn.softmax(logits, axis=-1).astype(v.dtype)
    p_win = p[..., :n_win].reshape(bsz, H, rows, GRID_W, wr, GRID_W)
    p_ctx = p[..., n_win:]
    o = jnp.einsum('bhrqwk,brwkhd->brqhd', p_win, vb) + jnp.einsum('bhrqc,bchd->brqhd', p_ctx, vc)
    return o.reshape(bsz, L, H * dh)


def context_attention(qc, kc, vc):
    bsz, lc, H, dh = qc.shape
    s = jnp.einsum('bqhd,bkhd->bhqk', qc, kc).astype(jnp.float32) * dh ** -0.5
    p = jax.nn.softmax(s, axis=-1).astype(vc.dtype)
    return jnp.einsum('bhqk,bkhd->bqhd', p, vc).reshape(bsz, lc, H * dh)


def centred_dwconv(x, w, b):
    y = lax.conv_general_dilated(x, w[:, None, :].astype(x.dtype), window_strides=(1,),
                                 padding=[(CONV_K // 2, CONV_K // 2)],
                                 dimension_numbers=('NWC', 'WIO', 'NWC'),
                                 feature_group_count=x.shape[-1])
    return y + b.astype(x.dtype)


def ssd_chunked(x, dt, A, bm, cm, h0, with_y):
    bsz, L, H, P = x.shape
    Q = SSD_CHUNK
    nc = L // Q
    rep = H // SSD_GROUPS
    f32 = jnp.float32
    xq = x.astype(f32).reshape(bsz, nc, Q, H, P)
    dtq = dt.reshape(bsz, nc, Q, H)
    bh = jnp.repeat(bm.astype(f32), rep, axis=2).reshape(bsz, nc, Q, H, SSD_STATE)
    ch = jnp.repeat(cm.astype(f32), rep, axis=2).reshape(bsz, nc, Q, H, SSD_STATE)
    a_cum = jnp.cumsum(dtq * A.astype(f32), axis=2)
    xdt = xq * dtq[..., None]
    a_end = a_cum[:, :, -1]
    states = jnp.einsum('bcjhn,bcjhp->bchpn', bh * jnp.exp(a_end[:, :, None] - a_cum)[..., None], xdt)

    def step(h, inp):
        dec, s = inp
        return dec[:, :, None, None] * h + s, h

    h_fin, h_in = lax.scan(step, h0, (jnp.moveaxis(jnp.exp(a_end), 1, 0), jnp.moveaxis(states, 1, 0)))
    if not with_y:
        return None, h_fin
    h_in = jnp.moveaxis(h_in, 0, 1)
    seg = a_cum[:, :, :, None, :] - a_cum[:, :, None, :, :]
    lower = jnp.tril(jnp.ones((Q, Q), dtype=bool))[None, None, :, :, None]
    decay = jnp.exp(jnp.where(lower, seg, -jnp.inf))
    scores = jnp.einsum('bcihn,bcjhn->bcijh', ch, bh) * decay
    y = (jnp.einsum('bcijh,bcjhp->bcihp', scores, xdt)
         + jnp.einsum('bcihn,bchpn->bcihp', ch * jnp.exp(a_cum)[..., None], h_in))
    return y.reshape(bsz, L, H, P), h_fin


def ssd_bidirectional(xbc, dt_raw, h0_f, h0_b, conv_w, conv_b, dt_bias, a_log, d_skip, with_y):
    xbc = jax.nn.silu(centred_dwconv(xbc, conv_w, conv_b))
    bsz, L, _ = xbc.shape
    gn = SSD_GROUPS * SSD_STATE
    xs = xbc[..., :SSD_INNER].reshape(bsz, L, SSD_HEADS, SSD_HEAD_DIM)
    bm = xbc[..., SSD_INNER:SSD_INNER + gn].reshape(bsz, L, SSD_GROUPS, SSD_STATE)
    cm = xbc[..., SSD_INNER + gn:].reshape(bsz, L, SSD_GROUPS, SSD_STATE)
    A = -jnp.exp(a_log.astype(jnp.float32))
    dt = jax.nn.softplus(dt_raw.astype(jnp.float32).reshape(bsz, L, 2, SSD_HEADS) + dt_bias.astype(jnp.float32))
    flip = lambda t: jnp.flip(t, axis=1)
    y_f, h_f = ssd_chunked(xs, dt[:, :, 0], A[0], bm, cm, h0_f, with_y)
    y_b, h_b = ssd_chunked(flip(xs), flip(dt[:, :, 1]), A[1], flip(bm), flip(cm), h0_b, with_y)
    if not with_y:
        return None, h_f, h_b
    y = y_f + flip(y_b) + d_skip.astype(jnp.float32)[:, None] * xs.astype(jnp.float32)
    return y.reshape(bsz, L, SSD_INNER), h_f, h_b


def gated_rmsnorm(y, z, g):
    yz = y.astype(jnp.float32) * jax.nn.silu(z.astype(jnp.float32))
    bsz, L, _ = yz.shape
    yg = yz.reshape(bsz, L, SSD_GROUPS, SSD_INNER // SSD_GROUPS)
    yg = yg * lax.rsqrt(jnp.mean(yg * yg, axis=-1, keepdims=True) + EPS)
    return (yg.reshape(bsz, L, SSD_INNER) * g.astype(jnp.float32)).astype(z.dtype)


def even_mixer(h, hc, row, col, w_in, w_out, qk_g, rpb, conv_w, conv_b, dt_bias, a_log, d_skip, ssd_g, ctx_out):
    cuts = [NA_WIDTH, 2 * NA_WIDTH, 3 * NA_WIDTH, 3 * NA_WIDTH + SSD_INNER, 3 * NA_WIDTH + SSD_INNER + CONV_CH]
    q, k, v, z, xbc, dt = jnp.split(h @ w_in, cuts, axis=-1)
    qc, kc, vc, zc, xbcc, dtc = jnp.split(hc @ w_in, cuts, axis=-1)
    heads = lambda t: t.reshape(t.shape[0], t.shape[1], NA_HEADS, NA_HEAD_DIM)
    q, k, v = rms_norm(heads(q), qk_g[0]), rms_norm(heads(k), qk_g[1]), heads(v)
    qc, kc, vc = rms_norm(heads(qc), qk_g[0]), rms_norm(heads(kc), qk_g[1]), heads(vc)
    attn = neighbourhood_attention(q, k, v, kc, vc, rpb, row, col)
    h0 = jnp.zeros((hc.shape[0], SSD_HEADS, SSD_HEAD_DIM, SSD_STATE), jnp.float32)
    yc_ssd, hf, hb = ssd_bidirectional(xbcc, dtc, h0, h0, conv_w, conv_b, dt_bias, a_log, d_skip, ctx_out)
    y_ssd, _, _ = ssd_bidirectional(xbc, dt, hf, hb, conv_w, conv_b, dt_bias, a_log, d_skip, True)
    out = jnp.concatenate([attn, gated_rmsnorm(y_ssd, z, ssd_g)], axis=-1) @ w_out
    if not ctx_out:
        return out, None
    attn_c = context_attention(qc, kc, vc)
    out_c = jnp.concatenate([attn_c, gated_rmsnorm(yc_ssd, zc, ssd_g)], axis=-1) @ w_out
    return out, out_c


def fourier_mix(h, w):
    bsz, L, dm = h.shape
    hg = h.astype(jnp.float32).reshape(bsz, L, F_GROUPS, dm // F_GROUPS)
    f = jnp.fft.fft2(hg, axes=(1, 3), norm='ortho').real
    return f.reshape(bsz, L, dm).astype(h.dtype) @ w


def setup_inputs(seed: int = 0) -> dict:
    key = jax.random.key(seed)
    ks = jax.random.split(key, 24)
    nrm = lambda k, shape, s: jax.random.normal(k, shape, jnp.float32) * s
    dt0 = jnp.exp(jax.random.uniform(ks[17], (N_EVEN, 2, SSD_HEADS), jnp.float32)
                  * (np.log(0.1) - np.log(0.001)) + np.log(0.001))
    return {
        'x': nrm(ks[0], (BATCH, SEQ, D_MODEL), 1.0),
        'c': nrm(ks[1], (BATCH, D_MODEL), 1.0),
        'ctx': nrm(ks[2], (BATCH, CTX_LEN, D_MODEL), 1.0),
        'c_ctx': nrm(ks[3], (D_MODEL,), 1.0),
        'mod_w': nrm(ks[4], (DEPTH, D_MODEL, N_MOD * D_MODEL), 0.25 * D_MODEL ** -0.5),
        'mod_b': nrm(ks[5], (DEPTH, N_MOD * D_MODEL), 0.02),
        'norm_g': 1.0 + nrm(ks[6], (DEPTH, 3, D_MODEL), 0.02),
        'ffn_w1': nrm(ks[7], (DEPTH, 2, D_MODEL, D_FF), D_MODEL ** -0.5),
        'ffn_w3': nrm(ks[8], (DEPTH, 2, D_MODEL, D_FF), D_MODEL ** -0.5),
        'ffn_w2': nrm(ks[9], (DEPTH, 2, D_FF, D_MODEL), D_FF ** -0.5),
        'mix_w_in': nrm(ks[10], (N_EVEN, D_MODEL, IN_W), D_MODEL ** -0.5),
        'mix_w_out': nrm(ks[11], (N_EVEN, MIX_W, D_MODEL), MIX_W ** -0.5),
        'qk_g': 1.0 + nrm(ks[12], (N_EVEN, 2, NA_HEAD_DIM), 0.02),
        'rpb': nrm(ks[13], (N_EVEN, NA_HEADS, 2 * WIN_R - 1, 2 * WIN_C - 1), 0.1),
        'conv_w': nrm(ks[14], (N_EVEN, CONV_K, CONV_CH), CONV_K ** -0.5),
        'conv_b': nrm(ks[15], (N_EVEN, CONV_CH), 0.02),
        'dt_bias': dt0 + jnp.log(-jnp.expm1(-dt0)),
        'a_log': jnp.log(jax.random.uniform(ks[18], (N_EVEN, 2, SSD_HEADS), jnp.float32, 1.0, 16.0)),
        'ssd_d': 1.0 + nrm(ks[19], (N_EVEN, SSD_HEADS), 0.1),
        'ssd_norm_g': 1.0 + nrm(ks[20], (N_EVEN, SSD_INNER), 0.02),
        'fourier_w': nrm(ks[21], (N_ODD, D_MODEL, D_MODEL), D_MODEL ** -0.5),
    }


def reference(x, c, ctx, c_ctx, mod_w, mod_b, norm_g, ffn_w1, ffn_w3, ffn_w2, mix_w_in, mix_w_out, qk_g, rpb,
              conv_w, conv_b, dt_bias, a_log, ssd_d, ssd_norm_g, fourier_w):
    L = x.shape[1]
    t = jnp.arange(L)
    row = t // GRID_W
    col = t % GRID_W
    last_ctx = ((DEPTH - 1) // 2) * 2
    xc = ctx
    for i in range(DEPTH):
        use_ctx = i <= last_ctx
        ctx_out = i < last_ctx
        j = i // 2
        m = jnp.split((jax.nn.silu(c) @ mod_w[i] + mod_b[i])[:, None, :], N_MOD, axis=-1)
        x = x + 0.5 * m[2] * swiglu(modulate(x, norm_g[i, 0], m[0], m[1]), ffn_w1[i, 0], ffn_w3[i, 0], ffn_w2[i, 0])
        h = modulate(x, norm_g[i, 1], m[3], m[4])
        hc = None
        mc = None
        if use_ctx:
            mc = jnp.split(jax.nn.silu(c_ctx) @ mod_w[i] + mod_b[i], N_MOD, axis=-1)
            xc = xc + 0.5 * mc[2] * swiglu(modulate(xc, norm_g[i, 0], mc[0], mc[1]), ffn_w1[i, 0], ffn_w3[i, 0], ffn_w2[i, 0])
            hc = modulate(xc, norm_g[i, 1], mc[3], mc[4])
        if i % 2 == 0:
            mix, mix_c = even_mixer(h, hc, row, col, mix_w_in[j], mix_w_out[j], qk_g[j], rpb[j], conv_w[j], conv_b[j],
                                    dt_bias[j], a_log[j], ssd_d[j], ssd_norm_g[j], ctx_out)
        else:
            mix = fourier_mix(h, fourier_w[j])
            mix_c = fourier_mix(hc, fourier_w[j]) if ctx_out else None
        x = x + m[5] * mix
        x = x + 0.5 * m[8] * swiglu(modulate(x, norm_g[i, 2], m[6], m[7]), ffn_w1[i, 1], ffn_w3[i, 1], ffn_w2[i, 1])
        if ctx_out:
            xc = xc + mc[5] * mix_c
            xc = xc + 0.5 * mc[8] * swiglu(modulate(xc, norm_g[i, 2], mc[6], mc[7]), ffn_w1[i, 1], ffn_w3[i, 1], ffn_w2[i, 1])
    return x
```

```python
import functools

import numpy as np
import jax
import jax.numpy as jnp
from jax import lax
from jax.experimental import pallas as pl
from jax.experimental.pallas import tpu as pltpu

F32 = jnp.float32
BF16 = jnp.bfloat16

D_MODEL = 2048
BATCH = 2
SEQ = 4096
DEPTH = 4
GRID_W = 64
GRID_ROWS = SEQ // GRID_W
CTX_LEN = 256
N_LAT = BATCH * SEQ
N_TOK = N_LAT + BATCH * CTX_LEN
NA_HEADS = 8
NA_HEAD_DIM = 128
NA_WIDTH = NA_HEADS * NA_HEAD_DIM
WIN_R = 8
WIN_C = 16
ROPE_BASE = 10000.0
SSD_HEADS = 16
SSD_HEAD_DIM = 64
SSD_INNER = SSD_HEADS * SSD_HEAD_DIM
SSD_GROUPS = 2
SSD_STATE = 128
SSD_CHUNK = 128
CONV_K = 5
CONV_CH = SSD_INNER + 2 * SSD_GROUPS * SSD_STATE
IN_W = 3 * NA_WIDTH + SSD_INNER + CONV_CH + 2 * SSD_HEADS
LANE = 128
IN_W_PAD = ((IN_W + LANE - 1) // LANE) * LANE
F_GROUPS = 4
F_GROUP_CH = D_MODEL // F_GROUPS
D_FF = 5632
N_MOD = 9
EPS = 1e-6
ATTN_SCALE = NA_HEAD_DIM ** -0.5

COL_Q = 0
COL_K = NA_WIDTH // LANE
COL_V = 2 * NA_WIDTH // LANE
COL_Z = 3 * NA_WIDTH // LANE
COL_XBC = (3 * NA_WIDTH + SSD_INNER) // LANE
COL_DT = (3 * NA_WIDTH + SSD_INNER + CONV_CH) // LANE

Q_ROWS = 8
Q_BLK = Q_ROWS * GRID_W
K_ROWS = 16
K_BLK = K_ROWS * GRID_W
N_QBLK = GRID_ROWS // Q_ROWS

VMEM_LIMIT = 56 * 1024 * 1024


def _cparams(sem):
    return pltpu.CompilerParams(dimension_semantics=sem, vmem_limit_bytes=VMEM_LIMIT)


def _sigmoid(x):
    return 1.0 / (1.0 + jnp.exp(-x))


def _silu(x):
    return x * _sigmoid(x)


def _mod_norm(x, g, shift, scale):
    ms = jnp.mean(x * x, axis=-1, keepdims=True)
    y = x * lax.rsqrt(ms + EPS) * g
    return y * (1.0 + scale) + shift


def _mod_row(tile_rows):
    def f(t):
        return jnp.minimum((t * tile_rows) // SEQ, BATCH)
    return f


def _mod_kernel(c_ref, w_ref, b_ref, o_ref):
    s = _silu(c_ref[...]).astype(BF16)
    o_ref[...] = jnp.dot(s, w_ref[...].astype(BF16), preferred_element_type=F32) + b_ref[...]


def _modulation(c, c_ctx, mod_w, mod_b):
    rows = 8
    cvec = jnp.concatenate([c, c_ctx[None, :], jnp.zeros((rows - BATCH - 1, D_MODEL), F32)], axis=0)
    n = N_MOD * D_MODEL
    tn = 1024
    out = pl.pallas_call(
        _mod_kernel,
        grid=(DEPTH, n // tn),
        in_specs=[
            pl.BlockSpec((rows, D_MODEL), lambda i, j: (0, 0)),
            pl.BlockSpec((None, D_MODEL, tn), lambda i, j: (i, 0, j)),
            pl.BlockSpec((None, 1, tn), lambda i, j: (i, 0, j)),
        ],
        out_specs=pl.BlockSpec((None, rows, tn), lambda i, j: (i, 0, j)),
        out_shape=jax.ShapeDtypeStruct((DEPTH, rows, n), F32),
        compiler_params=_cparams(("arbitrary", "arbitrary")),
        name="modulation",
    )(cvec, mod_w, mod_b.reshape(DEPTH, 1, n))
    return out[:, :BATCH + 1].reshape(DEPTH, BATCH + 1, N_MOD, 1, D_MODEL)


def _ffn_kernel(x_ref, mod_ref, g_ref, w1_ref, w3_ref, w2_ref, o_ref, h_scr, acc_scr):
    j = pl.program_id(1)

    @pl.when(j == 0)
    def _():
        h = _mod_norm(x_ref[...], g_ref[...], mod_ref[0], mod_ref[1])
        h_scr[...] = h.astype(BF16)
        acc_scr[...] = jnp.zeros_like(acc_scr)

    h = h_scr[...]
    a = jnp.dot(h, w1_ref[...], preferred_element_type=F32)
    b = jnp.dot(h, w3_ref[...], preferred_element_type=F32)
    u = (_silu(a) * b).astype(BF16)
    acc_scr[...] += jnp.dot(u, w2_ref[...], preferred_element_type=F32)

    @pl.when(j == pl.num_programs(1) - 1)
    def _():
        o_ref[...] = x_ref[...] + 0.5 * mod_ref[2] * acc_scr[...]


def _ffn(x, n_rows, mods_i, sub, g, w1, w3, w2, layer, which):
    tm, tf = 512, 512
    mrow = _mod_row(tm)
    return pl.pallas_call(
        _ffn_kernel,
        grid=(n_rows // tm, D_FF // tf),
        in_specs=[
            pl.BlockSpec((tm, D_MODEL), lambda t, j: (t, 0)),
            pl.BlockSpec((None, 3, 1, D_MODEL), lambda t, j: (mrow(t), sub, 0, 0)),
            pl.BlockSpec((1, D_MODEL), lambda t, j: (0, 0)),
            pl.BlockSpec((None, None, D_MODEL, tf), lambda t, j: (layer, which, 0, j)),
            pl.BlockSpec((None, None, D_MODEL, tf), lambda t, j: (layer, which, 0, j)),
            pl.BlockSpec((None, None, tf, D_MODEL), lambda t, j: (layer, which, j, 0)),
        ],
        out_specs=pl.BlockSpec((tm, D_MODEL), lambda t, j: (t, 0)),
        out_shape=jax.ShapeDtypeStruct((n_rows, D_MODEL), F32),
        scratch_shapes=[pltpu.VMEM((tm, D_MODEL), BF16), pltpu.VMEM((tm, D_MODEL), F32)],
        compiler_params=_cparams(("arbitrary", "arbitrary")),
        name="ffn",
    )(x, mods_i, g, w1, w3, w2)


def _norm_matmul_kernel(x_ref, mod_ref, g_ref, w_ref, o_ref, h_scr):
    @pl.when(pl.program_id(1) == 0)
    def _():
        h_scr[...] = _mod_norm(x_ref[...], g_ref[...], mod_ref[0], mod_ref[1]).astype(BF16)

    o_ref[...] = jnp.dot(h_scr[...], w_ref[...], preferred_element_type=F32)


def _in_proj(x, n_rows, mods_i, g, w_in, j_even):
    tm, tn = 512, 1152
    mrow = _mod_row(tm)
    return pl.pallas_call(
        _norm_matmul_kernel,
        grid=(n_rows // tm, IN_W_PAD // tn),
        in_specs=[
            pl.BlockSpec((tm, D_MODEL), lambda t, j: (t, 0)),
            pl.BlockSpec((None, 3, 1, D_MODEL), lambda t, j: (mrow(t), 1, 0, 0)),
            pl.BlockSpec((1, D_MODEL), lambda t, j: (0, 0)),
            pl.BlockSpec((None, D_MODEL, tn), lambda t, j: (j_even, 0, j)),
        ],
        out_specs=pl.BlockSpec((tm, tn), lambda t, j: (t, j)),
        out_shape=jax.ShapeDtypeStruct((n_rows, IN_W_PAD), F32),
        scratch_shapes=[pltpu.VMEM((tm, D_MODEL), BF16)],
        compiler_params=_cparams(("arbitrary", "arbitrary")),
        name="in_proj",
    )(x, mods_i, g, w_in)


def _head_rms(x, g):
    ms = jnp.mean(x * x, axis=-1, keepdims=True)
    return x * lax.rsqrt(ms + EPS) * g


def _rope(x, cos, sin):
    lane = lax.broadcasted_iota(jnp.int32, x.shape, 1)
    first_half = (lane & (NA_HEAD_DIM // 4)) == 0
    partner = jnp.where(first_half, pltpu.roll(x, NA_HEAD_DIM - NA_HEAD_DIM // 4, 1),
                        pltpu.roll(x, NA_HEAD_DIM // 4, 1))
    return x * cos + partner * sin


def _dot_nt(a, b):
    return lax.dot_general(a, b, (((1,), (1,)), ((), ())), preferred_element_type=F32)


def _na_kernel(q_ref, k_ref, v_ref, kc_ref, vc_ref, cos_ref, sin_ref, qkg_ref, bias_ref, o_ref,
               k_scr, v_scr, kc_scr, vc_scr):
    j = pl.program_id(2)

    @pl.when(j == 0)
    def _():
        kn = _head_rms(k_ref[...], qkg_ref[1])
        k_scr[...] = _rope(kn, cos_ref[...], sin_ref[...]).astype(BF16)
        v_scr[...] = v_ref[...].astype(BF16)
        kc_scr[...] = _head_rms(kc_ref[...], qkg_ref[1]).astype(BF16)
        vc_scr[...] = vc_ref[...].astype(BF16)

    q0 = pl.multiple_of(j * Q_BLK, Q_BLK)
    qn = _head_rms(q_ref[...], qkg_ref[0])
    qr = _rope(qn, cos_ref[pl.ds(q0, Q_BLK), :], sin_ref[pl.ds(q0, Q_BLK), :]).astype(BF16)
    k0 = pl.multiple_of(jnp.clip(Q_ROWS * j - WIN_R // 2, 0, GRID_ROWS - K_ROWS) * GRID_W, GRID_W)
    kw = k_scr[pl.ds(k0, K_BLK), :]
    vw = v_scr[pl.ds(k0, K_BLK), :]
    s_win = _dot_nt(qr, kw) * ATTN_SCALE + bias_ref[...]
    s_ctx = _dot_nt(qn.astype(BF16), kc_scr[...]) * ATTN_SCALE
    m = jnp.maximum(jnp.max(s_win, axis=-1, keepdims=True), jnp.max(s_ctx, axis=-1, keepdims=True))
    p_win = jnp.exp(s_win - m)
    p_ctx = jnp.exp(s_ctx - m)
    denom = jnp.sum(p_win, axis=-1, keepdims=True) + jnp.sum(p_ctx, axis=-1, keepdims=True)
    o = (jnp.dot(p_win.astype(BF16), vw, preferred_element_type=F32)
         + jnp.dot(p_ctx.astype(BF16), vc_scr[...], preferred_element_type=F32))
    o_ref[...] = (o / denom).astype(o_ref.dtype)


def _bias_pattern(j):
    return jnp.where(j == 0, 0, jnp.where(j == N_QBLK - 1, 2, 1))


def _na_attention(proj, out_rows, cos_t, sin_t, qkg, bias):
    lat_blk = SEQ // Q_BLK
    ctx_blk0 = N_LAT // CTX_LEN
    hd = NA_HEAD_DIM
    return pl.pallas_call(
        _na_kernel,
        grid=(BATCH, NA_HEADS, N_QBLK),
        in_specs=[
            pl.BlockSpec((Q_BLK, hd), lambda b, h, j: (b * lat_blk + j, COL_Q + h)),
            pl.BlockSpec((SEQ, hd), lambda b, h, j: (b, COL_K + h)),
            pl.BlockSpec((SEQ, hd), lambda b, h, j: (b, COL_V + h)),
            pl.BlockSpec((CTX_LEN, hd), lambda b, h, j: (ctx_blk0 + b, COL_K + h)),
            pl.BlockSpec((CTX_LEN, hd), lambda b, h, j: (ctx_blk0 + b, COL_V + h)),
            pl.BlockSpec((SEQ, hd), lambda b, h, j: (0, 0)),
            pl.BlockSpec((SEQ, hd), lambda b, h, j: (0, 0)),
            pl.BlockSpec((2, 1, hd), lambda b, h, j: (0, 0, 0)),
            pl.BlockSpec((None, None, Q_BLK, K_BLK), lambda b, h, j: (h, _bias_pattern(j), 0, 0)),
        ],
        out_specs=pl.BlockSpec((Q_BLK, hd), lambda b, h, j: (b * lat_blk + j, h)),
        out_shape=jax.ShapeDtypeStruct((out_rows, NA_WIDTH), BF16),
        scratch_shapes=[pltpu.VMEM((SEQ, hd), BF16), pltpu.VMEM((SEQ, hd), BF16),
                        pltpu.VMEM((CTX_LEN, hd), BF16), pltpu.VMEM((CTX_LEN, hd), BF16)],
        compiler_params=_cparams(("arbitrary", "arbitrary", "arbitrary")),
        name="na_attention",
    )(proj, proj, proj, proj, proj, cos_t, sin_t, qkg, bias)


def _ctx_attn_kernel(q_ref, k_ref, v_ref, qkg_ref, prev_ref, o_ref):
    del prev_ref
    qn = _head_rms(q_ref[...], qkg_ref[0]).astype(BF16)
    kn = _head_rms(k_ref[...], qkg_ref[1]).astype(BF16)
    s = _dot_nt(qn, kn) * ATTN_SCALE
    p = jnp.exp(s - jnp.max(s, axis=-1, keepdims=True))
    denom = jnp.sum(p, axis=-1, keepdims=True)
    o = jnp.dot(p.astype(BF16), v_ref[...].astype(BF16), preferred_element_type=F32)
    o_ref[...] = (o / denom).astype(o_ref.dtype)


def _ctx_attention(proj, qkg, attn):
    ctx_blk0 = N_LAT // CTX_LEN
    hd = NA_HEAD_DIM
    return pl.pallas_call(
        _ctx_attn_kernel,
        grid=(BATCH, NA_HEADS),
        in_specs=[
            pl.BlockSpec((CTX_LEN, hd), lambda b, h: (ctx_blk0 + b, COL_Q + h)),
            pl.BlockSpec((CTX_LEN, hd), lambda b, h: (ctx_blk0 + b, COL_K + h)),
            pl.BlockSpec((CTX_LEN, hd), lambda b, h: (ctx_blk0 + b, COL_V + h)),
            pl.BlockSpec((2, 1, hd), lambda b, h: (0, 0, 0)),
            pl.BlockSpec(memory_space=pl.ANY),
        ],
        out_specs=pl.BlockSpec((CTX_LEN, hd), lambda b, h: (ctx_blk0 + b, h)),
        out_shape=jax.ShapeDtypeStruct(attn.shape, attn.dtype),
        input_output_aliases={4: 0},
        compiler_params=_cparams(("arbitrary", "arbitrary")),
        name="ctx_attention",
    )(proj, proj, proj, qkg, attn)


def _attention_bias(rpb_j):
    qcol = np.arange(GRID_W)
    c0 = np.clip(qcol - WIN_C // 2, 0, GRID_W - WIN_C)
    col_ok = (qcol[None, :] >= c0[:, None]) & (qcol[None, :] < c0[:, None] + WIN_C)
    dc = np.clip(qcol[None, :] - qcol[:, None], -(WIN_C - 1), WIN_C - 1) + (WIN_C - 1)
    dr_all, ok_all = [], []
    for jb in (0, N_QBLK // 2, N_QBLK - 1):
        r = Q_ROWS * jb + np.arange(Q_ROWS)
        k_first = np.clip(Q_ROWS * jb - WIN_R // 2, 0, GRID_ROWS - K_ROWS)
        kr = k_first + np.arange(K_ROWS)
        r0 = np.clip(r - WIN_R // 2, 0, GRID_ROWS - WIN_R)
        ok = (kr[None, :] >= r0[:, None]) & (kr[None, :] < r0[:, None] + WIN_R)
        dr = np.clip(kr[None, :] - r[:, None] + (WIN_R - 1), 0, 2 * WIN_R - 2)
        dr_all.append(dr)
        ok_all.append(ok)
    dr_idx = np.stack(dr_all)
    ok_row = np.stack(ok_all)
    vals = rpb_j.astype(F32)[:, dr_idx[:, :, None, :, None], dc[None, None, :, None, :]]
    mask = ok_row[:, :, None, :, None] & col_ok[None, None, :, None, :]
    bias = jnp.where(jnp.asarray(mask)[None], vals, -jnp.inf)
    return bias.reshape(NA_HEADS, 3, Q_BLK, K_BLK)


def _rope_tables():
    quarter = NA_HEAD_DIM // 4
    inv_freq = ROPE_BASE ** (-jnp.arange(quarter, dtype=F32) / quarter)
    t = jnp.arange(SEQ)
    ang_r = (t // GRID_W).astype(F32)[:, None] * inv_freq[None, :]
    ang_c = (t % GRID_W).astype(F32)[:, None] * inv_freq[None, :]
    cos_t = jnp.concatenate([jnp.cos(ang_r), jnp.cos(ang_r), jnp.cos(ang_c), jnp.cos(ang_c)], axis=-1)
    sin_t = jnp.concatenate([-jnp.sin(ang_r), jnp.sin(ang_r), -jnp.sin(ang_c), jnp.sin(ang_c)], axis=-1)
    return cos_t, sin_t


CONV_PAD = 8


def _conv_kernel(x_ref, w_ref, b_ref, o_ref, pad_scr, *, seq_len, chunk):
    width = x_ref.shape[-1]
    zeros = jnp.zeros((CONV_PAD, width), F32)
    pad_scr[0:CONV_PAD, :] = zeros
    pad_scr[CONV_PAD + seq_len:2 * CONV_PAD + seq_len, :] = zeros
    pad_scr[CONV_PAD:CONV_PAD + seq_len, :] = x_ref[...]

    def body(c, carry):
        base = pl.multiple_of(c * chunk, chunk)
        xp = pad_scr[pl.ds(base, chunk + 2 * CONV_PAD), :]
        acc = jnp.zeros((chunk, width), F32) + b_ref[...]
        for k in range(CONV_K):
            lo = CONV_PAD - CONV_K // 2 + k
            acc = acc + w_ref[k:k + 1, :] * xp[lo:lo + chunk, :]
        o_ref[pl.ds(base, chunk), :] = _silu(acc)
        return carry

    lax.fori_loop(0, seq_len // chunk, body, 0)


def _conv_silu(proj, conv_w8, conv_b2, j_even):
    n_rows = proj.shape[0]
    cw = 256
    xbc_col0 = COL_XBC * LANE // cw

    def call(seq_len, n_seq, row_blk0, prev):
        kern = functools.partial(_conv_kernel, seq_len=seq_len, chunk=min(seq_len, 512))
        in_specs = [
            pl.BlockSpec((seq_len, cw), lambda s, c: (row_blk0 + s, xbc_col0 + c)),
            pl.BlockSpec((None, 8, cw), lambda s, c: (j_even, 0, c)),
            pl.BlockSpec((None, 1, cw), lambda s, c: (j_even, 0, c)),
        ]
        args = [proj, conv_w8, conv_b2]
        aliases = {}
        if prev is not None:
            in_specs.append(pl.BlockSpec(memory_space=pl.ANY))
            args.append(prev)
            aliases = {3: 0}
            kern = functools.partial(_drop_last_input, kern, 3)
        return pl.pallas_call(
            kern,
            grid=(n_seq, CONV_CH // cw),
            in_specs=in_specs,
            out_specs=pl.BlockSpec((seq_len, cw), lambda s, c: (row_blk0 + s, c)),
            out_shape=jax.ShapeDtypeStruct((n_rows, CONV_CH), F32),
            scratch_shapes=[pltpu.VMEM((seq_len + 2 * CONV_PAD, cw), F32)],
            input_output_aliases=aliases,
            compiler_params=_cparams(("arbitrary", "arbitrary")),
            name="conv_silu",
        )(*args)

    act = call(SEQ, BATCH, 0, None)
    return call(CTX_LEN, BATCH, N_LAT // CTX_LEN, act)


def _drop_last_input(kern, n_in, *refs):
    return kern(*refs[:n_in], *refs[n_in + 1:])


def _softplus(x):
    return jnp.maximum(x, 0.0) + jnp.log1p(jnp.exp(-jnp.abs(x)))


def _dot_exact(a, b):
    return jnp.dot(a, b, precision=lax.Precision.HIGHEST, preferred_element_type=F32)


def _ssd_direction(xs_ref, b_ref, c_ref, dt_ref, dtb_ref, alog_ref, y_ref, h_scr, lane0, forward):
    q = SSD_CHUNK
    gw = SSD_INNER // SSD_GROUPS
    dt = _softplus(dt_ref[...] + dtb_ref[...])
    a = dt * (-jnp.exp(alog_ref[...]))
    ri = lax.broadcasted_iota(jnp.int32, (q, q), 0)
    ci = lax.broadcasted_iota(jnp.int32, (q, q), 1)
    tri = (ci <= ri) if forward else (ci >= ri)
    a_cum = _dot_exact(tri.astype(F32), a)
    a_cum_t = a_cum.T
    er = lax.broadcasted_iota(jnp.int32, (LANE, SSD_INNER), 0)
    ec = lax.broadcasted_iota(jnp.int32, (LANE, SSD_INNER), 1)
    expand = ((er - lane0) == (ec // SSD_HEAD_DIM)).astype(F32)
    dt_e = _dot_exact(dt, expand)
    ac_e = _dot_exact(a_cum, expand)
    end = q - 1 if forward else 0
    a_end_e = ac_e[end:end + 1, :]
    xdt = xs_ref[...] * dt_e
    in_decay = jnp.exp(ac_e)
    out_decay = jnp.exp(a_end_e - ac_e)
    state_decay = jnp.exp(a_end_e)
    lane = lax.broadcasted_iota(jnp.int32, (q, LANE), 1)
    for g in range(SSD_GROUPS):
        gs = slice(g * gw, (g + 1) * gw)
        bg_t = b_ref[:, g * SSD_STATE:(g + 1) * SSD_STATE].T.astype(BF16)
        cg = c_ref[:, g * SSD_STATE:(g + 1) * SSD_STATE].astype(BF16)
        cb = jnp.dot(cg, bg_t, preferred_element_type=F32)
        h_t = h_scr[:, gs]
        y_inter = jnp.dot(cg, h_t.astype(BF16), preferred_element_type=F32) * in_decay[:, gs]
        for pair in range(gw // LANE):
            cs = slice(g * gw + pair * LANE, g * gw + (pair + 1) * LANE)
            x_pair = xdt[:, cs].astype(BF16)
            res = []
            for sub in range(LANE // SSD_HEAD_DIM):
                hl = lane0 + (g * gw + pair * LANE) // SSD_HEAD_DIM + sub
                seg = a_cum[:, hl:hl + 1] - a_cum_t[hl:hl + 1, :]
                decay = jnp.exp(jnp.where(tri, seg, -jnp.inf))
                res.append(jnp.dot((cb * decay).astype(BF16), x_pair, preferred_element_type=F32))
            y_pair = jnp.where(lane < SSD_HEAD_DIM, res[0], res[1])
            y_ref[:, cs] = y_pair + y_inter[:, pair * LANE:(pair + 1) * LANE]
        x_out = (xdt[:, gs] * out_decay[:, gs]).astype(BF16)
        h_scr[:, gs] = state_decay[:, gs] * h_t + jnp.dot(bg_t, x_out, preferred_element_type=F32)


def _ssd_kernel(xs_f, b_f, c_f, dt_f, xs_b, b_b, c_b, dt_b, dtb_ref, alog_ref, yf_ref, yb_ref,
                hf_scr, hb_scr):
    @pl.when(pl.program_id(1) == 0)
    def _():
        hf_scr[...] = jnp.zeros_like(hf_scr)
        hb_scr[...] = jnp.zeros_like(hb_scr)

    _ssd_direction(xs_f, b_f, c_f, dt_f, dtb_ref, alog_ref, yf_ref, hf_scr, 0, True)
    _ssd_direction(xs_b, b_b, c_b, dt_b, dtb_ref, alog_ref, yb_ref, hb_scr, SSD_HEADS, False)


def _ssd_scan(act, proj, dtb, alog, j_even):
    q = SSD_CHUNK
    n_rows = act.shape[0]
    n_ctx = CTX_LEN // q
    n_lat = SEQ // q
    ctx_blk0 = N_LAT // q

    def fwd_blk(b, s):
        return jnp.where(s < n_ctx, ctx_blk0 + b * n_ctx + s, b * n_lat + (s - n_ctx))

    def bwd_blk(b, s):
        return jnp.where(s < n_ctx, ctx_blk0 + b * n_ctx + (n_ctx - 1 - s),
                         b * n_lat + (n_lat - 1 - (s - n_ctx)))

    gn = SSD_GROUPS * SSD_STATE
    b_col = SSD_INNER // gn
    c_col = b_col + 1

    def specs(blk):
        return [
            pl.BlockSpec((q, SSD_INNER), lambda b, s: (blk(b, s), 0)),
            pl.BlockSpec((q, gn), lambda b, s: (blk(b, s), b_col)),
            pl.BlockSpec((q, gn), lambda b, s: (blk(b, s), c_col)),
            pl.BlockSpec((q, LANE), lambda b, s: (blk(b, s), COL_DT)),
        ]

    small = pl.BlockSpec((None, 1, LANE), lambda b, s: (j_even, 0, 0))
    return pl.pallas_call(
        _ssd_kernel,
        grid=(BATCH, n_ctx + n_lat),
        in_specs=specs(fwd_blk) + specs(bwd_blk) + [small, small],
        out_specs=[pl.BlockSpec((q, SSD_INNER), lambda b, s: (fwd_blk(b, s), 0)),
                   pl.BlockSpec((q, SSD_INNER), lambda b, s: (bwd_blk(b, s), 0))],
        out_shape=[jax.ShapeDtypeStruct((n_rows, SSD_INNER), F32)] * 2,
        scratch_shapes=[pltpu.VMEM((SSD_STATE, SSD_INNER), F32)] * 2,
        compiler_params=_cparams(("arbitrary", "arbitrary")),
        name="ssd_scan",
    )(act, act, act, proj, act, act, act, proj, dtb, alog)


def _out_proj_kernel(attn_ref, yf_ref, yb_ref, xs_ref, z_ref, dsk_ref, ng_ref, w_ref, x_ref, mod_ref,
                     o_ref):
    y = yf_ref[...] + yb_ref[...] + dsk_ref[...] * xs_ref[...]
    yz = y * _silu(z_ref[...])
    mix = jnp.dot(attn_ref[...], w_ref[0:NA_WIDTH, :], preferred_element_type=F32)
    gw = SSD_INNER // SSD_GROUPS
    for g in range(SSD_GROUPS):
        seg = yz[:, g * gw:(g + 1) * gw]
        ms = jnp.mean(seg * seg, axis=-1, keepdims=True)
        yn = (seg * lax.rsqrt(ms + EPS) * ng_ref[:, g * gw:(g + 1) * gw]).astype(BF16)
        mix = mix + jnp.dot(yn, w_ref[NA_WIDTH + g * gw:NA_WIDTH + (g + 1) * gw, :],
                            preferred_element_type=F32)
    o_ref[...] = x_ref[...] + mod_ref[2] * mix


def _out_proj(attn, y_f, y_b, act, proj, dsk, ng, w_out, x, n_rows, mods_i, j_even):
    tm = 256
    mrow = _mod_row(tm)
    z_col = COL_Z * LANE // SSD_INNER
    return pl.pallas_call(
        _out_proj_kernel,
        grid=(n_rows // tm,),
        in_specs=[
            pl.BlockSpec((tm, NA_WIDTH), lambda t: (t, 0)),
            pl.BlockSpec((tm, SSD_INNER), lambda t: (t, 0)),
            pl.BlockSpec((tm, SSD_INNER), lambda t: (t, 0)),
            pl.BlockSpec((tm, SSD_INNER), lambda t: (t, 0)),
            pl.BlockSpec((tm, SSD_INNER), lambda t: (t, z_col)),
            pl.BlockSpec((None, 1, SSD_INNER), lambda t: (j_even, 0, 0)),
            pl.BlockSpec((None, 1, SSD_INNER), lambda t: (j_even, 0, 0)),
            pl.BlockSpec((None, D_MODEL, D_MODEL), lambda t: (j_even, 0, 0)),
            pl.BlockSpec((tm, D_MODEL), lambda t: (t, 0)),
            pl.BlockSpec((None, 3, 1, D_MODEL), lambda t: (mrow(t), 1, 0, 0)),
        ],
        out_specs=pl.BlockSpec((tm, D_MODEL), lambda t: (t, 0)),
        out_shape=jax.ShapeDtypeStruct((n_rows, D_MODEL), F32),
        compiler_params=_cparams(("arbitrary",)),
        name="out_proj",
    )(attn, y_f, y_b, act, proj, dsk, ng, w_out, x, mods_i)


def _fourier_chan_kernel(x_ref, mod_ref, g_ref, cs_ref, p_ref, q_ref):
    h = _mod_norm(x_ref[...], g_ref[...], mod_ref[0], mod_ref[1])
    for g in range(F_GROUPS):
        gs = slice(g * F_GROUP_CH, (g + 1) * F_GROUP_CH)
        pq = jnp.dot(h[:, gs].astype(BF16), cs_ref[...], preferred_element_type=F32)
        p_ref[:, gs] = pq[:, :F_GROUP_CH].astype(BF16)
        q_ref[:, gs] = pq[:, F_GROUP_CH:].astype(BF16)


def _fourier_chan(x, n_rows, mods_i, g, cs):
    tm = 512
    mrow = _mod_row(tm)
    return pl.pallas_call(
        _fourier_chan_kernel,
        grid=(n_rows // tm,),
        in_specs=[
            pl.BlockSpec((tm, D_MODEL), lambda t: (t, 0)),
            pl.BlockSpec((None, 3, 1, D_MODEL), lambda t: (mrow(t), 1, 0, 0)),
            pl.BlockSpec((1, D_MODEL), lambda t: (0, 0)),
            pl.BlockSpec((F_GROUP_CH, 2 * F_GROUP_CH), lambda t: (0, 0)),
        ],
        out_specs=[pl.BlockSpec((tm, D_MODEL), lambda t: (t, 0))] * 2,
        out_shape=[jax.ShapeDtypeStruct((n_rows, D_MODEL), BF16)] * 2,
        compiler_params=_cparams(("arbitrary",)),
        name="fourier_chan",
    )(x, mods_i, g, cs)


def _fourier_pos_kernel(c_ref, s_ref, p_ref, q_ref, w_ref, x_ref, mod_ref, o_ref, acc_scr, *, scale):
    k = pl.program_id(2)

    @pl.when(k == 0)
    def _():
        acc_scr[...] = jnp.zeros_like(acc_scr)

    acc_scr[...] += (jnp.dot(c_ref[...], p_ref[...], preferred_element_type=F32)
                     - jnp.dot(s_ref[...], q_ref[...], preferred_element_type=F32))

    @pl.when(k == pl.num_programs(2) - 1)
    def _():
        f = (acc_scr[...] * scale).astype(BF16)
        o_ref[...] = x_ref[...] + mod_ref[2] * jnp.dot(f, w_ref[...], preferred_element_type=F32)


def _fourier_pos(cm, sm, p, q, w, x, out_rows, mods_i, j_odd, seq_len, row0, is_ctx, prev):
    tm = tk = min(seq_len, 512)
    nm = seq_len // tm
    blk0 = row0 // tm
    scale = float((seq_len * F_GROUP_CH) ** -0.5)
    kern = functools.partial(_fourier_pos_kernel, scale=scale)
    in_specs = [
        pl.BlockSpec((tm, tk), lambda b, m, k: (m, k)),
        pl.BlockSpec((tm, tk), lambda b, m, k: (m, k)),
        pl.BlockSpec((tk, D_MODEL), lambda b, m, k: (blk0 + b * nm + k, 0)),
        pl.BlockSpec((tk, D_MODEL), lambda b, m, k: (blk0 + b * nm + k, 0)),
        pl.BlockSpec((None, D_MODEL, D_MODEL), lambda b, m, k: (j_odd, 0, 0)),
        pl.BlockSpec((tm, D_MODEL), lambda b, m, k: (blk0 + b * nm + m, 0)),
        pl.BlockSpec((None, 3, 1, D_MODEL), lambda b, m, k: (BATCH if is_ctx else b, 1, 0, 0)),
    ]
    args = [cm, sm, p, q, w, x, mods_i]
    aliases = {}
    if prev is not None:
        in_specs.append(pl.BlockSpec(memory_space=pl.ANY))
        args.append(prev)
        aliases = {7: 0}
        kern = functools.partial(_drop_last_input, kern, 7)
    return pl.pallas_call(
        kern,
        grid=(BATCH, nm, seq_len // tk),
        in_specs=in_specs,
        out_specs=pl.BlockSpec((tm, D_MODEL), lambda b, m, k: (blk0 + b * nm + m, 0)),
        out_shape=jax.ShapeDtypeStruct((out_rows, D_MODEL), F32),
        scratch_shapes=[pltpu.VMEM((tm, D_MODEL), F32)],
        input_output_aliases=aliases,
        compiler_params=_cparams(("arbitrary", "arbitrary", "arbitrary")),
        name="fourier_pos",
    )(*args)


def _dft_mats(n):
    k = jnp.arange(n, dtype=jnp.int32)
    ang = ((k[:, None] * k[None, :]) % n).astype(F32) * np.float32(2.0 * np.pi / n)
    return jnp.cos(ang), jnp.sin(ang)


def kernel(x, c, ctx, c_ctx, mod_w, mod_b, norm_g, ffn_w1, ffn_w3, ffn_w2, mix_w_in, mix_w_out, qk_g, rpb,
           conv_w, conv_b, dt_bias, a_log, ssd_d, ssd_norm_g, fourier_w):
    n_even = mix_w_in.shape[0]
    xs = jnp.concatenate([x.reshape(N_LAT, D_MODEL), ctx.reshape(BATCH * CTX_LEN, D_MODEL)], axis=0)
    mods = _modulation(c, c_ctx, mod_w, mod_b)

    w1 = ffn_w1.astype(BF16)
    w3 = ffn_w3.astype(BF16)
    w2 = ffn_w2.astype(BF16)
    w_in = jnp.pad(mix_w_in, ((0, 0), (0, 0), (0, IN_W_PAD - IN_W))).astype(BF16)
    w_out = mix_w_out.astype(BF16)
    w_f = fourier_w.astype(BF16)
    qkg = qk_g.reshape(n_even, 2, 1, NA_HEAD_DIM)
    conv_w8 = jnp.pad(conv_w, ((0, 0), (0, 8 - CONV_K), (0, 0)))
    conv_b2 = conv_b.reshape(n_even, 1, CONV_CH)
    pad_l = LANE - 2 * SSD_HEADS
    dtb = jnp.pad(dt_bias.reshape(n_even, 1, 2 * SSD_HEADS), ((0, 0), (0, 0), (0, pad_l)))
    alog = jnp.pad(a_log.reshape(n_even, 1, 2 * SSD_HEADS), ((0, 0), (0, 0), (0, pad_l)))
    dsk = jnp.repeat(ssd_d, SSD_HEAD_DIM, axis=-1).reshape(n_even, 1, SSD_INNER)
    ng = ssd_norm_g.reshape(n_even, 1, SSD_INNER)
    cos_t, sin_t = _rope_tables()
    cc, sc = _dft_mats(F_GROUP_CH)
    cs_chan = jnp.concatenate([cc, sc], axis=1).astype(BF16)
    dft_lat = [m.astype(BF16) for m in _dft_mats(SEQ)]
    dft_ctx = [m.astype(BF16) for m in _dft_mats(CTX_LEN)]

    last_ctx = ((DEPTH - 1) // 2) * 2
    for i in range(DEPTH):
        use_ctx = i <= last_ctx
        ctx_out = i < last_ctx
        j = i // 2
        mods_i = mods[i]
        g = norm_g[i].reshape(3, 1, D_MODEL)
        rows_in = N_TOK if use_ctx else N_LAT
        rows_out = N_TOK if ctx_out else N_LAT
        xs = _ffn(xs, rows_in, mods_i, 0, g[0], w1, w3, w2, i, 0)
        if i % 2 == 0:
            proj = _in_proj(xs, rows_in, mods_i, g[1], w_in, j)
            if not use_ctx:
                raise NotImplementedError("even mixer layers always see the context here")
            bias = _attention_bias(rpb[j])
            attn = _na_attention(proj, rows_out, cos_t, sin_t, qkg[j], bias)
            if ctx_out:
                attn = _ctx_attention(proj, qkg[j], attn)
            act = _conv_silu(proj, conv_w8, conv_b2, j)
            y_f, y_b = _ssd_scan(act, proj, dtb, alog, j)
            xs = _out_proj(attn, y_f, y_b, act, proj, dsk, ng, w_out, xs, rows_out, mods_i, j)
        else:
            p, q = _fourier_chan(xs, rows_out, mods_i, g[1], cs_chan)
            new = _fourier_pos(dft_lat[0], dft_lat[1], p, q, w_f, xs, rows_out, mods_i, j, SEQ, 0, False, None)
            if ctx_out:
                new = _fourier_pos(dft_ctx[0], dft_ctx[1], p, q, w_f, xs, rows_out, mods_i, j, CTX_LEN, N_LAT,
                                   True, new)
            xs = new
        xs = _ffn(xs, rows_out, mods_i, 2, g[2], w1, w3, w2, i, 1)
    return xs[:N_LAT].reshape(BATCH, SEQ, D_MODEL)
```

```python
import functools

import numpy as np
import jax
import jax.numpy as jnp
from jax import lax
from jax.experimental import pallas as pl
from jax.experimental.pallas import tpu as pltpu

F32 = jnp.float32
BF16 = jnp.bfloat16

D_MODEL = 2048
BATCH = 2
SEQ = 4096
DEPTH = 4
GRID_W = 64
GRID_ROWS = SEQ // GRID_W
CTX_LEN = 256
N_LAT = BATCH * SEQ
N_TOK = N_LAT + BATCH * CTX_LEN
NA_HEADS = 8
NA_HEAD_DIM = 128
NA_WIDTH = NA_HEADS * NA_HEAD_DIM
WIN_R = 8
WIN_C = 16
ROPE_BASE = 10000.0
SSD_HEADS = 16
SSD_HEAD_DIM = 64
SSD_INNER = SSD_HEADS * SSD_HEAD_DIM
SSD_GROUPS = 2
SSD_STATE = 128
SSD_CHUNK = 128
CONV_K = 5
CONV_CH = SSD_INNER + 2 * SSD_GROUPS * SSD_STATE
IN_W = 3 * NA_WIDTH + SSD_INNER + CONV_CH + 2 * SSD_HEADS
LANE = 128
IN_W_PAD = ((IN_W + LANE - 1) // LANE) * LANE
F_GROUPS = 4
F_GROUP_CH = D_MODEL // F_GROUPS
D_FF = 5632
N_MOD = 9
EPS = 1e-6
ATTN_SCALE = NA_HEAD_DIM ** -0.5

COL_Q = 0
COL_K = NA_WIDTH // LANE
COL_V = 2 * NA_WIDTH // LANE
COL_Z = 3 * NA_WIDTH // LANE
COL_XBC = (3 * NA_WIDTH + SSD_INNER) // LANE
COL_DT = (3 * NA_WIDTH + SSD_INNER + CONV_CH) // LANE

Q_ROWS = 8
Q_BLK = Q_ROWS * GRID_W
K_ROWS = 16
K_BLK = K_ROWS * GRID_W
N_QBLK = GRID_ROWS // Q_ROWS

VMEM_LIMIT = 56 * 1024 * 1024


def _cparams(sem):
    return pltpu.CompilerParams(dimension_semantics=sem, vmem_limit_bytes=VMEM_LIMIT)


def _sigmoid(x):
    return 1.0 / (1.0 + jnp.exp(-x))


def _silu(x):
    return x * _sigmoid(x)


def _mod_norm(x, g, shift, scale):
    ms = jnp.mean(x * x, axis=-1, keepdims=True)
    y = x * lax.rsqrt(ms + EPS) * g
    return y * (1.0 + scale) + shift


def _mod_row(tile_rows):
    def f(t):
        return jnp.minimum((t * tile_rows) // SEQ, BATCH)
    return f


def _mod_kernel(c_ref, w_ref, b_ref, o_ref):
    s = _silu(c_ref[...]).astype(BF16)
    o_ref[...] = jnp.dot(s, w_ref[...].astype(BF16), preferred_element_type=F32) + b_ref[...]


def _modulation(c, c_ctx, mod_w, mod_b):
    rows = 8
    cvec = jnp.concatenate([c, c_ctx[None, :], jnp.zeros((rows - BATCH - 1, D_MODEL), F32)], axis=0)
    n = N_MOD * D_MODEL
    tn = 1024
    out = pl.pallas_call(
        _mod_kernel,
        grid=(DEPTH, n // tn),
        in_specs=[
            pl.BlockSpec((rows, D_MODEL), lambda i, j: (0, 0)),
            pl.BlockSpec((None, D_MODEL, tn), lambda i, j: (i, 0, j)),
            pl.BlockSpec((None, 1, tn), lambda i, j: (i, 0, j)),
        ],
        out_specs=pl.BlockSpec((None, rows, tn), lambda i, j: (i, 0, j)),
        out_shape=jax.ShapeDtypeStruct((DEPTH, rows, n), F32),
        compiler_params=_cparams(("arbitrary", "arbitrary")),
        name="modulation",
    )(cvec, mod_w, mod_b.reshape(DEPTH, 1, n))
    return out[:, :BATCH + 1].reshape(DEPTH, BATCH + 1, N_MOD, 1, D_MODEL)


def _ffn_kernel(x_ref, mod_ref, g_ref, w1_ref, w3_ref, w2_ref, o_ref, h_scr, acc_scr):
    j = pl.program_id(1)

    @pl.when(j == 0)
    def _():
        h = _mod_norm(x_ref[...], g_ref[...], mod_ref[0], mod_ref[1])
        h_scr[...] = h.astype(BF16)
        acc_scr[...] = jnp.zeros_like(acc_scr)

    h = h_scr[...]
    a = jnp.dot(h, w1_ref[...], preferred_element_type=F32)
    b = jnp.dot(h, w3_ref[...], preferred_element_type=F32)
    u = (_silu(a) * b).astype(BF16)
    acc_scr[...] += jnp.dot(u, w2_ref[...], preferred_element_type=F32)

    @pl.when(j == pl.num_programs(1) - 1)
    def _():
        o_ref[...] = x_ref[...] + 0.5 * mod_ref[2] * acc_scr[...]


def _ffn(x, n_rows, mods_i, sub, g, w1, w3, w2, layer, which):
    tm, tf = 512, 512
    mrow = _mod_row(tm)
    return pl.pallas_call(
        _ffn_kernel,
        grid=(n_rows // tm, D_FF // tf),
        in_specs=[
            pl.BlockSpec((tm, D_MODEL), lambda t, j: (t, 0)),
            pl.BlockSpec((None, 3, 1, D_MODEL), lambda t, j: (mrow(t), sub, 0, 0)),
            pl.BlockSpec((1, D_MODEL), lambda t, j: (0, 0)),
            pl.BlockSpec((None, None, D_MODEL, tf), lambda t, j: (layer, which, 0, j)),
            pl.BlockSpec((None, None, D_MODEL, tf), lambda t, j: (layer, which, 0, j)),
            pl.BlockSpec((None, None, tf, D_MODEL), lambda t, j: (layer, which, j, 0)),
        ],
        out_specs=pl.BlockSpec((tm, D_MODEL), lambda t, j: (t, 0)),
        out_shape=jax.ShapeDtypeStruct((n_rows, D_MODEL), F32),
        scratch_shapes=[pltpu.VMEM((tm, D_MODEL), BF16), pltpu.VMEM((tm, D_MODEL), F32)],
        compiler_params=_cparams(("arbitrary", "arbitrary")),
        name="ffn",
    )(x, mods_i, g, w1, w3, w2)


def _norm_matmul_kernel(x_ref, mod_ref, g_ref, w_ref, o_ref, h_scr):
    @pl.when(pl.program_id(1) == 0)
    def _():
        h_scr[...] = _mod_norm(x_ref[...], g_ref[...], mod_ref[0], mod_ref[1]).astype(BF16)

    o_ref[...] = jnp.dot(h_scr[...], w_ref[...], preferred_element_type=F32)


def _in_proj(x, n_rows, mods_i, g, w_in, j_even):
    tm, tn = 512, 1152
    mrow = _mod_row(tm)
    return pl.pallas_call(
        _norm_matmul_kernel,
        grid=(n_rows // tm, IN_W_PAD // tn),
        in_specs=[
            pl.BlockSpec((tm, D_MODEL), lambda t, j: (t, 0)),
            pl.BlockSpec((None, 3, 1, D_MODEL), lambda t, j: (mrow(t), 1, 0, 0)),
            pl.BlockSpec((1, D_MODEL), lambda t, j: (0, 0)),
            pl.BlockSpec((None, D_MODEL, tn), lambda t, j: (j_even, 0, j)),
        ],
        out_specs=pl.BlockSpec((tm, tn), lambda t, j: (t, j)),
        out_shape=jax.ShapeDtypeStruct((n_rows, IN_W_PAD), F32),
        scratch_shapes=[pltpu.VMEM((tm, D_MODEL), BF16)],
        compiler_params=_cparams(("arbitrary", "arbitrary")),
        name="in_proj",
    )(x, mods_i, g, w_in)


def _head_rms(x, g):
    ms = jnp.mean(x * x, axis=-1, keepdims=True)
    return x * lax.rsqrt(ms + EPS) * g


def _rope(x, cos, sin):
    lane = lax.broadcasted_iota(jnp.int32, x.shape, 1)
    first_half = (lane & (NA_HEAD_DIM // 4)) == 0
    partner = jnp.where(first_half, pltpu.roll(x, NA_HEAD_DIM - NA_HEAD_DIM // 4, 1),
                        pltpu.roll(x, NA_HEAD_DIM // 4, 1))
    return x * cos + partner * sin


def _dot_nt(a, b):
    return lax.dot_general(a, b, (((1,), (1,)), ((), ())), preferred_element_type=F32)


def _na_kernel(q_ref, k_ref, v_ref, kc_ref, vc_ref, cos_ref, sin_ref, qkg_ref, bias_ref, o_ref,
               k_scr, v_scr, kc_scr, vc_scr):
    j = pl.program_id(2)

    @pl.when(j == 0)
    def _():
        kn = _head_rms(k_ref[...], qkg_ref[1])
        k_scr[...] = _rope(kn, cos_ref[...], sin_ref[...]).astype(BF16)
        v_scr[...] = v_ref[...].astype(BF16)
        kc_scr[...] = _head_rms(kc_ref[...], qkg_ref[1]).astype(BF16)
        vc_scr[...] = vc_ref[...].astype(BF16)

    q0 = pl.multiple_of(j * Q_BLK, Q_BLK)
    qn = _head_rms(q_ref[...], qkg_ref[0])
    qr = _rope(qn, cos_ref[pl.ds(q0, Q_BLK), :], sin_ref[pl.ds(q0, Q_BLK), :]).astype(BF16)
    k0 = pl.multiple_of(jnp.clip(Q_ROWS * j - WIN_R // 2, 0, GRID_ROWS - K_ROWS) * GRID_W, GRID_W)
    kw = k_scr[pl.ds(k0, K_BLK), :]
    vw = v_scr[pl.ds(k0, K_BLK), :]
    s_win = _dot_nt(qr, kw) * ATTN_SCALE + bias_ref[...]
    s_ctx = _dot_nt(qn.astype(BF16), kc_scr[...]) * ATTN_SCALE
    m = jnp.maximum(jnp.max(s_win, axis=-1, keepdims=True), jnp.max(s_ctx, axis=-1, keepdims=True))
    p_win = jnp.exp(s_win - m)
    p_ctx = jnp.exp(s_ctx - m)
    denom = jnp.sum(p_win, axis=-1, keepdims=True) + jnp.sum(p_ctx, axis=-1, keepdims=True)
    o = (jnp.dot(p_win.astype(BF16), vw, preferred_element_type=F32)
         + jnp.dot(p_ctx.astype(BF16), vc_scr[...], preferred_element_type=F32))
    o_ref[...] = (o / denom).astype(o_ref.dtype)


def _bias_pattern(j):
    return jnp.where(j == 0, 0, jnp.where(j == N_QBLK - 1, 2, 1))


def _na_attention(proj, out_rows, cos_t, sin_t, qkg, bias):
    lat_blk = SEQ // Q_BLK
    ctx_blk0 = N_LAT // CTX_LEN
    hd = NA_HEAD_DIM
    return pl.pallas_call(
        _na_kernel,
        grid=(BATCH, NA_HEADS, N_QBLK),
        in_specs=[
            pl.BlockSpec((Q_BLK, hd), lambda b, h, j: (b * lat_blk + j, COL_Q + h)),
            pl.BlockSpec((SEQ, hd), lambda b, h, j: (b, COL_K + h)),
            pl.BlockSpec((SEQ, hd), lambda b, h, j: (b, COL_V + h)),
            pl.BlockSpec((CTX_LEN, hd), lambda b, h, j: (ctx_blk0 + b, COL_K + h)),
            pl.BlockSpec((CTX_LEN, hd), lambda b, h, j: (ctx_blk0 + b, COL_V + h)),
            pl.BlockSpec((SEQ, hd), lambda b, h, j: (0, 0)),
            pl.BlockSpec((SEQ, hd), lambda b, h, j: (0, 0)),
            pl.BlockSpec((2, 1, hd), lambda b, h, j: (0, 0, 0)),
            pl.BlockSpec((None, None, Q_BLK, K_BLK), lambda b, h, j: (h, _bias_pattern(j), 0, 0)),
        ],
        out_specs=pl.BlockSpec((Q_BLK, hd), lambda b, h, j: (b * lat_blk + j, h)),
        out_shape=jax.ShapeDtypeStruct((out_rows, NA_WIDTH), BF16),
        scratch_shapes=[pltpu.VMEM((SEQ, hd), BF16), pltpu.VMEM((SEQ, hd), BF16),
                        pltpu.VMEM((CTX_LEN, hd), BF16), pltpu.VMEM((CTX_LEN, hd), BF16)],
        compiler_params=_cparams(("arbitrary", "arbitrary", "arbitrary")),
        name="na_attention",
    )(proj, proj, proj, proj, proj, cos_t, sin_t, qkg, bias)


def _ctx_attn_kernel(q_ref, k_ref, v_ref, qkg_ref, prev_ref, o_ref):
    del prev_ref
    qn = _head_rms(q_ref[...], qkg_ref[0]).astype(BF16)
    kn = _head_rms(k_ref[...], qkg_ref[1]).astype(BF16)
    s = _dot_nt(qn, kn) * ATTN_SCALE
    p = jnp.exp(s - jnp.max(s, axis=-1, keepdims=True))
    denom = jnp.sum(p, axis=-1, keepdims=True)
    o = jnp.dot(p.astype(BF16), v_ref[...].astype(BF16), preferred_element_type=F32)
    o_ref[...] = (o / denom).astype(o_ref.dtype)


def _ctx_attention(proj, qkg, attn):
    ctx_blk0 = N_LAT // CTX_LEN
    hd = NA_HEAD_DIM
    return pl.pallas_call(
        _ctx_attn_kernel,
        grid=(BATCH, NA_HEADS),
        in_specs=[
            pl.BlockSpec((CTX_LEN, hd), lambda b, h: (ctx_blk0 + b, COL_Q + h)),
            pl.BlockSpec((CTX_LEN, hd), lambda b, h: (ctx_blk0 + b, COL_K + h)),
            pl.BlockSpec((CTX_LEN, hd), lambda b, h: (ctx_blk0 + b, COL_V + h)),
            pl.BlockSpec((2, 1, hd), lambda b, h: (0, 0, 0)),
            pl.BlockSpec(memory_space=pl.ANY),
        ],
        out_specs=pl.BlockSpec((CTX_LEN, hd), lambda b, h: (ctx_blk0 + b, h)),
        out_shape=jax.ShapeDtypeStruct(attn.shape, attn.dtype),
        input_output_aliases={4: 0},
        compiler_params=_cparams(("arbitrary", "arbitrary")),
        name="ctx_attention",
    )(proj, proj, proj, qkg, attn)


def _attention_bias(rpb_j):
    n_dr = 2 * WIN_R - 1
    qcol = np.arange(GRID_W)
    c0 = np.clip(qcol - WIN_C // 2, 0, GRID_W - WIN_C)
    col_ok = (qcol[None, :] >= c0[:, None]) & (qcol[None, :] < c0[:, None] + WIN_C)
    r = rpb_j.astype(F32)
    edge = GRID_W - WIN_C
    ep = jnp.concatenate([jnp.repeat(r[..., :1], edge, axis=-1), r, jnp.repeat(r[..., -1:], edge, axis=-1)],
                         axis=-1)
    t1 = jnp.stack([ep[..., GRID_W - 1 - qc:2 * GRID_W - 1 - qc] for qc in range(GRID_W)], axis=2)
    t1 = jnp.where(jnp.asarray(col_ok)[None, None], t1, -jnp.inf)
    band = t1.transpose(0, 2, 1, 3).reshape(NA_HEADS, GRID_W, n_dr * GRID_W)
    pad_w = Q_ROWS * GRID_W
    band = jnp.pad(band, ((0, 0), (0, 0), (pad_w, pad_w)), constant_values=-jnp.inf)
    strips, masks = [], []
    for jb in (0, N_QBLK // 2, N_QBLK - 1):
        k_first = int(np.clip(Q_ROWS * jb - WIN_R // 2, 0, GRID_ROWS - K_ROWS))
        kr = k_first + np.arange(K_ROWS)
        for a in range(Q_ROWS):
            row = Q_ROWS * jb + a
            r0 = int(np.clip(row - WIN_R // 2, 0, GRID_ROWS - WIN_R))
            off = pad_w + (k_first - row + WIN_R - 1) * GRID_W
            strips.append(band[:, :, off:off + K_BLK])
            masks.append(np.repeat((kr >= r0) & (kr < r0 + WIN_R), GRID_W))
    vals = jnp.stack(strips, axis=1)
    mask = np.stack(masks)[None, :, None, :]
    bias = jnp.where(jnp.asarray(mask), vals, -jnp.inf)
    return bias.reshape(NA_HEADS, 3, Q_BLK, K_BLK)


def _rope_tables():
    quarter = NA_HEAD_DIM // 4
    inv_freq = ROPE_BASE ** (-jnp.arange(quarter, dtype=F32) / quarter)
    t = jnp.arange(SEQ)
    ang_r = (t // GRID_W).astype(F32)[:, None] * inv_freq[None, :]
    ang_c = (t % GRID_W).astype(F32)[:, None] * inv_freq[None, :]
    cos_t = jnp.concatenate([jnp.cos(ang_r), jnp.cos(ang_r), jnp.cos(ang_c), jnp.cos(ang_c)], axis=-1)
    sin_t = jnp.concatenate([-jnp.sin(ang_r), jnp.sin(ang_r), -jnp.sin(ang_c), jnp.sin(ang_c)], axis=-1)
    return cos_t, sin_t


CONV_PAD = 8


def _conv_kernel(x_ref, w_ref, b_ref, o_ref, pad_scr, *, seq_len, chunk):
    width = x_ref.shape[-1]
    zeros = jnp.zeros((CONV_PAD, width), F32)
    pad_scr[0:CONV_PAD, :] = zeros
    pad_scr[CONV_PAD + seq_len:2 * CONV_PAD + seq_len, :] = zeros
    pad_scr[CONV_PAD:CONV_PAD + seq_len, :] = x_ref[...]

    def body(c, carry):
        base = pl.multiple_of(c * chunk, chunk)
        xp = pad_scr[pl.ds(base, chunk + 2 * CONV_PAD), :]
        acc = jnp.zeros((chunk, width), F32) + b_ref[...]
        for k in range(CONV_K):
            lo = CONV_PAD - CONV_K // 2 + k
            acc = acc + w_ref[k:k + 1, :] * xp[lo:lo + chunk, :]
        o_ref[pl.ds(base, chunk), :] = _silu(acc)
        return carry

    lax.fori_loop(0, seq_len // chunk, body, 0)


def _conv_silu(proj, conv_w8, conv_b2, j_even):
    n_rows = proj.shape[0]
    cw = 256
    xbc_col0 = COL_XBC * LANE // cw

    def call(seq_len, n_seq, row_blk0, prev):
        kern = functools.partial(_conv_kernel, seq_len=seq_len, chunk=min(seq_len, 512))
        in_specs = [
            pl.BlockSpec((seq_len, cw), lambda s, c: (row_blk0 + s, xbc_col0 + c)),
            pl.BlockSpec((None, 8, cw), lambda s, c: (j_even, 0, c)),
            pl.BlockSpec((None, 1, cw), lambda s, c: (j_even, 0, c)),
        ]
        args = [proj, conv_w8, conv_b2]
        aliases = {}
        if prev is not None:
            in_specs.append(pl.BlockSpec(memory_space=pl.ANY))
            args.append(prev)
            aliases = {3: 0}
            kern = functools.partial(_drop_last_input, kern, 3)
        return pl.pallas_call(
            kern,
            grid=(n_seq, CONV_CH // cw),
            in_specs=in_specs,
            out_specs=pl.BlockSpec((seq_len, cw), lambda s, c: (row_blk0 + s, c)),
            out_shape=jax.ShapeDtypeStruct((n_rows, CONV_CH), F32),
            scratch_shapes=[pltpu.VMEM((seq_len + 2 * CONV_PAD, cw), F32)],
            input_output_aliases=aliases,
            compiler_params=_cparams(("arbitrary", "arbitrary")),
            name="conv_silu",
        )(*args)

    act = call(SEQ, BATCH, 0, None)
    return call(CTX_LEN, BATCH, N_LAT // CTX_LEN, act)


def _drop_last_input(kern, n_in, *refs):
    return kern(*refs[:n_in], *refs[n_in + 1:])


def _softplus(x):
    return jnp.maximum(x, 0.0) + jnp.log1p(jnp.exp(-jnp.abs(x)))


def _dot_exact(a, b):
    return jnp.dot(a, b, precision=lax.Precision.HIGHEST, preferred_element_type=F32)


def _ssd_direction(xs_ref, b_ref, c_ref, dt_ref, dtb_ref, alog_ref, y_ref, h_scr, lane0, forward):
    q = SSD_CHUNK
    gw = SSD_INNER // SSD_GROUPS
    dt = _softplus(dt_ref[...] + dtb_ref[...])
    a = dt * (-jnp.exp(alog_ref[...]))
    ri = lax.broadcasted_iota(jnp.int32, (q, q), 0)
    ci = lax.broadcasted_iota(jnp.int32, (q, q), 1)
    tri = (ci <= ri) if forward else (ci >= ri)
    a_cum = _dot_exact(tri.astype(F32), a)
    a_cum_t = a_cum.T
    er = lax.broadcasted_iota(jnp.int32, (LANE, SSD_INNER), 0)
    ec = lax.broadcasted_iota(jnp.int32, (LANE, SSD_INNER), 1)
    expand = ((er - lane0) == (ec // SSD_HEAD_DIM)).astype(F32)
    dt_e = _dot_exact(dt, expand)
    ac_e = _dot_exact(a_cum, expand)
    end = q - 1 if forward else 0
    a_end_e = ac_e[end:end + 1, :]
    xdt = xs_ref[...] * dt_e
    in_decay = jnp.exp(ac_e)
    out_decay = jnp.exp(a_end_e - ac_e)
    state_decay = jnp.exp(a_end_e)
    lane = lax.broadcasted_iota(jnp.int32, (q, LANE), 1)
    for g in range(SSD_GROUPS):
        gs = slice(g * gw, (g + 1) * gw)
        bg_t = b_ref[:, g * SSD_STATE:(g + 1) * SSD_STATE].T.astype(BF16)
        cg = c_ref[:, g * SSD_STATE:(g + 1) * SSD_STATE].astype(BF16)
        cb = jnp.dot(cg, bg_t, preferred_element_type=F32)
        h_t = h_scr[:, gs]
        y_inter = jnp.dot(cg, h_t.astype(BF16), preferred_element_type=F32) * in_decay[:, gs]
        for pair in range(gw // LANE):
            cs = slice(g * gw + pair * LANE, g * gw + (pair + 1) * LANE)
            x_pair = xdt[:, cs].astype(BF16)
            res = []
            for sub in range(LANE // SSD_HEAD_DIM):
                hl = lane0 + (g * gw + pair * LANE) // SSD_HEAD_DIM + sub
                seg = a_cum[:, hl:hl + 1] - a_cum_t[hl:hl + 1, :]
                decay = jnp.exp(jnp.where(tri, seg, -jnp.inf))
                res.append(jnp.dot((cb * decay).astype(BF16), x_pair, preferred_element_type=F32))
            y_pair = jnp.where(lane < SSD_HEAD_DIM, res[0], res[1])
            y_ref[:, cs] = y_pair + y_inter[:, pair * LANE:(pair + 1) * LANE]
        x_out = (xdt[:, gs] * out_decay[:, gs]).astype(BF16)
        h_scr[:, gs] = state_decay[:, gs] * h_t + jnp.dot(bg_t, x_out, preferred_element_type=F32)


def _ssd_kernel(xs_f, b_f, c_f, dt_f, xs_b, b_b, c_b, dt_b, dtb_ref, alog_ref, yf_ref, yb_ref,
                hf_scr, hb_scr):
    @pl.when(pl.program_id(1) == 0)
    def _():
        hf_scr[...] = jnp.zeros_like(hf_scr)
        hb_scr[...] = jnp.zeros_like(hb_scr)

    _ssd_direction(xs_f, b_f, c_f, dt_f, dtb_ref, alog_ref, yf_ref, hf_scr, 0, True)
    _ssd_direction(xs_b, b_b, c_b, dt_b, dtb_ref, alog_ref, yb_ref, hb_scr, SSD_HEADS, False)


def _ssd_scan(act, proj, dtb, alog, j_even):
    q = SSD_CHUNK
    n_rows = act.shape[0]
    n_ctx = CTX_LEN // q
    n_lat = SEQ // q
    ctx_blk0 = N_LAT // q

    def fwd_blk(b, s):
        return jnp.where(s < n_ctx, ctx_blk0 + b * n_ctx + s, b * n_lat + (s - n_ctx))

    def bwd_blk(b, s):
        return jnp.where(s < n_ctx, ctx_blk0 + b * n_ctx + (n_ctx - 1 - s),
                         b * n_lat + (n_lat - 1 - (s - n_ctx)))

    gn = SSD_GROUPS * SSD_STATE
    b_col = SSD_INNER // gn
    c_col = b_col + 1

    def specs(blk):
        return [
            pl.BlockSpec((q, SSD_INNER), lambda b, s: (blk(b, s), 0)),
            pl.BlockSpec((q, gn), lambda b, s: (blk(b, s), b_col)),
            pl.BlockSpec((q, gn), lambda b, s: (blk(b, s), c_col)),
            pl.BlockSpec((q, LANE), lambda b, s: (blk(b, s), COL_DT)),
        ]

    small = pl.BlockSpec((None, 1, LANE), lambda b, s: (j_even, 0, 0))
    return pl.pallas_call(
        _ssd_kernel,
        grid=(BATCH, n_ctx + n_lat),
        in_specs=specs(fwd_blk) + specs(bwd_blk) + [small, small],
        out_specs=[pl.BlockSpec((q, SSD_INNER), lambda b, s: (fwd_blk(b, s), 0)),
                   pl.BlockSpec((q, SSD_INNER), lambda b, s: (bwd_blk(b, s), 0))],
        out_shape=[jax.ShapeDtypeStruct((n_rows, SSD_INNER), F32)] * 2,
        scratch_shapes=[pltpu.VMEM((SSD_STATE, SSD_INNER), F32)] * 2,
        compiler_params=_cparams(("arbitrary", "arbitrary")),
        name="ssd_scan",
    )(act, act, act, proj, act, act, act, proj, dtb, alog)


def _out_proj_kernel(attn_ref, yf_ref, yb_ref, xs_ref, z_ref, dsk_ref, ng_ref, w_ref, x_ref, mod_ref,
                     o_ref):
    y = yf_ref[...] + yb_ref[...] + dsk_ref[...] * xs_ref[...]
    yz = y * _silu(z_ref[...])
    mix = jnp.dot(attn_ref[...], w_ref[0:NA_WIDTH, :], preferred_element_type=F32)
    gw = SSD_INNER // SSD_GROUPS
    for g in range(SSD_GROUPS):
        seg = yz[:, g * gw:(g + 1) * gw]
        ms = jnp.mean(seg * seg, axis=-1, keepdims=True)
        yn = (seg * lax.rsqrt(ms + EPS) * ng_ref[:, g * gw:(g + 1) * gw]).astype(BF16)
        mix = mix + jnp.dot(yn, w_ref[NA_WIDTH + g * gw:NA_WIDTH + (g + 1) * gw, :],
                            preferred_element_type=F32)
    o_ref[...] = x_ref[...] + mod_ref[2] * mix


def _out_proj(attn, y_f, y_b, act, proj, dsk, ng, w_out, x, n_rows, mods_i, j_even):
    tm = 256
    mrow = _mod_row(tm)
    z_col = COL_Z * LANE // SSD_INNER
    return pl.pallas_call(
        _out_proj_kernel,
        grid=(n_rows // tm,),
        in_specs=[
            pl.BlockSpec((tm, NA_WIDTH), lambda t: (t, 0)),
            pl.BlockSpec((tm, SSD_INNER), lambda t: (t, 0)),
            pl.BlockSpec((tm, SSD_INNER), lambda t: (t, 0)),
            pl.BlockSpec((tm, SSD_INNER), lambda t: (t, 0)),
            pl.BlockSpec((tm, SSD_INNER), lambda t: (t, z_col)),
            pl.BlockSpec((None, 1, SSD_INNER), lambda t: (j_even, 0, 0)),
            pl.BlockSpec((None, 1, SSD_INNER), lambda t: (j_even, 0, 0)),
            pl.BlockSpec((None, D_MODEL, D_MODEL), lambda t: (j_even, 0, 0)),
            pl.BlockSpec((tm, D_MODEL), lambda t: (t, 0)),
            pl.BlockSpec((None, 3, 1, D_MODEL), lambda t: (mrow(t), 1, 0, 0)),
        ],
        out_specs=pl.BlockSpec((tm, D_MODEL), lambda t: (t, 0)),
        out_shape=jax.ShapeDtypeStruct((n_rows, D_MODEL), F32),
        compiler_params=_cparams(("arbitrary",)),
        name="out_proj",
    )(attn, y_f, y_b, act, proj, dsk, ng, w_out, x, mods_i)


def _fourier_chan_kernel(x_ref, mod_ref, g_ref, cs_ref, p_ref, q_ref):
    h = _mod_norm(x_ref[...], g_ref[...], mod_ref[0], mod_ref[1])
    for g in range(F_GROUPS):
        gs = slice(g * F_GROUP_CH, (g + 1) * F_GROUP_CH)
        pq = jnp.dot(h[:, gs].astype(BF16), cs_ref[...], preferred_element_type=F32)
        p_ref[:, gs] = pq[:, :F_GROUP_CH].astype(BF16)
        q_ref[:, gs] = pq[:, F_GROUP_CH:].astype(BF16)


def _fourier_chan(x, n_rows, mods_i, g, cs):
    tm = 512
    mrow = _mod_row(tm)
    return pl.pallas_call(
        _fourier_chan_kernel,
        grid=(n_rows // tm,),
        in_specs=[
            pl.BlockSpec((tm, D_MODEL), lambda t: (t, 0)),
            pl.BlockSpec((None, 3, 1, D_MODEL), lambda t: (mrow(t), 1, 0, 0)),
            pl.BlockSpec((1, D_MODEL), lambda t: (0, 0)),
            pl.BlockSpec((F_GROUP_CH, 2 * F_GROUP_CH), lambda t: (0, 0)),
        ],
        out_specs=[pl.BlockSpec((tm, D_MODEL), lambda t: (t, 0))] * 2,
        out_shape=[jax.ShapeDtypeStruct((n_rows, D_MODEL), BF16)] * 2,
        compiler_params=_cparams(("arbitrary",)),
        name="fourier_chan",
    )(x, mods_i, g, cs)


def _fourier_pos_kernel(c_ref, s_ref, p_ref, q_ref, w_ref, x_ref, mod_ref, o_ref, acc_scr, *, scale):
    k = pl.program_id(2)

    @pl.when(k == 0)
    def _():
        acc_scr[...] = jnp.zeros_like(acc_scr)

    acc_scr[...] += (jnp.dot(c_ref[...], p_ref[...], preferred_element_type=F32)
                     - jnp.dot(s_ref[...], q_ref[...], preferred_element_type=F32))

    @pl.when(k == pl.num_programs(2) - 1)
    def _():
        f = (acc_scr[...] * scale).astype(BF16)
        o_ref[...] = x_ref[...] + mod_ref[2] * jnp.dot(f, w_ref[...], preferred_element_type=F32)


def _fourier_pos(cm, sm, p, q, w, x, out_rows, mods_i, j_odd, seq_len, row0, is_ctx, prev):
    tm = tk = min(seq_len, 512)
    nm = seq_len // tm
    blk0 = row0 // tm
    scale = float((seq_len * F_GROUP_CH) ** -0.5)
    kern = functools.partial(_fourier_pos_kernel, scale=scale)
    in_specs = [
        pl.BlockSpec((tm, tk), lambda b, m, k: (m, k)),
        pl.BlockSpec((tm, tk), lambda b, m, k: (m, k)),
        pl.BlockSpec((tk, D_MODEL), lambda b, m, k: (blk0 + b * nm + k, 0)),
        pl.BlockSpec((tk, D_MODEL), lambda b, m, k: (blk0 + b * nm + k, 0)),
        pl.BlockSpec((None, D_MODEL, D_MODEL), lambda b, m, k: (j_odd, 0, 0)),
        pl.BlockSpec((tm, D_MODEL), lambda b, m, k: (blk0 + b * nm + m, 0)),
        pl.BlockSpec((None, 3, 1, D_MODEL), lambda b, m, k: (BATCH if is_ctx else b, 1, 0, 0)),
    ]
    args = [cm, sm, p, q, w, x, mods_i]
    aliases = {}
    if prev is not None:
        in_specs.append(pl.BlockSpec(memory_space=pl.ANY))
        args.append(prev)
        aliases = {7: 0}
        kern = functools.partial(_drop_last_input, kern, 7)
    return pl.pallas_call(
        kern,
        grid=(BATCH, nm, seq_len // tk),
        in_specs=in_specs,
        out_specs=pl.BlockSpec((tm, D_MODEL), lambda b, m, k: (blk0 + b * nm + m, 0)),
        out_shape=jax.ShapeDtypeStruct((out_rows, D_MODEL), F32),
        scratch_shapes=[pltpu.VMEM((tm, D_MODEL), F32)],
        input_output_aliases=aliases,
        compiler_params=_cparams(("arbitrary", "arbitrary", "arbitrary")),
        name="fourier_pos",
    )(*args)


def _dft_mats(n):
    k = jnp.arange(n, dtype=jnp.int32)
    ang = ((k[:, None] * k[None, :]) % n).astype(F32) * np.float32(2.0 * np.pi / n)
    return jnp.cos(ang), jnp.sin(ang)


def kernel(x, c, ctx, c_ctx, mod_w, mod_b, norm_g, ffn_w1, ffn_w3, ffn_w2, mix_w_in, mix_w_out, qk_g, rpb,
           conv_w, conv_b, dt_bias, a_log, ssd_d, ssd_norm_g, fourier_w):
    n_even = mix_w_in.shape[0]
    xs = jnp.concatenate([x.reshape(N_LAT, D_MODEL), ctx.reshape(BATCH * CTX_LEN, D_MODEL)], axis=0)
    mods = _modulation(c, c_ctx, mod_w, mod_b)

    w1 = ffn_w1.astype(BF16)
    w3 = ffn_w3.astype(BF16)
    w2 = ffn_w2.astype(BF16)
    w_in = jnp.pad(mix_w_in, ((0, 0), (0, 0), (0, IN_W_PAD - IN_W))).astype(BF16)
    w_out = mix_w_out.astype(BF16)
    w_f = fourier_w.astype(BF16)
    qkg = qk_g.reshape(n_even, 2, 1, NA_HEAD_DIM)
    conv_w8 = jnp.pad(conv_w, ((0, 0), (0, 8 - CONV_K), (0, 0)))
    conv_b2 = conv_b.reshape(n_even, 1, CONV_CH)
    pad_l = LANE - 2 * SSD_HEADS
    dtb = jnp.pad(dt_bias.reshape(n_even, 1, 2 * SSD_HEADS), ((0, 0), (0, 0), (0, pad_l)))
    alog = jnp.pad(a_log.reshape(n_even, 1, 2 * SSD_HEADS), ((0, 0), (0, 0), (0, pad_l)))
    dsk = jnp.repeat(ssd_d, SSD_HEAD_DIM, axis=-1).reshape(n_even, 1, SSD_INNER)
    ng = ssd_norm_g.reshape(n_even, 1, SSD_INNER)
    cos_t, sin_t = _rope_tables()
    cc, sc = _dft_mats(F_GROUP_CH)
    cs_chan = jnp.concatenate([cc, sc], axis=1).astype(BF16)
    dft_lat = [m.astype(BF16) for m in _dft_mats(SEQ)]
    dft_ctx = [m.astype(BF16) for m in _dft_mats(CTX_LEN)]

    last_ctx = ((DEPTH - 1) // 2) * 2
    for i in range(DEPTH):
        use_ctx = i <= last_ctx
        ctx_out = i < last_ctx
        j = i // 2
        mods_i = mods[i]
        g = norm_g[i].reshape(3, 1, D_MODEL)
        rows_in = N_TOK if use_ctx else N_LAT
        rows_out = N_TOK if ctx_out else N_LAT
        xs = _ffn(xs, rows_in, mods_i, 0, g[0], w1, w3, w2, i, 0)
        if i % 2 == 0:
            proj = _in_proj(xs, rows_in, mods_i, g[1], w_in, j)
            if not use_ctx:
                raise NotImplementedError("even mixer layers always see the context here")
            bias = _attention_bias(rpb[j])
            attn = _na_attention(proj, rows_out, cos_t, sin_t, qkg[j], bias)
            if ctx_out:
                attn = _ctx_attention(proj, qkg[j], attn)
            act = _conv_silu(proj, conv_w8, conv_b2, j)
            y_f, y_b = _ssd_scan(act, proj, dtb, alog, j)
            xs = _out_proj(attn, y_f, y_b, act, proj, dsk, ng, w_out, xs, rows_out, mods_i, j)
        else:
            p, q = _fourier_chan(xs, rows_out, mods_i, g[1], cs_chan)
            new = _fourier_pos(dft_lat[0], dft_lat[1], p, q, w_f, xs, rows_out, mods_i, j, SEQ, 0, False, None)
            if ctx_out:
                new = _fourier_pos(dft_ctx[0], dft_ctx[1], p, q, w_f, xs, rows_out, mods_i, j, CTX_LEN, N_LAT,
                                   True, new)
            xs = new
        xs = _ffn(xs, rows_out, mods_i, 2, g[2], w1, w3, w2, i, 1)
    return xs[:N_LAT].reshape(BATCH, SEQ, D_MODEL)
```

```python
import functools

import numpy as np
import jax
import jax.numpy as jnp
from jax import lax
from jax.experimental import pallas as pl
from jax.experimental.pallas import tpu as pltpu

F32 = jnp.float32
BF16 = jnp.bfloat16

D_MODEL = 2048
BATCH = 2
SEQ = 4096
DEPTH = 4
GRID_W = 64
GRID_ROWS = SEQ // GRID_W
CTX_LEN = 256
N_LAT = BATCH * SEQ
N_TOK = N_LAT + BATCH * CTX_LEN
NA_HEADS = 8
NA_HEAD_DIM = 128
NA_WIDTH = NA_HEADS * NA_HEAD_DIM
WIN_R = 8
WIN_C = 16
ROPE_BASE = 10000.0
SSD_HEADS = 16
SSD_HEAD_DIM = 64
SSD_INNER = SSD_HEADS * SSD_HEAD_DIM
SSD_GROUPS = 2
SSD_STATE = 128
SSD_CHUNK = 128
CONV_K = 5
CONV_CH = SSD_INNER + 2 * SSD_GROUPS * SSD_STATE
IN_W = 3 * NA_WIDTH + SSD_INNER + CONV_CH + 2 * SSD_HEADS
LANE = 128
IN_W_PAD = ((IN_W + LANE - 1) // LANE) * LANE
F_GROUPS = 4
F_GROUP_CH = D_MODEL // F_GROUPS
D_FF = 5632
N_MOD = 9
EPS = 1e-6
ATTN_SCALE = NA_HEAD_DIM ** -0.5

COL_Q = 0
COL_K = NA_WIDTH // LANE
COL_V = 2 * NA_WIDTH // LANE
COL_Z = 3 * NA_WIDTH // LANE
COL_XBC = (3 * NA_WIDTH + SSD_INNER) // LANE
COL_DT = (3 * NA_WIDTH + SSD_INNER + CONV_CH) // LANE

Q_ROWS = 8
Q_BLK = Q_ROWS * GRID_W
K_ROWS = 16
K_BLK = K_ROWS * GRID_W
N_QBLK = GRID_ROWS // Q_ROWS

VMEM_LIMIT = 56 * 1024 * 1024


def _cparams(sem):
    return pltpu.CompilerParams(dimension_semantics=sem, vmem_limit_bytes=VMEM_LIMIT)


def _sigmoid(x):
    return 1.0 / (1.0 + jnp.exp(-x))


def _silu(x):
    return x * _sigmoid(x)


def _mod_norm(x, g, shift, scale):
    ms = jnp.mean(x * x, axis=-1, keepdims=True)
    y = x * lax.rsqrt(ms + EPS) * g
    return y * (1.0 + scale) + shift


def _mod_row(tile_rows):
    def f(t):
        return jnp.minimum((t * tile_rows) // SEQ, BATCH)
    return f


def _mod_kernel(c_ref, w_ref, b_ref, o_ref):
    s = _silu(c_ref[...]).astype(BF16)
    o_ref[...] = jnp.dot(s, w_ref[...].astype(BF16), preferred_element_type=F32) + b_ref[...]


def _modulation(c, c_ctx, mod_w, mod_b):
    rows = 8
    cvec = jnp.concatenate([c, c_ctx[None, :], jnp.zeros((rows - BATCH - 1, D_MODEL), F32)], axis=0)
    n = N_MOD * D_MODEL
    tn = 1024
    out = pl.pallas_call(
        _mod_kernel,
        grid=(DEPTH, n // tn),
        in_specs=[
            pl.BlockSpec((rows, D_MODEL), lambda i, j: (0, 0)),
            pl.BlockSpec((None, D_MODEL, tn), lambda i, j: (i, 0, j)),
            pl.BlockSpec((None, 1, tn), lambda i, j: (i, 0, j)),
        ],
        out_specs=pl.BlockSpec((None, rows, tn), lambda i, j: (i, 0, j)),
        out_shape=jax.ShapeDtypeStruct((DEPTH, rows, n), F32),
        compiler_params=_cparams(("arbitrary", "arbitrary")),
        name="modulation",
    )(cvec, mod_w, mod_b.reshape(DEPTH, 1, n))
    return out[:, :BATCH + 1].reshape(DEPTH, BATCH + 1, N_MOD, 1, D_MODEL)


def _ffn_kernel(x_ref, mod_ref, g_ref, w1_ref, w3_ref, w2_ref, o_ref, h_scr, acc_scr):
    j = pl.program_id(1)

    @pl.when(j == 0)
    def _():
        h = _mod_norm(x_ref[...], g_ref[...], mod_ref[0], mod_ref[1])
        h_scr[...] = h.astype(BF16)
        acc_scr[...] = jnp.zeros_like(acc_scr)

    h = h_scr[...]
    a = jnp.dot(h, w1_ref[...], preferred_element_type=F32)
    b = jnp.dot(h, w3_ref[...], preferred_element_type=F32)
    u = (_silu(a) * b).astype(BF16)
    acc_scr[...] += jnp.dot(u, w2_ref[...], preferred_element_type=F32)

    @pl.when(j == pl.num_programs(1) - 1)
    def _():
        o_ref[...] = x_ref[...] + 0.5 * mod_ref[2] * acc_scr[...]


def _ffn(x, n_rows, mods_i, sub, g, w1, w3, w2, layer, which):
    tm, tf = 512, 512
    mrow = _mod_row(tm)
    return pl.pallas_call(
        _ffn_kernel,
        grid=(n_rows // tm, D_FF // tf),
        in_specs=[
            pl.BlockSpec((tm, D_MODEL), lambda t, j: (t, 0)),
            pl.BlockSpec((None, 3, 1, D_MODEL), lambda t, j: (mrow(t), sub, 0, 0)),
            pl.BlockSpec((1, D_MODEL), lambda t, j: (0, 0)),
            pl.BlockSpec((None, None, D_MODEL, tf), lambda t, j: (layer, which, 0, j)),
            pl.BlockSpec((None, None, D_MODEL, tf), lambda t, j: (layer, which, 0, j)),
            pl.BlockSpec((None, None, tf, D_MODEL), lambda t, j: (layer, which, j, 0)),
        ],
        out_specs=pl.BlockSpec((tm, D_MODEL), lambda t, j: (t, 0)),
        out_shape=jax.ShapeDtypeStruct((n_rows, D_MODEL), F32),
        scratch_shapes=[pltpu.VMEM((tm, D_MODEL), BF16), pltpu.VMEM((tm, D_MODEL), F32)],
        compiler_params=_cparams(("arbitrary", "arbitrary")),
        name="ffn",
    )(x, mods_i, g, w1, w3, w2)


def _norm_matmul_kernel(x_ref, mod_ref, g_ref, w_ref, o_ref, h_scr):
    @pl.when(pl.program_id(1) == 0)
    def _():
        h_scr[...] = _mod_norm(x_ref[...], g_ref[...], mod_ref[0], mod_ref[1]).astype(BF16)

    o_ref[...] = jnp.dot(h_scr[...], w_ref[...], preferred_element_type=F32)


def _in_proj(x, n_rows, mods_i, g, w_in, j_even):
    tm, tn = 512, 1152
    mrow = _mod_row(tm)
    return pl.pallas_call(
        _norm_matmul_kernel,
        grid=(n_rows // tm, IN_W_PAD // tn),
        in_specs=[
            pl.BlockSpec((tm, D_MODEL), lambda t, j: (t, 0)),
            pl.BlockSpec((None, 3, 1, D_MODEL), lambda t, j: (mrow(t), 1, 0, 0)),
            pl.BlockSpec((1, D_MODEL), lambda t, j: (0, 0)),
            pl.BlockSpec((None, D_MODEL, tn), lambda t, j: (j_even, 0, j)),
        ],
        out_specs=pl.BlockSpec((tm, tn), lambda t, j: (t, j)),
        out_shape=jax.ShapeDtypeStruct((n_rows, IN_W_PAD), F32),
        scratch_shapes=[pltpu.VMEM((tm, D_MODEL), BF16)],
        compiler_params=_cparams(("arbitrary", "arbitrary")),
        name="in_proj",
    )(x, mods_i, g, w_in)


def _head_rms(x, g):
    ms = jnp.mean(x * x, axis=-1, keepdims=True)
    return x * lax.rsqrt(ms + EPS) * g


def _rope(x, cos, sin):
    lane = lax.broadcasted_iota(jnp.int32, x.shape, 1)
    first_half = (lane & (NA_HEAD_DIM // 4)) == 0
    partner = jnp.where(first_half, pltpu.roll(x, NA_HEAD_DIM - NA_HEAD_DIM // 4, 1),
                        pltpu.roll(x, NA_HEAD_DIM // 4, 1))
    return x * cos + partner * sin


def _dot_nt(a, b):
    return lax.dot_general(a, b, (((1,), (1,)), ((), ())), preferred_element_type=F32)


def _softmax_pv(scores, values):
    m = functools.reduce(jnp.maximum, [jnp.max(s, axis=-1, keepdims=True) for s in scores])
    ps = [jnp.exp(s - m) for s in scores]
    denom = functools.reduce(jnp.add, [jnp.sum(p, axis=-1, keepdims=True) for p in ps])
    o = functools.reduce(jnp.add, [jnp.dot(p.astype(BF16), v, preferred_element_type=F32)
                                   for p, v in zip(ps, values)])
    return o / denom


Q_SPLIT = 2


def _na_kernel(q_ref, k_ref, v_ref, qc_ref, kc_ref, vc_ref, cos_ref, sin_ref, qkg_ref, bias_ref, o_ref, oc_ref,
               k_scr, v_scr, kc_scr, vc_scr):
    j = pl.program_id(2)

    @pl.when(j == 0)
    def _():
        kn = _head_rms(k_ref[...], qkg_ref[1])
        k_scr[...] = _rope(kn, cos_ref[...], sin_ref[...]).astype(BF16)
        v_scr[...] = v_ref[...].astype(BF16)
        kcn = _head_rms(kc_ref[...], qkg_ref[1]).astype(BF16)
        vcb = vc_ref[...].astype(BF16)
        kc_scr[...] = kcn
        vc_scr[...] = vcb
        qcn = _head_rms(qc_ref[...], qkg_ref[0]).astype(BF16)
        oc_ref[...] = _softmax_pv([_dot_nt(qcn, kcn) * ATTN_SCALE], [vcb]).astype(oc_ref.dtype)

    k0 = pl.multiple_of(jnp.clip(Q_ROWS * j - WIN_R // 2, 0, GRID_ROWS - K_ROWS) * GRID_W, GRID_W)
    kw = k_scr[pl.ds(k0, K_BLK), :]
    vw = v_scr[pl.ds(k0, K_BLK), :]
    sub = Q_BLK // Q_SPLIT
    for part in range(Q_SPLIT):
        rows = pl.ds(part * sub, sub)
        q0 = pl.multiple_of(j * Q_BLK + part * sub, sub)
        qn = _head_rms(q_ref[rows, :], qkg_ref[0])
        qr = _rope(qn, cos_ref[pl.ds(q0, sub), :], sin_ref[pl.ds(q0, sub), :]).astype(BF16)
        s_win = _dot_nt(qr, kw) * ATTN_SCALE + bias_ref[rows, :]
        s_ctx = _dot_nt(qn.astype(BF16), kc_scr[...]) * ATTN_SCALE
        o_ref[rows, :] = _softmax_pv([s_win, s_ctx], [vw, vc_scr[...]]).astype(o_ref.dtype)


def _bias_pattern(j):
    return jnp.where(j == 0, 0, jnp.where(j == N_QBLK - 1, 2, 1))


def _na_attention(proj, cos_t, sin_t, qkg, bias):
    lat_blk = SEQ // Q_BLK
    ctx_blk0 = N_LAT // CTX_LEN
    hd = NA_HEAD_DIM
    return pl.pallas_call(
        _na_kernel,
        grid=(BATCH, NA_HEADS, N_QBLK),
        in_specs=[
            pl.BlockSpec((Q_BLK, hd), lambda b, h, j: (b * lat_blk + j, COL_Q + h)),
            pl.BlockSpec((SEQ, hd), lambda b, h, j: (b, COL_K + h)),
            pl.BlockSpec((SEQ, hd), lambda b, h, j: (b, COL_V + h)),
            pl.BlockSpec((CTX_LEN, hd), lambda b, h, j: (ctx_blk0 + b, COL_Q + h)),
            pl.BlockSpec((CTX_LEN, hd), lambda b, h, j: (ctx_blk0 + b, COL_K + h)),
            pl.BlockSpec((CTX_LEN, hd), lambda b, h, j: (ctx_blk0 + b, COL_V + h)),
            pl.BlockSpec((SEQ, hd), lambda b, h, j: (0, 0)),
            pl.BlockSpec((SEQ, hd), lambda b, h, j: (0, 0)),
            pl.BlockSpec((2, 1, hd), lambda b, h, j: (0, 0, 0)),
            pl.BlockSpec((None, None, Q_BLK, K_BLK), lambda b, h, j: (h, _bias_pattern(j), 0, 0)),
        ],
        out_specs=[pl.BlockSpec((Q_BLK, hd), lambda b, h, j: (b * lat_blk + j, h)),
                   pl.BlockSpec((CTX_LEN, hd), lambda b, h, j: (b, h))],
        out_shape=[jax.ShapeDtypeStruct((N_LAT, NA_WIDTH), BF16),
                   jax.ShapeDtypeStruct((BATCH * CTX_LEN, NA_WIDTH), BF16)],
        scratch_shapes=[pltpu.VMEM((SEQ, hd), BF16), pltpu.VMEM((SEQ, hd), BF16),
                        pltpu.VMEM((CTX_LEN, hd), BF16), pltpu.VMEM((CTX_LEN, hd), BF16)],
        compiler_params=_cparams(("arbitrary", "arbitrary", "arbitrary")),
        name="na_attention",
    )(proj, proj, proj, proj, proj, proj, cos_t, sin_t, qkg, bias)


def _attention_bias(rpb_j):
    n_dr = 2 * WIN_R - 1
    qcol = np.arange(GRID_W)
    c0 = np.clip(qcol - WIN_C // 2, 0, GRID_W - WIN_C)
    col_ok = (qcol[None, :] >= c0[:, None]) & (qcol[None, :] < c0[:, None] + WIN_C)
    r = rpb_j.astype(F32)
    edge = GRID_W - WIN_C
    ep = jnp.concatenate([jnp.repeat(r[..., :1], edge, axis=-1), r, jnp.repeat(r[..., -1:], edge, axis=-1)],
                         axis=-1)
    t1 = jnp.stack([ep[..., GRID_W - 1 - qc:2 * GRID_W - 1 - qc] for qc in range(GRID_W)], axis=2)
    t1 = jnp.where(jnp.asarray(col_ok)[None, None], t1, -jnp.inf)
    band = t1.transpose(0, 2, 1, 3).reshape(NA_HEADS, GRID_W, n_dr * GRID_W)
    pad_w = Q_ROWS * GRID_W
    band = jnp.pad(band, ((0, 0), (0, 0), (pad_w, pad_w)), constant_values=-jnp.inf)
    strips, masks = [], []
    for jb in (0, N_QBLK // 2, N_QBLK - 1):
        k_first = int(np.clip(Q_ROWS * jb - WIN_R // 2, 0, GRID_ROWS - K_ROWS))
        kr = k_first + np.arange(K_ROWS)
        for a in range(Q_ROWS):
            row = Q_ROWS * jb + a
            r0 = int(np.clip(row - WIN_R // 2, 0, GRID_ROWS - WIN_R))
            off = pad_w + (k_first - row + WIN_R - 1) * GRID_W
            strips.append(band[:, :, off:off + K_BLK])
            masks.append(np.repeat((kr >= r0) & (kr < r0 + WIN_R), GRID_W))
    vals = jnp.stack(strips, axis=1)
    mask = np.stack(masks)[None, :, None, :]
    bias = jnp.where(jnp.asarray(mask), vals, -jnp.inf)
    return bias.reshape(NA_HEADS, 3, Q_BLK, K_BLK)


def _rope_tables():
    quarter = NA_HEAD_DIM // 4
    inv_freq = ROPE_BASE ** (-jnp.arange(quarter, dtype=F32) / quarter)
    t = jnp.arange(SEQ)
    ang_r = (t // GRID_W).astype(F32)[:, None] * inv_freq[None, :]
    ang_c = (t % GRID_W).astype(F32)[:, None] * inv_freq[None, :]
    cos_t = jnp.concatenate([jnp.cos(ang_r), jnp.cos(ang_r), jnp.cos(ang_c), jnp.cos(ang_c)], axis=-1)
    sin_t = jnp.concatenate([-jnp.sin(ang_r), jnp.sin(ang_r), -jnp.sin(ang_c), jnp.sin(ang_c)], axis=-1)
    return cos_t, sin_t


SUBLANE = 8
CONV_HALO = SUBLANE
CONV_BLK = 256


def _conv_kernel(x_ref, prev_ref, next_ref, w_ref, b_ref, o_ref, pad_scr):
    t = pl.program_id(0)
    n_lat_blk = N_LAT // CONV_BLK
    is_ctx = t >= n_lat_blk
    seq_blks = jnp.where(is_ctx, CTX_LEN // CONV_BLK, SEQ // CONV_BLK)
    pos = jnp.where(is_ctx, t - n_lat_blk, t) % seq_blks
    pad_scr[0:CONV_HALO, :] = jnp.where(pos == 0, 0.0, prev_ref[...])
    pad_scr[CONV_HALO:CONV_HALO + CONV_BLK, :] = x_ref[...]
    pad_scr[CONV_HALO + CONV_BLK:2 * CONV_HALO + CONV_BLK, :] = jnp.where(pos == seq_blks - 1, 0.0,
                                                                           next_ref[...])
    xp = pad_scr[...]
    acc = jnp.zeros(x_ref.shape, F32) + b_ref[...]
    for k in range(CONV_K):
        lo = CONV_HALO - CONV_K // 2 + k
        acc = acc + w_ref[k:k + 1, :] * xp[lo:lo + CONV_BLK, :]
    o_ref[...] = _silu(acc)


def _conv_silu(proj, conv_w8, conv_b2, j_even):
    n_rows = proj.shape[0]
    cw = 512
    xbc_col0 = COL_XBC * LANE // cw
    halo_per_blk = CONV_BLK // CONV_HALO
    last_halo = n_rows // CONV_HALO - 1
    return pl.pallas_call(
        _conv_kernel,
        grid=(n_rows // CONV_BLK, CONV_CH // cw),
        in_specs=[
            pl.BlockSpec((CONV_BLK, cw), lambda t, c: (t, xbc_col0 + c)),
            pl.BlockSpec((CONV_HALO, cw), lambda t, c: (jnp.maximum(t * halo_per_blk - 1, 0), xbc_col0 + c)),
            pl.BlockSpec((CONV_HALO, cw),
                         lambda t, c: (jnp.minimum((t + 1) * halo_per_blk, last_halo), xbc_col0 + c)),
            pl.BlockSpec((None, SUBLANE, cw), lambda t, c: (j_even, 0, c)),
            pl.BlockSpec((None, 1, cw), lambda t, c: (j_even, 0, c)),
        ],
        out_specs=pl.BlockSpec((CONV_BLK, cw), lambda t, c: (t, c)),
        out_shape=jax.ShapeDtypeStruct((n_rows, CONV_CH), F32),
        scratch_shapes=[pltpu.VMEM((CONV_BLK + 2 * CONV_HALO, cw), F32)],
        compiler_params=_cparams(("arbitrary", "arbitrary")),
        name="conv_silu",
    )(proj, proj, proj, conv_w8, conv_b2)


def _softplus(x):
    return jnp.maximum(x, 0.0) + jnp.log1p(jnp.exp(-jnp.abs(x)))


def _dot_exact(a, b):
    return jnp.dot(a, b, precision=lax.Precision.HIGHEST, preferred_element_type=F32)


def _ssd_direction(xs_ref, b_ref, c_ref, dt_ref, dtb_ref, alog_ref, y_ref, h_scr, lane0, forward):
    q = SSD_CHUNK
    gw = SSD_INNER // SSD_GROUPS
    dt = _softplus(dt_ref[...] + dtb_ref[...])
    a = dt * (-jnp.exp(alog_ref[...]))
    ri = lax.broadcasted_iota(jnp.int32, (q, q), 0)
    ci = lax.broadcasted_iota(jnp.int32, (q, q), 1)
    tri = (ci <= ri) if forward else (ci >= ri)
    a_cum = _dot_exact(tri.astype(F32), a)
    a_cum_t = a_cum.T
    er = lax.broadcasted_iota(jnp.int32, (LANE, SSD_INNER), 0)
    ec = lax.broadcasted_iota(jnp.int32, (LANE, SSD_INNER), 1)
    expand = ((er - lane0) == (ec // SSD_HEAD_DIM)).astype(F32)
    dt_e = _dot_exact(dt, expand)
    ac_e = _dot_exact(a_cum, expand)
    end = q - 1 if forward else 0
    a_end_e = ac_e[end:end + 1, :]
    xdt = xs_ref[...] * dt_e
    in_decay = jnp.exp(ac_e)
    out_decay = jnp.exp(a_end_e - ac_e)
    state_decay = jnp.exp(a_end_e)
    lane = lax.broadcasted_iota(jnp.int32, (q, LANE), 1)
    for g in range(SSD_GROUPS):
        gs = slice(g * gw, (g + 1) * gw)
        bg_t = b_ref[:, g * SSD_STATE:(g + 1) * SSD_STATE].T.astype(BF16)
        cg = c_ref[:, g * SSD_STATE:(g + 1) * SSD_STATE].astype(BF16)
        cb = jnp.dot(cg, bg_t, preferred_element_type=F32)
        h_t = h_scr[:, gs]
        y_inter = jnp.dot(cg, h_t.astype(BF16), preferred_element_type=F32) * in_decay[:, gs]
        for pair in range(gw // LANE):
            cs = slice(g * gw + pair * LANE, g * gw + (pair + 1) * LANE)
            x_pair = xdt[:, cs].astype(BF16)
            res = []
            for sub in range(LANE // SSD_HEAD_DIM):
                hl = lane0 + (g * gw + pair * LANE) // SSD_HEAD_DIM + sub
                seg = a_cum[:, hl:hl + 1] - a_cum_t[hl:hl + 1, :]
                decay = jnp.exp(jnp.where(tri, seg, -jnp.inf))
                res.append(jnp.dot((cb * decay).astype(BF16), x_pair, preferred_element_type=F32))
            y_pair = jnp.where(lane < SSD_HEAD_DIM, res[0], res[1])
            y_ref[:, cs] = y_pair + y_inter[:, pair * LANE:(pair + 1) * LANE]
        x_out = (xdt[:, gs] * out_decay[:, gs]).astype(BF16)
        h_scr[:, gs] = state_decay[:, gs] * h_t + jnp.dot(bg_t, x_out, preferred_element_type=F32)


def _ssd_kernel(xs_f, b_f, c_f, dt_f, xs_b, b_b, c_b, dt_b, dtb_ref, alog_ref, yf_ref, yb_ref,
                hf_scr, hb_scr):
    @pl.when(pl.program_id(1) == 0)
    def _():
        hf_scr[...] = jnp.zeros_like(hf_scr)
        hb_scr[...] = jnp.zeros_like(hb_scr)

    _ssd_direction(xs_f, b_f, c_f, dt_f, dtb_ref, alog_ref, yf_ref, hf_scr, 0, True)
    _ssd_direction(xs_b, b_b, c_b, dt_b, dtb_ref, alog_ref, yb_ref, hb_scr, SSD_HEADS, False)


def _ssd_scan(act, proj, dtb, alog, j_even):
    q = SSD_CHUNK
    n_rows = act.shape[0]
    n_ctx = CTX_LEN // q
    n_lat = SEQ // q
    ctx_blk0 = N_LAT // q

    def fwd_blk(b, s):
        return jnp.where(s < n_ctx, ctx_blk0 + b * n_ctx + s, b * n_lat + (s - n_ctx))

    def bwd_blk(b, s):
        return jnp.where(s < n_ctx, ctx_blk0 + b * n_ctx + (n_ctx - 1 - s),
                         b * n_lat + (n_lat - 1 - (s - n_ctx)))

    gn = SSD_GROUPS * SSD_STATE
    b_col = SSD_INNER // gn
    c_col = b_col + 1

    def specs(blk):
        return [
            pl.BlockSpec((q, SSD_INNER), lambda b, s: (blk(b, s), 0)),
            pl.BlockSpec((q, gn), lambda b, s: (blk(b, s), b_col)),
            pl.BlockSpec((q, gn), lambda b, s: (blk(b, s), c_col)),
            pl.BlockSpec((q, LANE), lambda b, s: (blk(b, s), COL_DT)),
        ]

    small = pl.BlockSpec((None, 1, LANE), lambda b, s: (j_even, 0, 0))
    return pl.pallas_call(
        _ssd_kernel,
        grid=(BATCH, n_ctx + n_lat),
        in_specs=specs(fwd_blk) + specs(bwd_blk) + [small, small],
        out_specs=[pl.BlockSpec((q, SSD_INNER), lambda b, s: (fwd_blk(b, s), 0)),
                   pl.BlockSpec((q, SSD_INNER), lambda b, s: (bwd_blk(b, s), 0))],
        out_shape=[jax.ShapeDtypeStruct((n_rows, SSD_INNER), F32)] * 2,
        scratch_shapes=[pltpu.VMEM((SSD_STATE, SSD_INNER), F32)] * 2,
        compiler_params=_cparams(("arbitrary", "arbitrary")),
        name="ssd_scan",
    )(act, act, act, proj, act, act, act, proj, dtb, alog)


def _out_proj_kernel(attn_ref, yf_ref, yb_ref, xs_ref, z_ref, dsk_ref, ng_ref, w_ref, x_ref, mod_ref,
                     o_ref):
    y = yf_ref[...] + yb_ref[...] + dsk_ref[...] * xs_ref[...]
    yz = y * _silu(z_ref[...])
    mix = jnp.dot(attn_ref[...], w_ref[0:NA_WIDTH, :], preferred_element_type=F32)
    gw = SSD_INNER // SSD_GROUPS
    for g in range(SSD_GROUPS):
        seg = yz[:, g * gw:(g + 1) * gw]
        ms = jnp.mean(seg * seg, axis=-1, keepdims=True)
        yn = (seg * lax.rsqrt(ms + EPS) * ng_ref[:, g * gw:(g + 1) * gw]).astype(BF16)
        mix = mix + jnp.dot(yn, w_ref[NA_WIDTH + g * gw:NA_WIDTH + (g + 1) * gw, :],
                            preferred_element_type=F32)
    o_ref[...] = x_ref[...] + mod_ref[2] * mix


def _out_proj(attn, y_f, y_b, act, proj, dsk, ng, w_out, x, n_rows, mods_i, j_even):
    tm = 256
    mrow = _mod_row(tm)
    z_col = COL_Z * LANE // SSD_INNER
    return pl.pallas_call(
        _out_proj_kernel,
        grid=(n_rows // tm,),
        in_specs=[
            pl.BlockSpec((tm, NA_WIDTH), lambda t: (t, 0)),
            pl.BlockSpec((tm, SSD_INNER), lambda t: (t, 0)),
            pl.BlockSpec((tm, SSD_INNER), lambda t: (t, 0)),
            pl.BlockSpec((tm, SSD_INNER), lambda t: (t, 0)),
            pl.BlockSpec((tm, SSD_INNER), lambda t: (t, z_col)),
            pl.BlockSpec((None, 1, SSD_INNER), lambda t: (j_even, 0, 0)),
            pl.BlockSpec((None, 1, SSD_INNER), lambda t: (j_even, 0, 0)),
            pl.BlockSpec((None, D_MODEL, D_MODEL), lambda t: (j_even, 0, 0)),
            pl.BlockSpec((tm, D_MODEL), lambda t: (t, 0)),
            pl.BlockSpec((None, 3, 1, D_MODEL), lambda t: (mrow(t), 1, 0, 0)),
        ],
        out_specs=pl.BlockSpec((tm, D_MODEL), lambda t: (t, 0)),
        out_shape=jax.ShapeDtypeStruct((n_rows, D_MODEL), F32),
        compiler_params=_cparams(("arbitrary",)),
        name="out_proj",
    )(attn, y_f, y_b, act, proj, dsk, ng, w_out, x, mods_i)


def _fourier_chan_kernel(x_ref, mod_ref, g_ref, cs_ref, p_ref, q_ref):
    h = _mod_norm(x_ref[...], g_ref[...], mod_ref[0], mod_ref[1])
    for g in range(F_GROUPS):
        gs = slice(g * F_GROUP_CH, (g + 1) * F_GROUP_CH)
        pq = jnp.dot(h[:, gs].astype(BF16), cs_ref[...], preferred_element_type=F32)
        p_ref[:, gs] = pq[:, :F_GROUP_CH].astype(BF16)
        q_ref[:, gs] = pq[:, F_GROUP_CH:].astype(BF16)


def _fourier_chan(x, n_rows, mods_i, g, cs):
    tm = 512
    mrow = _mod_row(tm)
    return pl.pallas_call(
        _fourier_chan_kernel,
        grid=(n_rows // tm,),
        in_specs=[
            pl.BlockSpec((tm, D_MODEL), lambda t: (t, 0)),
            pl.BlockSpec((None, 3, 1, D_MODEL), lambda t: (mrow(t), 1, 0, 0)),
            pl.BlockSpec((1, D_MODEL), lambda t: (0, 0)),
            pl.BlockSpec((F_GROUP_CH, 2 * F_GROUP_CH), lambda t: (0, 0)),
        ],
        out_specs=[pl.BlockSpec((tm, D_MODEL), lambda t: (t, 0))] * 2,
        out_shape=[jax.ShapeDtypeStruct((n_rows, D_MODEL), BF16)] * 2,
        compiler_params=_cparams(("arbitrary",)),
        name="fourier_chan",
    )(x, mods_i, g, cs)


DFT_SPLIT = 64


def _fourier_pos_kernel(u_ref, v_ref, p_ref, q_ref, w_ref, x_ref, mod_ref, o_ref, acc_scr, c_scr, s_scr, *,
                        scale):
    k = pl.program_id(2)

    @pl.when(k == 0)
    def _():
        acc_scr[...] = jnp.zeros_like(acc_scr)

    vr, vi = v_ref[0], v_ref[1]
    for a in range(u_ref.shape[1]):
        ur, ui = u_ref[0, a:a + 1, :], u_ref[1, a:a + 1, :]
        rows = slice(a * DFT_SPLIT, (a + 1) * DFT_SPLIT)
        c_scr[rows, :] = (ur * vr - ui * vi).astype(BF16)
        s_scr[rows, :] = (ui * vr + ur * vi).astype(BF16)
    acc_scr[...] += (jnp.dot(c_scr[...], p_ref[...], preferred_element_type=F32)
                     - jnp.dot(s_scr[...], q_ref[...], preferred_element_type=F32))

    @pl.when(k == pl.num_programs(2) - 1)
    def _():
        f = (acc_scr[...] * scale).astype(BF16)
        o_ref[...] = x_ref[...] + mod_ref[2] * jnp.dot(f, w_ref[...], preferred_element_type=F32)


def _fourier_pos(tabs, p, q, w, x, mods_i, j_odd, seq_len, row0, is_ctx):
    u_tab, v_tab = tabs
    tm = tk = min(seq_len, 512)
    nm = seq_len // tm
    blk0 = row0 // tm
    scale = float((seq_len * F_GROUP_CH) ** -0.5)
    return pl.pallas_call(
        functools.partial(_fourier_pos_kernel, scale=scale),
        grid=(BATCH, nm, seq_len // tk),
        in_specs=[
            pl.BlockSpec((2, tm // DFT_SPLIT, tk), lambda b, m, k: (0, m, k)),
            pl.BlockSpec((2, DFT_SPLIT, tk), lambda b, m, k: (0, 0, k)),
            pl.BlockSpec((tk, D_MODEL), lambda b, m, k: (blk0 + b * nm + k, 0)),
            pl.BlockSpec((tk, D_MODEL), lambda b, m, k: (blk0 + b * nm + k, 0)),
            pl.BlockSpec((None, D_MODEL, D_MODEL), lambda b, m, k: (j_odd, 0, 0)),
            pl.BlockSpec((tm, D_MODEL), lambda b, m, k: (blk0 + b * nm + m, 0)),
            pl.BlockSpec((None, 3, 1, D_MODEL), lambda b, m, k: (BATCH if is_ctx else b, 1, 0, 0)),
        ],
        out_specs=pl.BlockSpec((tm, D_MODEL), lambda b, m, k: (b * nm + m, 0)),
        out_shape=jax.ShapeDtypeStruct((BATCH * seq_len, D_MODEL), F32),
        scratch_shapes=[pltpu.VMEM((tm, D_MODEL), F32), pltpu.VMEM((tm, tk), BF16), pltpu.VMEM((tm, tk), BF16)],
        compiler_params=_cparams(("arbitrary", "arbitrary", "arbitrary")),
        name="fourier_pos",
    )(u_tab, v_tab, p, q, w, x, mods_i)


def _phase_table(freq, n):
    l = jnp.arange(n, dtype=jnp.int32)
    ang = ((freq[:, None] * l[None, :]) % n).astype(F32) * np.float32(2.0 * np.pi / n)
    return jnp.stack([jnp.cos(ang), jnp.sin(ang)])


def _dft_tables(n):
    a = jnp.arange(n // DFT_SPLIT, dtype=jnp.int32) * DFT_SPLIT
    b = jnp.arange(DFT_SPLIT, dtype=jnp.int32)
    return _phase_table(a, n), _phase_table(b, n)


def kernel(x, c, ctx, c_ctx, mod_w, mod_b, norm_g, ffn_w1, ffn_w3, ffn_w2, mix_w_in, mix_w_out, qk_g, rpb,
           conv_w, conv_b, dt_bias, a_log, ssd_d, ssd_norm_g, fourier_w):
    n_even = mix_w_in.shape[0]
    xs = jnp.concatenate([x.reshape(N_LAT, D_MODEL), ctx.reshape(BATCH * CTX_LEN, D_MODEL)], axis=0)
    mods = _modulation(c, c_ctx, mod_w, mod_b)

    w1 = ffn_w1.astype(BF16)
    w3 = ffn_w3.astype(BF16)
    w2 = ffn_w2.astype(BF16)
    w_in = jnp.pad(mix_w_in, ((0, 0), (0, 0), (0, IN_W_PAD - IN_W))).astype(BF16)
    w_out = mix_w_out.astype(BF16)
    w_f = fourier_w.astype(BF16)
    qkg = qk_g.reshape(n_even, 2, 1, NA_HEAD_DIM)
    conv_w8 = jnp.pad(conv_w, ((0, 0), (0, SUBLANE - CONV_K), (0, 0)))
    conv_b2 = conv_b.reshape(n_even, 1, CONV_CH)
    pad_l = LANE - 2 * SSD_HEADS
    dtb = jnp.pad(dt_bias.reshape(n_even, 1, 2 * SSD_HEADS), ((0, 0), (0, 0), (0, pad_l)))
    alog = jnp.pad(a_log.reshape(n_even, 1, 2 * SSD_HEADS), ((0, 0), (0, 0), (0, pad_l)))
    dsk = jnp.repeat(ssd_d, SSD_HEAD_DIM, axis=-1).reshape(n_even, 1, SSD_INNER)
    ng = ssd_norm_g.reshape(n_even, 1, SSD_INNER)
    cos_t, sin_t = _rope_tables()
    chan = _phase_table(jnp.arange(F_GROUP_CH, dtype=jnp.int32), F_GROUP_CH)
    cs_chan = jnp.concatenate([chan[0], chan[1]], axis=1).astype(BF16)
    dft_lat = _dft_tables(SEQ)
    dft_ctx = _dft_tables(CTX_LEN)

    last_ctx = ((DEPTH - 1) // 2) * 2
    for i in range(DEPTH):
        use_ctx = i <= last_ctx
        ctx_out = i < last_ctx
        j = i // 2
        mods_i = mods[i]
        g = norm_g[i].reshape(3, 1, D_MODEL)
        rows_in = N_TOK if use_ctx else N_LAT
        rows_out = N_TOK if ctx_out else N_LAT
        xs = _ffn(xs, rows_in, mods_i, 0, g[0], w1, w3, w2, i, 0)
        if i % 2 == 0:
            assert use_ctx, "even mixer layers read the context keys and SSD states"
            proj = _in_proj(xs, rows_in, mods_i, g[1], w_in, j)
            bias = _attention_bias(rpb[j])
            attn, attn_ctx = _na_attention(proj, cos_t, sin_t, qkg[j], bias)
            if ctx_out:
                attn = jnp.concatenate([attn, attn_ctx], axis=0)
            act = _conv_silu(proj, conv_w8, conv_b2, j)
            y_f, y_b = _ssd_scan(act, proj, dtb, alog, j)
            xs = _out_proj(attn, y_f, y_b, act, proj, dsk, ng, w_out, xs, rows_out, mods_i, j)
        else:
            p, q = _fourier_chan(xs, rows_out, mods_i, g[1], cs_chan)
            new = _fourier_pos(dft_lat, p, q, w_f, xs, mods_i, j, SEQ, 0, False)
            if ctx_out:
                new_ctx = _fourier_pos(dft_ctx, p, q, w_f, xs, mods_i, j, CTX_LEN, N_LAT, True)
                new = jnp.concatenate([new, new_ctx], axis=0)
            xs = new
        xs = _ffn(xs, rows_out, mods_i, 2, g[2], w1, w3, w2, i, 1)
    return xs[:N_LAT].reshape(BATCH, SEQ, D_MODEL)
```

```python
import functools

import numpy as np
import jax
import jax.numpy as jnp
from jax import lax
from jax.experimental import pallas as pl
from jax.experimental.pallas import tpu as pltpu

F32 = jnp.float32
BF16 = jnp.bfloat16

D_MODEL = 2048
BATCH = 2
SEQ = 4096
DEPTH = 4
GRID_W = 64
GRID_ROWS = SEQ // GRID_W
CTX_LEN = 256
N_LAT = BATCH * SEQ
N_TOK = N_LAT + BATCH * CTX_LEN
NA_HEADS = 8
NA_HEAD_DIM = 128
NA_WIDTH = NA_HEADS * NA_HEAD_DIM
WIN_R = 8
WIN_C = 16
ROPE_BASE = 10000.0
SSD_HEADS = 16
SSD_HEAD_DIM = 64
SSD_INNER = SSD_HEADS * SSD_HEAD_DIM
SSD_GROUPS = 2
SSD_STATE = 128
SSD_CHUNK = 128
CONV_K = 5
CONV_CH = SSD_INNER + 2 * SSD_GROUPS * SSD_STATE
IN_W = 3 * NA_WIDTH + SSD_INNER + CONV_CH + 2 * SSD_HEADS
LANE = 128
IN_W_PAD = ((IN_W + LANE - 1) // LANE) * LANE
F_GROUPS = 4
F_GROUP_CH = D_MODEL // F_GROUPS
D_FF = 5632
N_MOD = 9
EPS = 1e-6
ATTN_SCALE = NA_HEAD_DIM ** -0.5

COL_XBC = 0
COL_Z = CONV_CH // LANE
COL_Q = (CONV_CH + SSD_INNER) // LANE
COL_K = COL_Q + NA_WIDTH // LANE
COL_V = COL_K + NA_WIDTH // LANE
COL_DT = COL_V + NA_WIDTH // LANE

Q_ROWS = 8
Q_BLK = Q_ROWS * GRID_W
K_ROWS = 16
K_BLK = K_ROWS * GRID_W
N_QBLK = GRID_ROWS // Q_ROWS

VMEM_LIMIT = 56 * 1024 * 1024


def _cparams(sem, vmem_limit=VMEM_LIMIT):
    return pltpu.CompilerParams(dimension_semantics=sem, vmem_limit_bytes=vmem_limit)


def _sigmoid(x):
    return 1.0 / (1.0 + jnp.exp(-x))


def _silu(x):
    return x * _sigmoid(x)


def _mod_norm(x, g, shift, scale):
    ms = jnp.mean(x * x, axis=-1, keepdims=True)
    y = x * lax.rsqrt(ms + EPS) * g
    return y * (1.0 + scale) + shift


def _mod_row(tile_rows):
    def f(t):
        return jnp.minimum((t * tile_rows) // SEQ, BATCH)
    return f


def _mod_kernel(c_ref, w_ref, b_ref, o_ref):
    s = _silu(c_ref[...]).astype(BF16)
    o_ref[...] = jnp.dot(s, w_ref[...].astype(BF16), preferred_element_type=F32) + b_ref[...]


def _modulation(c, c_ctx, mod_w, mod_b):
    rows = 8
    cvec = jnp.concatenate([c, c_ctx[None, :], jnp.zeros((rows - BATCH - 1, D_MODEL), F32)], axis=0)
    n = N_MOD * D_MODEL
    tn = 1024
    out = pl.pallas_call(
        _mod_kernel,
        grid=(DEPTH, n // tn),
        in_specs=[
            pl.BlockSpec((rows, D_MODEL), lambda i, j: (0, 0)),
            pl.BlockSpec((None, D_MODEL, tn), lambda i, j: (i, 0, j)),
            pl.BlockSpec((None, 1, tn), lambda i, j: (i, 0, j)),
        ],
        out_specs=pl.BlockSpec((None, rows, tn), lambda i, j: (i, 0, j)),
        out_shape=jax.ShapeDtypeStruct((DEPTH, rows, n), F32),
        compiler_params=_cparams(("arbitrary", "arbitrary")),
        name="modulation",
    )(cvec, mod_w, mod_b.reshape(DEPTH, 1, n))
    return out[:, :BATCH + 1].reshape(DEPTH, BATCH + 1, N_MOD, 1, D_MODEL)


FFN_TM = 1024
FFN_TF = 256
FFN_VMEM_LIMIT = 61 * 1024 * 1024


def _ffn_accumulate(h_scr, w1c, w3c, w2c, o_ref, rows):
    h = h_scr[0:rows, :]
    a = jnp.dot(h, w1c, preferred_element_type=F32)
    b = jnp.dot(h, w3c, preferred_element_type=F32)
    u = (_silu(a) * b).astype(BF16)
    o_ref[0:rows, :] += jnp.dot(u, w2c, preferred_element_type=F32)


def _ffn_kernel(x_ref, mod_ref, g_ref, w1_ref, w3_ref, w2_ref, o_ref, h_scr, *, tail_rows):
    t = pl.program_id(0)
    j = pl.program_id(1)
    last_t = pl.num_programs(0) - 1

    @pl.when(j == 0)
    def _():
        h_scr[...] = _mod_norm(x_ref[...], g_ref[...], mod_ref[0], mod_ref[1]).astype(BF16)
        o_ref[...] = jnp.zeros_like(o_ref)

    w1c = w1_ref[...].astype(BF16)
    w3c = w3_ref[...].astype(BF16)
    w2c = w2_ref[...].astype(BF16)
    if tail_rows == FFN_TM:
        _ffn_accumulate(h_scr, w1c, w3c, w2c, o_ref, FFN_TM)
    else:
        @pl.when(t < last_t)
        def _():
            _ffn_accumulate(h_scr, w1c, w3c, w2c, o_ref, FFN_TM)

        @pl.when(t == last_t)
        def _():
            _ffn_accumulate(h_scr, w1c, w3c, w2c, o_ref, tail_rows)

    @pl.when(j == pl.num_programs(1) - 1)
    def _():
        o_ref[...] = x_ref[...] + 0.5 * mod_ref[2] * o_ref[...]


def _ffn(x, n_rows, mods_i, sub, g, w1, w3, w2, layer, which):
    tm, tf = FFN_TM, FFN_TF
    mrow = _mod_row(tm)
    nt = pl.cdiv(n_rows, tm)
    tail_rows = n_rows - (nt - 1) * tm
    return pl.pallas_call(
        functools.partial(_ffn_kernel, tail_rows=tail_rows),
        grid=(nt, D_FF // tf),
        in_specs=[
            pl.BlockSpec((tm, D_MODEL), lambda t, j: (t, 0)),
            pl.BlockSpec((None, 3, 1, D_MODEL), lambda t, j: (mrow(t), sub, 0, 0)),
            pl.BlockSpec((1, D_MODEL), lambda t, j: (0, 0)),
            pl.BlockSpec((None, None, D_MODEL, tf), lambda t, j: (layer, which, 0, j)),
            pl.BlockSpec((None, None, D_MODEL, tf), lambda t, j: (layer, which, 0, j)),
            pl.BlockSpec((None, None, tf, D_MODEL), lambda t, j: (layer, which, j, 0)),
        ],
        out_specs=pl.BlockSpec((tm, D_MODEL), lambda t, j: (t, 0)),
        out_shape=jax.ShapeDtypeStruct((n_rows, D_MODEL), F32),
        scratch_shapes=[pltpu.VMEM((tm, D_MODEL), BF16)],
        compiler_params=_cparams(("arbitrary", "arbitrary"), FFN_VMEM_LIMIT),
        name="ffn",
    )(x, mods_i, g, w1, w3, w2)


def _norm_matmul_kernel(x_ref, mod_ref, g_ref, w_ref, o_ref, h_scr):
    @pl.when(pl.program_id(1) == 0)
    def _():
        h_scr[...] = _mod_norm(x_ref[...], g_ref[...], mod_ref[0], mod_ref[1]).astype(BF16)

    o_ref[...] = jnp.dot(h_scr[...], w_ref[...], preferred_element_type=F32)


def _in_proj(x, n_rows, mods_i, g, w_in, j_even):
    tm, tn = 512, 1920
    mrow = _mod_row(tm)
    return pl.pallas_call(
        _norm_matmul_kernel,
        grid=(n_rows // tm, IN_W_PAD // tn),
        in_specs=[
            pl.BlockSpec((tm, D_MODEL), lambda t, j: (t, 0)),
            pl.BlockSpec((None, 3, 1, D_MODEL), lambda t, j: (mrow(t), 1, 0, 0)),
            pl.BlockSpec((1, D_MODEL), lambda t, j: (0, 0)),
            pl.BlockSpec((None, D_MODEL, tn), lambda t, j: (j_even, 0, j)),
        ],
        out_specs=pl.BlockSpec((tm, tn), lambda t, j: (t, j)),
        out_shape=jax.ShapeDtypeStruct((n_rows, IN_W_PAD), F32),
        scratch_shapes=[pltpu.VMEM((tm, D_MODEL), BF16)],
        compiler_params=_cparams(("arbitrary", "arbitrary")),
        name="in_proj",
    )(x, mods_i, g, w_in)


def _head_rms(x, g):
    ms = jnp.mean(x * x, axis=-1, keepdims=True)
    return x * lax.rsqrt(ms + EPS) * g


def _rope(x, cos, sin):
    lane = lax.broadcasted_iota(jnp.int32, x.shape, 1)
    first_half = (lane & (NA_HEAD_DIM // 4)) == 0
    partner = jnp.where(first_half, pltpu.roll(x, NA_HEAD_DIM - NA_HEAD_DIM // 4, 1),
                        pltpu.roll(x, NA_HEAD_DIM // 4, 1))
    return x * cos + partner * sin


def _dot_nt(a, b):
    return lax.dot_general(a, b, (((1,), (1,)), ((), ())), preferred_element_type=F32)


def _softmax_pv(scores, values):
    m = functools.reduce(jnp.maximum, [jnp.max(s, axis=-1, keepdims=True) for s in scores])
    ps = [jnp.exp(s - m) for s in scores]
    denom = functools.reduce(jnp.add, [jnp.sum(p, axis=-1, keepdims=True) for p in ps])
    o = functools.reduce(jnp.add, [jnp.dot(p.astype(BF16), v, preferred_element_type=F32)
                                   for p, v in zip(ps, values)])
    return o / denom


Q_SPLIT = 2


def _na_kernel(q_ref, k_ref, v_ref, qc_ref, kc_ref, vc_ref, cos_ref, sin_ref, qkg_ref, bias_ref, o_ref, oc_ref,
               k_scr, v_scr, kc_scr, vc_scr):
    j = pl.program_id(2)

    @pl.when(j == 0)
    def _():
        kn = _head_rms(k_ref[...], qkg_ref[1])
        k_scr[...] = _rope(kn, cos_ref[...], sin_ref[...]).astype(BF16)
        v_scr[...] = v_ref[...].astype(BF16)
        kcn = _head_rms(kc_ref[...], qkg_ref[1]).astype(BF16)
        vcb = vc_ref[...].astype(BF16)
        kc_scr[...] = kcn
        vc_scr[...] = vcb
        qcn = _head_rms(qc_ref[...], qkg_ref[0]).astype(BF16)
        oc_ref[...] = _softmax_pv([_dot_nt(qcn, kcn) * ATTN_SCALE], [vcb]).astype(oc_ref.dtype)

    k0 = pl.multiple_of(jnp.clip(Q_ROWS * j - WIN_R // 2, 0, GRID_ROWS - K_ROWS) * GRID_W, GRID_W)
    kw = k_scr[pl.ds(k0, K_BLK), :]
    vw = v_scr[pl.ds(k0, K_BLK), :]
    sub = Q_BLK // Q_SPLIT
    for part in range(Q_SPLIT):
        rows = pl.ds(part * sub, sub)
        q0 = pl.multiple_of(j * Q_BLK + part * sub, sub)
        qn = _head_rms(q_ref[rows, :], qkg_ref[0])
        qr = _rope(qn, cos_ref[pl.ds(q0, sub), :], sin_ref[pl.ds(q0, sub), :]).astype(BF16)
        s_win = _dot_nt(qr, kw) * ATTN_SCALE + bias_ref[rows, :]
        s_ctx = _dot_nt(qn.astype(BF16), kc_scr[...]) * ATTN_SCALE
        o_ref[rows, :] = _softmax_pv([s_win, s_ctx], [vw, vc_scr[...]]).astype(o_ref.dtype)


def _bias_pattern(j):
    return jnp.where(j == 0, 0, jnp.where(j == N_QBLK - 1, 2, 1))


def _na_attention(proj, cos_t, sin_t, qkg, bias):
    lat_blk = SEQ // Q_BLK
    ctx_blk0 = N_LAT // CTX_LEN
    hd = NA_HEAD_DIM
    return pl.pallas_call(
        _na_kernel,
        grid=(BATCH, NA_HEADS, N_QBLK),
        in_specs=[
            pl.BlockSpec((Q_BLK, hd), lambda b, h, j: (b * lat_blk + j, COL_Q + h)),
            pl.BlockSpec((SEQ, hd), lambda b, h, j: (b, COL_K + h)),
            pl.BlockSpec((SEQ, hd), lambda b, h, j: (b, COL_V + h)),
            pl.BlockSpec((CTX_LEN, hd), lambda b, h, j: (ctx_blk0 + b, COL_Q + h)),
            pl.BlockSpec((CTX_LEN, hd), lambda b, h, j: (ctx_blk0 + b, COL_K + h)),
            pl.BlockSpec((CTX_LEN, hd), lambda b, h, j: (ctx_blk0 + b, COL_V + h)),
            pl.BlockSpec((SEQ, hd), lambda b, h, j: (0, 0)),
            pl.BlockSpec((SEQ, hd), lambda b, h, j: (0, 0)),
            pl.BlockSpec((2, 1, hd), lambda b, h, j: (0, 0, 0)),
            pl.BlockSpec((None, None, Q_BLK, K_BLK), lambda b, h, j: (h, _bias_pattern(j), 0, 0)),
        ],
        out_specs=[pl.BlockSpec((Q_BLK, hd), lambda b, h, j: (b * lat_blk + j, h)),
                   pl.BlockSpec((CTX_LEN, hd), lambda b, h, j: (b, h))],
        out_shape=[jax.ShapeDtypeStruct((N_LAT, NA_WIDTH), BF16),
                   jax.ShapeDtypeStruct((BATCH * CTX_LEN, NA_WIDTH), BF16)],
        scratch_shapes=[pltpu.VMEM((SEQ, hd), BF16), pltpu.VMEM((SEQ, hd), BF16),
                        pltpu.VMEM((CTX_LEN, hd), BF16), pltpu.VMEM((CTX_LEN, hd), BF16)],
        compiler_params=_cparams(("arbitrary", "arbitrary", "arbitrary")),
        name="na_attention",
    )(proj, proj, proj, proj, proj, proj, cos_t, sin_t, qkg, bias)


def _attention_bias(rpb_j):
    n_dr = 2 * WIN_R - 1
    qcol = np.arange(GRID_W)
    c0 = np.clip(qcol - WIN_C // 2, 0, GRID_W - WIN_C)
    col_ok = (qcol[None, :] >= c0[:, None]) & (qcol[None, :] < c0[:, None] + WIN_C)
    r = rpb_j.astype(F32)
    edge = GRID_W - WIN_C
    ep = jnp.concatenate([jnp.repeat(r[..., :1], edge, axis=-1), r, jnp.repeat(r[..., -1:], edge, axis=-1)],
                         axis=-1)
    t1 = jnp.stack([ep[..., GRID_W - 1 - qc:2 * GRID_W - 1 - qc] for qc in range(GRID_W)], axis=2)
    t1 = jnp.where(jnp.asarray(col_ok)[None, None], t1, -jnp.inf)
    band = t1.transpose(0, 2, 1, 3).reshape(NA_HEADS, GRID_W, n_dr * GRID_W)
    pad_w = Q_ROWS * GRID_W
    band = jnp.pad(band, ((0, 0), (0, 0), (pad_w, pad_w)), constant_values=-jnp.inf)
    strips, masks = [], []
    for jb in (0, N_QBLK // 2, N_QBLK - 1):
        k_first = int(np.clip(Q_ROWS * jb - WIN_R // 2, 0, GRID_ROWS - K_ROWS))
        kr = k_first + np.arange(K_ROWS)
        for a in range(Q_ROWS):
            row = Q_ROWS * jb + a
            r0 = int(np.clip(row - WIN_R // 2, 0, GRID_ROWS - WIN_R))
            off = pad_w + (k_first - row + WIN_R - 1) * GRID_W
            strips.append(band[:, :, off:off + K_BLK])
            masks.append(np.repeat((kr >= r0) & (kr < r0 + WIN_R), GRID_W))
    vals = jnp.stack(strips, axis=1)
    mask = np.stack(masks)[None, :, None, :]
    bias = jnp.where(jnp.asarray(mask), vals, -jnp.inf)
    return bias.reshape(NA_HEADS, 3, Q_BLK, K_BLK)


def _rope_tables():
    quarter = NA_HEAD_DIM // 4
    inv_freq = ROPE_BASE ** (-jnp.arange(quarter, dtype=F32) / quarter)
    t = jnp.arange(SEQ)
    ang_r = (t // GRID_W).astype(F32)[:, None] * inv_freq[None, :]
    ang_c = (t % GRID_W).astype(F32)[:, None] * inv_freq[None, :]
    cos_t = jnp.concatenate([jnp.cos(ang_r), jnp.cos(ang_r), jnp.cos(ang_c), jnp.cos(ang_c)], axis=-1)
    sin_t = jnp.concatenate([-jnp.sin(ang_r), jnp.sin(ang_r), -jnp.sin(ang_c), jnp.sin(ang_c)], axis=-1)
    return cos_t, sin_t


SUBLANE = 8
CONV_HALO = SUBLANE
CONV_BLK = 256


def _conv_kernel(x_ref, prev_ref, next_ref, w_ref, b_ref, o_ref, pad_scr):
    t = pl.program_id(0)
    n_lat_blk = N_LAT // CONV_BLK
    is_ctx = t >= n_lat_blk
    seq_blks = jnp.where(is_ctx, CTX_LEN // CONV_BLK, SEQ // CONV_BLK)
    pos = jnp.where(is_ctx, t - n_lat_blk, t) % seq_blks
    pad_scr[0:CONV_HALO, :] = jnp.where(pos == 0, 0.0, prev_ref[...])
    pad_scr[CONV_HALO:CONV_HALO + CONV_BLK, :] = x_ref[...]
    pad_scr[CONV_HALO + CONV_BLK:2 * CONV_HALO + CONV_BLK, :] = jnp.where(pos == seq_blks - 1, 0.0,
                                                                           next_ref[...])
    xp = pad_scr[...]
    acc = jnp.zeros(x_ref.shape, F32) + b_ref[...]
    for k in range(CONV_K):
        lo = CONV_HALO - CONV_K // 2 + k
        acc = acc + w_ref[k:k + 1, :] * xp[lo:lo + CONV_BLK, :]
    o_ref[...] = _silu(acc)


def _conv_silu(proj, conv_w8, conv_b2, j_even):
    n_rows = proj.shape[0]
    cw = CONV_CH
    xbc_col0 = COL_XBC * LANE // cw
    halo_per_blk = CONV_BLK // CONV_HALO
    last_halo = n_rows // CONV_HALO - 1
    return pl.pallas_call(
        _conv_kernel,
        grid=(n_rows // CONV_BLK, CONV_CH // cw),
        in_specs=[
            pl.BlockSpec((CONV_BLK, cw), lambda t, c: (t, xbc_col0 + c)),
            pl.BlockSpec((CONV_HALO, cw), lambda t, c: (jnp.maximum(t * halo_per_blk - 1, 0), xbc_col0 + c)),
            pl.BlockSpec((CONV_HALO, cw),
                         lambda t, c: (jnp.minimum((t + 1) * halo_per_blk, last_halo), xbc_col0 + c)),
            pl.BlockSpec((None, SUBLANE, cw), lambda t, c: (j_even, 0, c)),
            pl.BlockSpec((None, 1, cw), lambda t, c: (j_even, 0, c)),
        ],
        out_specs=pl.BlockSpec((CONV_BLK, cw), lambda t, c: (t, c)),
        out_shape=jax.ShapeDtypeStruct((n_rows, CONV_CH), F32),
        scratch_shapes=[pltpu.VMEM((CONV_BLK + 2 * CONV_HALO, cw), F32)],
        compiler_params=_cparams(("arbitrary", "arbitrary")),
        name="conv_silu",
    )(proj, proj, proj, conv_w8, conv_b2)


def _softplus(x):
    return jnp.maximum(x, 0.0) + jnp.log1p(jnp.exp(-jnp.abs(x)))


def _dot_exact(a, b):
    return jnp.dot(a, b, precision=lax.Precision.HIGHEST, preferred_element_type=F32)


def _ssd_direction(xs_ref, b_ref, c_ref, dt_ref, dtb_ref, alog_ref, y_ref, h_scr, lane0, forward):
    q = SSD_CHUNK
    gw = SSD_INNER // SSD_GROUPS
    dt = _softplus(dt_ref[...] + dtb_ref[...])
    a = dt * (-jnp.exp(alog_ref[...]))
    ri = lax.broadcasted_iota(jnp.int32, (q, q), 0)
    ci = lax.broadcasted_iota(jnp.int32, (q, q), 1)
    tri = (ci <= ri) if forward else (ci >= ri)
    a_cum = _dot_exact(tri.astype(F32), a)
    a_cum_t = a_cum.T
    er = lax.broadcasted_iota(jnp.int32, (LANE, SSD_INNER), 0)
    ec = lax.broadcasted_iota(jnp.int32, (LANE, SSD_INNER), 1)
    expand = ((er - lane0) == (ec // SSD_HEAD_DIM)).astype(F32)
    dt_e = _dot_exact(dt, expand)
    ac_e = _dot_exact(a_cum, expand)
    end = q - 1 if forward else 0
    a_end_e = ac_e[end:end + 1, :]
    xdt = xs_ref[...] * dt_e
    in_decay = jnp.exp(ac_e)
    out_decay = jnp.exp(a_end_e - ac_e)
    state_decay = jnp.exp(a_end_e)
    lane = lax.broadcasted_iota(jnp.int32, (q, LANE), 1)
    for g in range(SSD_GROUPS):
        gs = slice(g * gw, (g + 1) * gw)
        bg_t = b_ref[:, g * SSD_STATE:(g + 1) * SSD_STATE].T.astype(BF16)
        cg = c_ref[:, g * SSD_STATE:(g + 1) * SSD_STATE].astype(BF16)
        cb = jnp.dot(cg, bg_t, preferred_element_type=F32)
        h_t = h_scr[:, gs]
        y_inter = jnp.dot(cg, h_t.astype(BF16), preferred_element_type=F32) * in_decay[:, gs]
        for pair in range(gw // LANE):
            cs = slice(g * gw + pair * LANE, g * gw + (pair + 1) * LANE)
            x_pair = xdt[:, cs].astype(BF16)
            res = []
            for sub in range(LANE // SSD_HEAD_DIM):
                hl = lane0 + (g * gw + pair * LANE) // SSD_HEAD_DIM + sub
                seg = a_cum[:, hl:hl + 1] - a_cum_t[hl:hl + 1, :]
                decay = jnp.exp(jnp.where(tri, seg, -jnp.inf))
                res.append(jnp.dot((cb * decay).astype(BF16), x_pair, preferred_element_type=F32))
            y_pair = jnp.where(lane < SSD_HEAD_DIM, res[0], res[1])
            y_ref[:, cs] = y_pair + y_inter[:, pair * LANE:(pair + 1) * LANE]
        x_out = (xdt[:, gs] * out_decay[:, gs]).astype(BF16)
        h_scr[:, gs] = state_decay[:, gs] * h_t + jnp.dot(bg_t, x_out, preferred_element_type=F32)


def _ssd_kernel(xs_f, b_f, c_f, dt_f, xs_b, b_b, c_b, dt_b, dtb_ref, alog_ref, yf_ref, yb_ref,
                hf_scr, hb_scr):
    @pl.when(pl.program_id(1) == 0)
    def _():
        hf_scr[...] = jnp.zeros_like(hf_scr)
        hb_scr[...] = jnp.zeros_like(hb_scr)

    _ssd_direction(xs_f, b_f, c_f, dt_f, dtb_ref, alog_ref, yf_ref, hf_scr, 0, True)
    _ssd_direction(xs_b, b_b, c_b, dt_b, dtb_ref, alog_ref, yb_ref, hb_scr, SSD_HEADS, False)


def _ssd_scan(act, proj, dtb, alog, j_even):
    q = SSD_CHUNK
    n_rows = act.shape[0]
    n_ctx = CTX_LEN // q
    n_lat = SEQ // q
    ctx_blk0 = N_LAT // q

    def fwd_blk(b, s):
        return jnp.where(s < n_ctx, ctx_blk0 + b * n_ctx + s, b * n_lat + (s - n_ctx))

    def bwd_blk(b, s):
        return jnp.where(s < n_ctx, ctx_blk0 + b * n_ctx + (n_ctx - 1 - s),
                         b * n_lat + (n_lat - 1 - (s - n_ctx)))

    gn = SSD_GROUPS * SSD_STATE
    b_col = SSD_INNER // gn
    c_col = b_col + 1

    def specs(blk):
        return [
            pl.BlockSpec((q, SSD_INNER), lambda b, s: (blk(b, s), 0)),
            pl.BlockSpec((q, gn), lambda b, s: (blk(b, s), b_col)),
            pl.BlockSpec((q, gn), lambda b, s: (blk(b, s), c_col)),
            pl.BlockSpec((q, LANE), lambda b, s: (blk(b, s), COL_DT)),
        ]

    small = pl.BlockSpec((None, 1, LANE), lambda b, s: (j_even, 0, 0))
    return pl.pallas_call(
        _ssd_kernel,
        grid=(BATCH, n_ctx + n_lat),
        in_specs=specs(fwd_blk) + specs(bwd_blk) + [small, small],
        out_specs=[pl.BlockSpec((q, SSD_INNER), lambda b, s: (fwd_blk(b, s), 0)),
                   pl.BlockSpec((q, SSD_INNER), lambda b, s: (bwd_blk(b, s), 0))],
        out_shape=[jax.ShapeDtypeStruct((n_rows, SSD_INNER), F32)] * 2,
        scratch_shapes=[pltpu.VMEM((SSD_STATE, SSD_INNER), F32)] * 2,
        compiler_params=_cparams(("arbitrary", "arbitrary")),
        name="ssd_scan",
    )(act, act, act, proj, act, act, act, proj, dtb, alog)


def _out_proj_kernel(attn_ref, yf_ref, yb_ref, xs_ref, z0_ref, z1_ref, dsk_ref, ng_ref, w_ref, x_ref, mod_ref,
                     o_ref):
    mix = jnp.dot(attn_ref[...], w_ref[0:NA_WIDTH, :], preferred_element_type=F32)
    gw = SSD_INNER // SSD_GROUPS
    for g, z_ref in enumerate((z0_ref, z1_ref)):
        gs = slice(g * gw, (g + 1) * gw)
        y = yf_ref[:, gs] + yb_ref[:, gs] + dsk_ref[:, gs] * xs_ref[:, gs]
        seg = y * _silu(z_ref[...])
        ms = jnp.mean(seg * seg, axis=-1, keepdims=True)
        yn = (seg * lax.rsqrt(ms + EPS) * ng_ref[:, g * gw:(g + 1) * gw]).astype(BF16)
        mix = mix + jnp.dot(yn, w_ref[NA_WIDTH + g * gw:NA_WIDTH + (g + 1) * gw, :],
                            preferred_element_type=F32)
    o_ref[...] = x_ref[...] + mod_ref[2] * mix


def _out_proj(attn, y_f, y_b, act, proj, dsk, ng, w_out, x, n_rows, mods_i, j_even):
    tm = 256
    mrow = _mod_row(tm)
    gw = SSD_INNER // SSD_GROUPS
    z_col = COL_Z * LANE // gw
    assert SSD_GROUPS == 2
    return pl.pallas_call(
        _out_proj_kernel,
        grid=(n_rows // tm,),
        in_specs=[
            pl.BlockSpec((tm, NA_WIDTH), lambda t: (t, 0)),
            pl.BlockSpec((tm, SSD_INNER), lambda t: (t, 0)),
            pl.BlockSpec((tm, SSD_INNER), lambda t: (t, 0)),
            pl.BlockSpec((tm, SSD_INNER), lambda t: (t, 0)),
            pl.BlockSpec((tm, gw), lambda t: (t, z_col)),
            pl.BlockSpec((tm, gw), lambda t: (t, z_col + 1)),
            pl.BlockSpec((None, 1, SSD_INNER), lambda t: (j_even, 0, 0)),
            pl.BlockSpec((None, 1, SSD_INNER), lambda t: (j_even, 0, 0)),
            pl.BlockSpec((None, D_MODEL, D_MODEL), lambda t: (j_even, 0, 0)),
            pl.BlockSpec((tm, D_MODEL), lambda t: (t, 0)),
            pl.BlockSpec((None, 3, 1, D_MODEL), lambda t: (mrow(t), 1, 0, 0)),
        ],
        out_specs=pl.BlockSpec((tm, D_MODEL), lambda t: (t, 0)),
        out_shape=jax.ShapeDtypeStruct((n_rows, D_MODEL), F32),
        compiler_params=_cparams(("arbitrary",)),
        name="out_proj",
    )(attn, y_f, y_b, act, proj, proj, dsk, ng, w_out, x, mods_i)


def _fourier_chan_kernel(x_ref, mod_ref, g_ref, cs_ref, p_ref, q_ref):
    h = _mod_norm(x_ref[...], g_ref[...], mod_ref[0], mod_ref[1])
    for g in range(F_GROUPS):
        gs = slice(g * F_GROUP_CH, (g + 1) * F_GROUP_CH)
        pq = jnp.dot(h[:, gs].astype(BF16), cs_ref[...], preferred_element_type=F32)
        p_ref[:, gs] = pq[:, :F_GROUP_CH].astype(BF16)
        q_ref[:, gs] = pq[:, F_GROUP_CH:].astype(BF16)


def _fourier_chan(x, n_rows, mods_i, g, cs):
    tm = 512
    mrow = _mod_row(tm)
    return pl.pallas_call(
        _fourier_chan_kernel,
        grid=(n_rows // tm,),
        in_specs=[
            pl.BlockSpec((tm, D_MODEL), lambda t: (t, 0)),
            pl.BlockSpec((None, 3, 1, D_MODEL), lambda t: (mrow(t), 1, 0, 0)),
            pl.BlockSpec((1, D_MODEL), lambda t: (0, 0)),
            pl.BlockSpec((F_GROUP_CH, 2 * F_GROUP_CH), lambda t: (0, 0)),
        ],
        out_specs=[pl.BlockSpec((tm, D_MODEL), lambda t: (t, 0))] * 2,
        out_shape=[jax.ShapeDtypeStruct((n_rows, D_MODEL), BF16)] * 2,
        compiler_params=_cparams(("arbitrary",)),
        name="fourier_chan",
    )(x, mods_i, g, cs)


DFT_SPLIT = 64


def _fourier_pos_kernel(u_ref, v_ref, p_ref, q_ref, w_ref, x_ref, mod_ref, o_ref, acc_scr, c_scr, s_scr, *,
                        scale):
    k = pl.program_id(2)

    @pl.when(k == 0)
    def _():
        acc_scr[...] = jnp.zeros_like(acc_scr)

    vr, vi = v_ref[0], v_ref[1]
    for a in range(u_ref.shape[1]):
        ur, ui = u_ref[0, a:a + 1, :], u_ref[1, a:a + 1, :]
        rows = slice(a * DFT_SPLIT, (a + 1) * DFT_SPLIT)
        c_scr[rows, :] = (ur * vr - ui * vi).astype(BF16)
        s_scr[rows, :] = (ui * vr + ur * vi).astype(BF16)
    acc_scr[...] += (jnp.dot(c_scr[...], p_ref[...], preferred_element_type=F32)
                     - jnp.dot(s_scr[...], q_ref[...], preferred_element_type=F32))

    @pl.when(k == pl.num_programs(2) - 1)
    def _():
        f = (acc_scr[...] * scale).astype(BF16)
        o_ref[...] = x_ref[...] + mod_ref[2] * jnp.dot(f, w_ref[...], preferred_element_type=F32)


def _fourier_pos(tabs, p, q, w, x, mods_i, j_odd, seq_len, row0, is_ctx):
    u_tab, v_tab = tabs
    tm = tk = min(seq_len, 512)
    nm = seq_len // tm
    blk0 = row0 // tm
    scale = float((seq_len * F_GROUP_CH) ** -0.5)
    return pl.pallas_call(
        functools.partial(_fourier_pos_kernel, scale=scale),
        grid=(BATCH, nm, seq_len // tk),
        in_specs=[
            pl.BlockSpec((2, tm // DFT_SPLIT, tk), lambda b, m, k: (0, m, k)),
            pl.BlockSpec((2, DFT_SPLIT, tk), lambda b, m, k: (0, 0, k)),
            pl.BlockSpec((tk, D_MODEL), lambda b, m, k: (blk0 + b * nm + k, 0)),
            pl.BlockSpec((tk, D_MODEL), lambda b, m, k: (blk0 + b * nm + k, 0)),
            pl.BlockSpec((None, D_MODEL, D_MODEL), lambda b, m, k: (j_odd, 0, 0)),
            pl.BlockSpec((tm, D_MODEL), lambda b, m, k: (blk0 + b * nm + m, 0)),
            pl.BlockSpec((None, 3, 1, D_MODEL), lambda b, m, k: (BATCH if is_ctx else b, 1, 0, 0)),
        ],
        out_specs=pl.BlockSpec((tm, D_MODEL), lambda b, m, k: (b * nm + m, 0)),
        out_shape=jax.ShapeDtypeStruct((BATCH * seq_len, D_MODEL), F32),
        scratch_shapes=[pltpu.VMEM((tm, D_MODEL), F32), pltpu.VMEM((tm, tk), BF16), pltpu.VMEM((tm, tk), BF16)],
        compiler_params=_cparams(("arbitrary", "arbitrary", "arbitrary")),
        name="fourier_pos",
    )(u_tab, v_tab, p, q, w, x, mods_i)


def _phase_table(freq, n):
    l = jnp.arange(n, dtype=jnp.int32)
    ang = ((freq[:, None] * l[None, :]) % n).astype(F32) * np.float32(2.0 * np.pi / n)
    return jnp.stack([jnp.cos(ang), jnp.sin(ang)])


def _dft_tables(n):
    a = jnp.arange(n // DFT_SPLIT, dtype=jnp.int32) * DFT_SPLIT
    b = jnp.arange(DFT_SPLIT, dtype=jnp.int32)
    return _phase_table(a, n), _phase_table(b, n)


def kernel(x, c, ctx, c_ctx, mod_w, mod_b, norm_g, ffn_w1, ffn_w3, ffn_w2, mix_w_in, mix_w_out, qk_g, rpb,
           conv_w, conv_b, dt_bias, a_log, ssd_d, ssd_norm_g, fourier_w):
    n_even = mix_w_in.shape[0]
    xs = jnp.concatenate([x.reshape(N_LAT, D_MODEL), ctx.reshape(BATCH * CTX_LEN, D_MODEL)], axis=0)
    mods = _modulation(c, c_ctx, mod_w, mod_b)

    w1, w3, w2 = ffn_w1, ffn_w3, ffn_w2
    qkv_w, z_w, xbc_w, dt_w = jnp.split(mix_w_in, [3 * NA_WIDTH, 3 * NA_WIDTH + SSD_INNER,
                                                    3 * NA_WIDTH + SSD_INNER + CONV_CH], axis=-1)
    w_in = jnp.concatenate([xbc_w, z_w, qkv_w, dt_w, jnp.zeros((n_even, D_MODEL, IN_W_PAD - IN_W), F32)],
                           axis=-1).astype(BF16)
    w_out = mix_w_out.astype(BF16)
    w_f = fourier_w.astype(BF16)
    qkg = qk_g.reshape(n_even, 2, 1, NA_HEAD_DIM)
    conv_w8 = jnp.pad(conv_w, ((0, 0), (0, SUBLANE - CONV_K), (0, 0)))
    conv_b2 = conv_b.reshape(n_even, 1, CONV_CH)
    pad_l = LANE - 2 * SSD_HEADS
    dtb = jnp.pad(dt_bias.reshape(n_even, 1, 2 * SSD_HEADS), ((0, 0), (0, 0), (0, pad_l)))
    alog = jnp.pad(a_log.reshape(n_even, 1, 2 * SSD_HEADS), ((0, 0), (0, 0), (0, pad_l)))
    dsk = jnp.repeat(ssd_d, SSD_HEAD_DIM, axis=-1).reshape(n_even, 1, SSD_INNER)
    ng = ssd_norm_g.reshape(n_even, 1, SSD_INNER)
    cos_t, sin_t = _rope_tables()
    chan = _phase_table(jnp.arange(F_GROUP_CH, dtype=jnp.int32), F_GROUP_CH)
    cs_chan = jnp.concatenate([chan[0], chan[1]], axis=1).astype(BF16)
    dft_lat = _dft_tables(SEQ)
    dft_ctx = _dft_tables(CTX_LEN)

    last_ctx = ((DEPTH - 1) // 2) * 2
    for i in range(DEPTH):
        use_ctx = i <= last_ctx
        ctx_out = i < last_ctx
        j = i // 2
        mods_i = mods[i]
        g = norm_g[i].reshape(3, 1, D_MODEL)
        rows_in = N_TOK if use_ctx else N_LAT
        rows_out = N_TOK if ctx_out else N_LAT
        xs = _ffn(xs, rows_in, mods_i, 0, g[0], w1, w3, w2, i, 0)
        if i % 2 == 0:
            assert use_ctx, "even mixer layers read the context keys and SSD states"
            proj = _in_proj(xs, rows_in, mods_i, g[1], w_in, j)
            bias = _attention_bias(rpb[j])
            attn, attn_ctx = _na_attention(proj, cos_t, sin_t, qkg[j], bias)
            if ctx_out:
                attn = jnp.concatenate([attn, attn_ctx], axis=0)
            act = _conv_silu(proj, conv_w8, conv_b2, j)
            y_f, y_b = _ssd_scan(act, proj, dtb, alog, j)
            xs = _out_proj(attn, y_f, y_b, act, proj, dsk, ng, w_out, xs, rows_out, mods_i, j)
        else:
            p, q = _fourier_chan(xs, rows_out, mods_i, g[1], cs_chan)
            new = _fourier_pos(dft_lat, p, q, w_f, xs, mods_i, j, SEQ, 0, False)
            if ctx_out:
                new_ctx = _fourier_pos(dft_ctx, p, q, w_f, xs, mods_i, j, CTX_LEN, N_LAT, True)
                new = jnp.concatenate([new, new_ctx], axis=0)
            xs = new
        xs = _ffn(xs, rows_out, mods_i, 2, g[2], w1, w3, w2, i, 1)
    return xs[:N_LAT].reshape(BATCH, SEQ, D_MODEL)
```

```python
import functools

import numpy as np
import jax
import jax.numpy as jnp
from jax import lax
from jax.experimental import pallas as pl
from jax.experimental.pallas import tpu as pltpu

F32 = jnp.float32
BF16 = jnp.bfloat16

D_MODEL = 2048
BATCH = 2
SEQ = 4096
DEPTH = 4
GRID_W = 64
GRID_ROWS = SEQ // GRID_W
CTX_LEN = 256
N_LAT = BATCH * SEQ
N_TOK = N_LAT + BATCH * CTX_LEN
NA_HEADS = 8
NA_HEAD_DIM = 128
NA_WIDTH = NA_HEADS * NA_HEAD_DIM
WIN_R = 8
WIN_C = 16
ROPE_BASE = 10000.0
SSD_HEADS = 16
SSD_HEAD_DIM = 64
SSD_INNER = SSD_HEADS * SSD_HEAD_DIM
SSD_GROUPS = 2
SSD_STATE = 128
SSD_CHUNK = 128
CONV_K = 5
CONV_CH = SSD_INNER + 2 * SSD_GROUPS * SSD_STATE
IN_W = 3 * NA_WIDTH + SSD_INNER + CONV_CH + 2 * SSD_HEADS
LANE = 128
IN_W_PAD = ((IN_W + LANE - 1) // LANE) * LANE
F_GROUPS = 4
F_GROUP_CH = D_MODEL // F_GROUPS
D_FF = 5632
N_MOD = 9
EPS = 1e-6
ATTN_SCALE = NA_HEAD_DIM ** -0.5

COL_Q = 0
COL_K = NA_WIDTH // LANE
COL_V = 2 * NA_WIDTH // LANE
COL_Z = 3 * NA_WIDTH // LANE
COL_XBC = (3 * NA_WIDTH + SSD_INNER) // LANE
COL_DT = (3 * NA_WIDTH + SSD_INNER + CONV_CH) // LANE

Q_ROWS = 8
Q_BLK = Q_ROWS * GRID_W
K_ROWS = 16
K_BLK = K_ROWS * GRID_W
N_QBLK = GRID_ROWS // Q_ROWS

VMEM_LIMIT = 56 * 1024 * 1024


def _cparams(sem, vmem_limit=VMEM_LIMIT):
    return pltpu.CompilerParams(dimension_semantics=sem, vmem_limit_bytes=vmem_limit)


def _sigmoid(x):
    return 1.0 / (1.0 + jnp.exp(-x))


def _silu(x):
    return x * _sigmoid(x)


def _mod_norm(x, g, shift, scale):
    ms = jnp.mean(x * x, axis=-1, keepdims=True)
    y = x * lax.rsqrt(ms + EPS) * g
    return y * (1.0 + scale) + shift


def _mod_row(tile_rows):
    def f(t):
        return jnp.minimum((t * tile_rows) // SEQ, BATCH)
    return f


def _mod_kernel(c_ref, w_ref, b_ref, o_ref):
    s = _silu(c_ref[...]).astype(BF16)
    o_ref[...] = jnp.dot(s, w_ref[...].astype(BF16), preferred_element_type=F32) + b_ref[...]


def _modulation(c, c_ctx, mod_w, mod_b):
    rows = 8
    cvec = jnp.concatenate([c, c_ctx[None, :], jnp.zeros((rows - BATCH - 1, D_MODEL), F32)], axis=0)
    n = N_MOD * D_MODEL
    tn = 1024
    out = pl.pallas_call(
        _mod_kernel,
        grid=(DEPTH, n // tn),
        in_specs=[
            pl.BlockSpec((rows, D_MODEL), lambda i, j: (0, 0)),
            pl.BlockSpec((None, D_MODEL, tn), lambda i, j: (i, 0, j)),
            pl.BlockSpec((None, 1, tn), lambda i, j: (i, 0, j)),
        ],
        out_specs=pl.BlockSpec((None, rows, tn), lambda i, j: (i, 0, j)),
        out_shape=jax.ShapeDtypeStruct((DEPTH, rows, n), F32),
        compiler_params=_cparams(("arbitrary", "arbitrary")),
        name="modulation",
    )(cvec, mod_w, mod_b.reshape(DEPTH, 1, n))
    return out[:, :BATCH + 1].reshape(DEPTH, BATCH + 1, N_MOD, 1, D_MODEL)


FFN_TM = 512
FFN_TF = 512
FFN_NORM_ROWS = 64
FFN_CAST_TILES = 16
FFN_CAST_ROWS = 256


def _ffn_kernel(*refs, cast_next, cast_every):
    if cast_next:
        (x_ref, xn_ref, mod_ref, modn_ref, g_ref, w1_ref, w3_ref, w2_ref, n1_ref, n3_ref, n2_ref,
         o_ref, c1_ref, c3_ref, c2_ref, h_scr, hn_scr, acc_scr) = refs

        step = pl.program_id(0) * pl.num_programs(1) + pl.program_id(1)

        @pl.when((step % cast_every == 0) & (pl.program_id(0) < FFN_CAST_TILES))
        def _():
            c1_ref[...] = n1_ref[...].astype(BF16)
            c3_ref[...] = n3_ref[...].astype(BF16)
            c2_ref[...] = n2_ref[...].astype(BF16)
    else:
        (x_ref, xn_ref, mod_ref, modn_ref, g_ref, w1_ref, w3_ref, w2_ref, o_ref,
         h_scr, hn_scr, acc_scr) = refs
    t = pl.program_id(0)
    j = pl.program_id(1)

    @pl.when(j == 0)
    def _():
        acc_scr[...] = jnp.zeros_like(acc_scr)

    @pl.when((j == 0) & (t == 0))
    def _():
        h_scr[...] = _mod_norm(x_ref[...], g_ref[...], mod_ref[0], mod_ref[1]).astype(BF16)

    @pl.when((j == 0) & (t > 0))
    def _():
        h_scr[...] = hn_scr[...]

    n_slices = x_ref.shape[0] // FFN_NORM_ROWS
    r = pl.multiple_of(jnp.minimum(j, n_slices - 1) * FFN_NORM_ROWS, FFN_NORM_ROWS)
    hn = _mod_norm(xn_ref[pl.ds(r, FFN_NORM_ROWS), :], g_ref[...], modn_ref[0], modn_ref[1])
    hn_scr[pl.ds(r, FFN_NORM_ROWS), :] = hn.astype(BF16)

    h = h_scr[...]
    a = jnp.dot(h, w1_ref[...], preferred_element_type=F32)
    b = jnp.dot(h, w3_ref[...], preferred_element_type=F32)
    u = (_silu(a) * b).astype(BF16)
    acc_scr[...] += jnp.dot(u, w2_ref[...], preferred_element_type=F32)

    @pl.when(j == pl.num_programs(1) - 1)
    def _():
        o_ref[...] = x_ref[...] + 0.5 * mod_ref[2] * acc_scr[...]


def _ffn(x, n_rows, mods_i, sub, g, weights, next_weights):
    tm, tf = FFN_TM, FFN_TF
    mrow = _mod_row(tm)
    nt = n_rows // tm
    nj = D_FF // tf
    assert nj >= tm // FFN_NORM_ROWS
    assert nt >= FFN_CAST_TILES

    def nxt(t):
        return jnp.minimum(t + 1, nt - 1)

    in_specs = [
        pl.BlockSpec((tm, D_MODEL), lambda t, j: (t, 0)),
        pl.BlockSpec((tm, D_MODEL), lambda t, j: (nxt(t), 0)),
        pl.BlockSpec((None, 3, 1, D_MODEL), lambda t, j: (mrow(t), sub, 0, 0)),
        pl.BlockSpec((None, 3, 1, D_MODEL), lambda t, j: (mrow(nxt(t)), sub, 0, 0)),
        pl.BlockSpec((1, D_MODEL), lambda t, j: (0, 0)),
        pl.BlockSpec((D_MODEL, tf), lambda t, j: (0, j)),
        pl.BlockSpec((D_MODEL, tf), lambda t, j: (0, j)),
        pl.BlockSpec((tf, D_MODEL), lambda t, j: (j, 0)),
    ]
    out_specs = [pl.BlockSpec((tm, D_MODEL), lambda t, j: (t, 0))]
    out_shape = [jax.ShapeDtypeStruct((n_rows, D_MODEL), F32)]
    args = [x, x, mods_i, mods_i, g, *weights]
    cr = FFN_CAST_ROWS
    steps_per_blk = (FFN_CAST_TILES * nj * cr * tf) // (D_MODEL * D_FF)
    assert steps_per_blk * D_MODEL * D_FF == FFN_CAST_TILES * nj * cr * tf
    if next_weights is not None:
        n1, n3, n2, layer, which = next_weights

        def blk(t, j):
            bid = jnp.minimum(t * nj + j, FFN_CAST_TILES * nj - 1) // steps_per_blk
            return bid // nj, bid % nj

        in_specs += [
            pl.BlockSpec((None, None, cr, tf), lambda t, j: (layer, which, *blk(t, j))),
            pl.BlockSpec((None, None, cr, tf), lambda t, j: (layer, which, *blk(t, j))),
            pl.BlockSpec((None, None, tf, cr), lambda t, j: (layer, which, *blk(t, j)[::-1])),
        ]
        out_specs += [
            pl.BlockSpec((cr, tf), lambda t, j: blk(t, j)),
            pl.BlockSpec((cr, tf), lambda t, j: blk(t, j)),
            pl.BlockSpec((tf, cr), lambda t, j: blk(t, j)[::-1]),
        ]
        out_shape += [jax.ShapeDtypeStruct((D_MODEL, D_FF), BF16), jax.ShapeDtypeStruct((D_MODEL, D_FF), BF16),
                      jax.ShapeDtypeStruct((D_FF, D_MODEL), BF16)]
        args += [n1, n3, n2]
    outs = pl.pallas_call(
        functools.partial(_ffn_kernel, cast_next=next_weights is not None, cast_every=steps_per_blk),
        grid=(nt, nj),
        in_specs=in_specs,
        out_specs=out_specs,
        out_shape=out_shape,
        scratch_shapes=[pltpu.VMEM((tm, D_MODEL), BF16), pltpu.VMEM((tm, D_MODEL), BF16),
                        pltpu.VMEM((tm, D_MODEL), F32)],
        compiler_params=_cparams(("arbitrary", "arbitrary")),
        name="ffn",
    )(*args)
    return outs[0], (tuple(outs[1:]) if next_weights is not None else None)


def _norm_matmul_kernel(x_ref, mod_ref, g_ref, w_ref, o_ref, h_scr):
    @pl.when(pl.program_id(1) == 0)
    def _():
        h_scr[...] = _mod_norm(x_ref[...], g_ref[...], mod_ref[0], mod_ref[1]).astype(BF16)

    o_ref[...] = jnp.dot(h_scr[...], w_ref[...], preferred_element_type=F32)


def _in_proj(x, n_rows, mods_i, g, w_in, j_even):
    tm, tn = 512, 1920
    mrow = _mod_row(tm)
    return pl.pallas_call(
        _norm_matmul_kernel,
        grid=(n_rows // tm, IN_W_PAD // tn),
        in_specs=[
            pl.BlockSpec((tm, D_MODEL), lambda t, j: (t, 0)),
            pl.BlockSpec((None, 3, 1, D_MODEL), lambda t, j: (mrow(t), 1, 0, 0)),
            pl.BlockSpec((1, D_MODEL), lambda t, j: (0, 0)),
            pl.BlockSpec((None, D_MODEL, tn), lambda t, j: (j_even, 0, j)),
        ],
        out_specs=pl.BlockSpec((tm, tn), lambda t, j: (t, j)),
        out_shape=jax.ShapeDtypeStruct((n_rows, IN_W_PAD), F32),
        scratch_shapes=[pltpu.VMEM((tm, D_MODEL), BF16)],
        compiler_params=_cparams(("arbitrary", "arbitrary")),
        name="in_proj",
    )(x, mods_i, g, w_in)


def _head_rms(x, g):
    ms = jnp.mean(x * x, axis=-1, keepdims=True)
    return x * lax.rsqrt(ms + EPS) * g


def _rope(x, cos, sin):
    lane = lax.broadcasted_iota(jnp.int32, x.shape, 1)
    first_half = (lane & (NA_HEAD_DIM // 4)) == 0
    partner = jnp.where(first_half, pltpu.roll(x, NA_HEAD_DIM - NA_HEAD_DIM // 4, 1),
                        pltpu.roll(x, NA_HEAD_DIM // 4, 1))
    return x * cos + partner * sin


def _dot_nt(a, b):
    return lax.dot_general(a, b, (((1,), (1,)), ((), ())), preferred_element_type=F32)


def _softmax_pv(scores, values):
    m = functools.reduce(jnp.maximum, [jnp.max(s, axis=-1, keepdims=True) for s in scores])
    ps = [jnp.exp(s - m) for s in scores]
    denom = functools.reduce(jnp.add, [jnp.sum(p, axis=-1, keepdims=True) for p in ps])
    o = functools.reduce(jnp.add, [jnp.dot(p.astype(BF16), v, preferred_element_type=F32)
                                   for p, v in zip(ps, values)])
    return o / denom


Q_SPLIT = 2


def _na_kernel(q_ref, k_ref, v_ref, qc_ref, kc_ref, vc_ref, cos_ref, sin_ref, qkg_ref, bias_ref, o_ref, oc_ref,
               k_scr, v_scr, kc_scr, vc_scr):
    j = pl.program_id(2)

    @pl.when(j == 0)
    def _():
        kn = _head_rms(k_ref[...], qkg_ref[1])
        k_scr[...] = _rope(kn, cos_ref[...], sin_ref[...]).astype(BF16)
        v_scr[...] = v_ref[...].astype(BF16)
        kcn = _head_rms(kc_ref[...], qkg_ref[1]).astype(BF16)
        vcb = vc_ref[...].astype(BF16)
        kc_scr[...] = kcn
        vc_scr[...] = vcb
        qcn = _head_rms(qc_ref[...], qkg_ref[0]).astype(BF16)
        oc_ref[...] = _softmax_pv([_dot_nt(qcn, kcn) * ATTN_SCALE], [vcb]).astype(oc_ref.dtype)

    k0 = pl.multiple_of(jnp.clip(Q_ROWS * j - WIN_R // 2, 0, GRID_ROWS - K_ROWS) * GRID_W, GRID_W)
    kw = k_scr[pl.ds(k0, K_BLK), :]
    vw = v_scr[pl.ds(k0, K_BLK), :]
    sub = Q_BLK // Q_SPLIT
    for part in range(Q_SPLIT):
        rows = pl.ds(part * sub, sub)
        q0 = pl.multiple_of(j * Q_BLK + part * sub, sub)
        qn = _head_rms(q_ref[rows, :], qkg_ref[0])
        qr = _rope(qn, cos_ref[pl.ds(q0, sub), :], sin_ref[pl.ds(q0, sub), :]).astype(BF16)
        s_win = _dot_nt(qr, kw) * ATTN_SCALE + bias_ref[rows, :]
        s_ctx = _dot_nt(qn.astype(BF16), kc_scr[...]) * ATTN_SCALE
        o_ref[rows, :] = _softmax_pv([s_win, s_ctx], [vw, vc_scr[...]]).astype(o_ref.dtype)


def _bias_pattern(j):
    return jnp.where(j == 0, 0, jnp.where(j == N_QBLK - 1, 2, 1))


def _na_attention(proj, cos_t, sin_t, qkg, bias):
    lat_blk = SEQ // Q_BLK
    ctx_blk0 = N_LAT // CTX_LEN
    hd = NA_HEAD_DIM
    return pl.pallas_call(
        _na_kernel,
        grid=(BATCH, NA_HEADS, N_QBLK),
        in_specs=[
            pl.BlockSpec((Q_BLK, hd), lambda b, h, j: (b * lat_blk + j, COL_Q + h)),
            pl.BlockSpec((SEQ, hd), lambda b, h, j: (b, COL_K + h)),
            pl.BlockSpec((SEQ, hd), lambda b, h, j: (b, COL_V + h)),
            pl.BlockSpec((CTX_LEN, hd), lambda b, h, j: (ctx_blk0 + b, COL_Q + h)),
            pl.BlockSpec((CTX_LEN, hd), lambda b, h, j: (ctx_blk0 + b, COL_K + h)),
            pl.BlockSpec((CTX_LEN, hd), lambda b, h, j: (ctx_blk0 + b, COL_V + h)),
            pl.BlockSpec((SEQ, hd), lambda b, h, j: (0, 0)),
            pl.BlockSpec((SEQ, hd), lambda b, h, j: (0, 0)),
            pl.BlockSpec((2, 1, hd), lambda b, h, j: (0, 0, 0)),
            pl.BlockSpec((None, None, Q_BLK, K_BLK), lambda b, h, j: (h, _bias_pattern(j), 0, 0)),
        ],
        out_specs=[pl.BlockSpec((Q_BLK, hd), lambda b, h, j: (b * lat_blk + j, h)),
                   pl.BlockSpec((CTX_LEN, hd), lambda b, h, j: (b, h))],
        out_shape=[jax.ShapeDtypeStruct((N_LAT, NA_WIDTH), BF16),
                   jax.ShapeDtypeStruct((BATCH * CTX_LEN, NA_WIDTH), BF16)],
        scratch_shapes=[pltpu.VMEM((SEQ, hd), BF16), pltpu.VMEM((SEQ, hd), BF16),
                        pltpu.VMEM((CTX_LEN, hd), BF16), pltpu.VMEM((CTX_LEN, hd), BF16)],
        compiler_params=_cparams(("arbitrary", "arbitrary", "arbitrary")),
        name="na_attention",
    )(proj, proj, proj, proj, proj, proj, cos_t, sin_t, qkg, bias)


def _attention_bias(rpb_j):
    n_dr = 2 * WIN_R - 1
    qcol = np.arange(GRID_W)
    c0 = np.clip(qcol - WIN_C // 2, 0, GRID_W - WIN_C)
    col_ok = (qcol[None, :] >= c0[:, None]) & (qcol[None, :] < c0[:, None] + WIN_C)
    r = rpb_j.astype(F32)
    edge = GRID_W - WIN_C
    ep = jnp.concatenate([jnp.repeat(r[..., :1], edge, axis=-1), r, jnp.repeat(r[..., -1:], edge, axis=-1)],
                         axis=-1)
    t1 = jnp.stack([ep[..., GRID_W - 1 - qc:2 * GRID_W - 1 - qc] for qc in range(GRID_W)], axis=2)
    t1 = jnp.where(jnp.asarray(col_ok)[None, None], t1, -jnp.inf)
    band = t1.transpose(0, 2, 1, 3).reshape(NA_HEADS, GRID_W, n_dr * GRID_W)
    pad_w = Q_ROWS * GRID_W
    band = jnp.pad(band, ((0, 0), (0, 0), (pad_w, pad_w)), constant_values=-jnp.inf)
    strips, masks = [], []
    for jb in (0, N_QBLK // 2, N_QBLK - 1):
        k_first = int(np.clip(Q_ROWS * jb - WIN_R // 2, 0, GRID_ROWS - K_ROWS))
        kr = k_first + np.arange(K_ROWS)
        for a in range(Q_ROWS):
            row = Q_ROWS * jb + a
            r0 = int(np.clip(row - WIN_R // 2, 0, GRID_ROWS - WIN_R))
            off = pad_w + (k_first - row + WIN_R - 1) * GRID_W
            strips.append(band[:, :, off:off + K_BLK])
            masks.append(np.repeat((kr >= r0) & (kr < r0 + WIN_R), GRID_W))
    vals = jnp.stack(strips, axis=1)
    mask = np.stack(masks)[None, :, None, :]
    bias = jnp.where(jnp.asarray(mask), vals, -jnp.inf)
    return bias.reshape(NA_HEADS, 3, Q_BLK, K_BLK)


def _rope_tables():
    quarter = NA_HEAD_DIM // 4
    inv_freq = ROPE_BASE ** (-jnp.arange(quarter, dtype=F32) / quarter)
    t = jnp.arange(SEQ)
    ang_r = (t // GRID_W).astype(F32)[:, None] * inv_freq[None, :]
    ang_c = (t % GRID_W).astype(F32)[:, None] * inv_freq[None, :]
    cos_t = jnp.concatenate([jnp.cos(ang_r), jnp.cos(ang_r), jnp.cos(ang_c), jnp.cos(ang_c)], axis=-1)
    sin_t = jnp.concatenate([-jnp.sin(ang_r), jnp.sin(ang_r), -jnp.sin(ang_c), jnp.sin(ang_c)], axis=-1)
    return cos_t, sin_t


SUBLANE = 8
CONV_HALO = SUBLANE
CONV_BLK = 256


CONV_CW = 512


def _conv_kernel(*refs):
    nc = CONV_CH // CONV_CW
    x_refs, prev_refs, next_refs = refs[:nc], refs[nc:2 * nc], refs[2 * nc:3 * nc]
    w_ref, b_ref, o_ref, pad_scr = refs[3 * nc:]
    t = pl.program_id(0)
    n_lat_blk = N_LAT // CONV_BLK
    is_ctx = t >= n_lat_blk
    seq_blks = jnp.where(is_ctx, CTX_LEN // CONV_BLK, SEQ // CONV_BLK)
    pos = jnp.where(is_ctx, t - n_lat_blk, t) % seq_blks
    for c in range(nc):
        cs = slice(c * CONV_CW, (c + 1) * CONV_CW)
        pad_scr[0:CONV_HALO, :] = jnp.where(pos == 0, 0.0, prev_refs[c][...])
        pad_scr[CONV_HALO:CONV_HALO + CONV_BLK, :] = x_refs[c][...]
        pad_scr[CONV_HALO + CONV_BLK:2 * CONV_HALO + CONV_BLK, :] = jnp.where(pos == seq_blks - 1, 0.0,
                                                                               next_refs[c][...])
        xp = pad_scr[...]
        acc = jnp.zeros((CONV_BLK, CONV_CW), F32) + b_ref[:, cs]
        for k in range(CONV_K):
            lo = CONV_HALO - CONV_K // 2 + k
            acc = acc + w_ref[k:k + 1, cs] * xp[lo:lo + CONV_BLK, :]
        o_ref[:, cs] = _silu(acc)


def _conv_silu(proj, conv_w8, conv_b2, j_even):
    n_rows = proj.shape[0]
    nc = CONV_CH // CONV_CW
    col0 = COL_XBC * LANE // CONV_CW
    halo_per_blk = CONV_BLK // CONV_HALO
    last_halo = n_rows // CONV_HALO - 1
    x_specs = [pl.BlockSpec((CONV_BLK, CONV_CW), lambda t, c=c: (t, col0 + c)) for c in range(nc)]
    prev_specs = [pl.BlockSpec((CONV_HALO, CONV_CW),
                               lambda t, c=c: (jnp.maximum(t * halo_per_blk - 1, 0), col0 + c)) for c in range(nc)]
    next_specs = [pl.BlockSpec((CONV_HALO, CONV_CW),
                               lambda t, c=c: (jnp.minimum((t + 1) * halo_per_blk, last_halo), col0 + c))
                  for c in range(nc)]
    return pl.pallas_call(
        _conv_kernel,
        grid=(n_rows // CONV_BLK,),
        in_specs=x_specs + prev_specs + next_specs + [
            pl.BlockSpec((None, SUBLANE, CONV_CH), lambda t: (j_even, 0, 0)),
            pl.BlockSpec((None, 1, CONV_CH), lambda t: (j_even, 0, 0)),
        ],
        out_specs=pl.BlockSpec((CONV_BLK, CONV_CH), lambda t: (t, 0)),
        out_shape=jax.ShapeDtypeStruct((n_rows, CONV_CH), F32),
        scratch_shapes=[pltpu.VMEM((CONV_BLK + 2 * CONV_HALO, CONV_CW), F32)],
        compiler_params=_cparams(("arbitrary",)),
        name="conv_silu",
    )(*([proj] * (3 * nc)), conv_w8, conv_b2)


def _softplus(x):
    return jnp.maximum(x, 0.0) + jnp.log1p(jnp.exp(-jnp.abs(x)))


def _dot_exact(a, b):
    return jnp.dot(a, b, precision=lax.Precision.HIGHEST, preferred_element_type=F32)


def _ssd_direction(xs_ref, b_ref, c_ref, dt_ref, dtb_ref, alog_ref, y_ref, h_scr, lane0, forward):
    q = SSD_CHUNK
    gw = SSD_INNER // SSD_GROUPS
    dt = _softplus(dt_ref[...] + dtb_ref[...])
    a = dt * (-jnp.exp(alog_ref[...]))
    ri = lax.broadcasted_iota(jnp.int32, (q, q), 0)
    ci = lax.broadcasted_iota(jnp.int32, (q, q), 1)
    tri = (ci <= ri) if forward else (ci >= ri)
    a_cum = _dot_exact(tri.astype(F32), a)
    a_cum_t = a_cum.T
    er = lax.broadcasted_iota(jnp.int32, (LANE, SSD_INNER), 0)
    ec = lax.broadcasted_iota(jnp.int32, (LANE, SSD_INNER), 1)
    expand = ((er - lane0) == (ec // SSD_HEAD_DIM)).astype(F32)
    dt_e = _dot_exact(dt, expand)
    ac_e = _dot_exact(a_cum, expand)
    end = q - 1 if forward else 0
    a_end_e = ac_e[end:end + 1, :]
    xdt = xs_ref[...] * dt_e
    in_decay = jnp.exp(ac_e)
    out_decay = jnp.exp(a_end_e - ac_e)
    state_decay = jnp.exp(a_end_e)
    lane = lax.broadcasted_iota(jnp.int32, (q, LANE), 1)
    for g in range(SSD_GROUPS):
        gs = slice(g * gw, (g + 1) * gw)
        bg_t = b_ref[:, g * SSD_STATE:(g + 1) * SSD_STATE].T.astype(BF16)
        cg = c_ref[:, g * SSD_STATE:(g + 1) * SSD_STATE].astype(BF16)
        cb = jnp.dot(cg, bg_t, preferred_element_type=F32)
        h_t = h_scr[:, gs]
        y_inter = jnp.dot(cg, h_t.astype(BF16), preferred_element_type=F32) * in_decay[:, gs]
        for pair in range(gw // LANE):
            cs = slice(g * gw + pair * LANE, g * gw + (pair + 1) * LANE)
            x_pair = xdt[:, cs].astype(BF16)
            res = []
            for sub in range(LANE // SSD_HEAD_DIM):
                hl = lane0 + (g * gw + pair * LANE) // SSD_HEAD_DIM + sub
                seg = a_cum[:, hl:hl + 1] - a_cum_t[hl:hl + 1, :]
                decay = jnp.exp(jnp.where(tri, seg, -jnp.inf))
                res.append(jnp.dot((cb * decay).astype(BF16), x_pair, preferred_element_type=F32))
            y_pair = jnp.where(lane < SSD_HEAD_DIM, res[0], res[1])
            y_ref[:, cs] = y_pair + y_inter[:, pair * LANE:(pair + 1) * LANE]
        x_out = (xdt[:, gs] * out_decay[:, gs]).astype(BF16)
        h_scr[:, gs] = state_decay[:, gs] * h_t + jnp.dot(bg_t, x_out, preferred_element_type=F32)


def _ssd_kernel(xs_f, b_f, c_f, dt_f, xs_b, b_b, c_b, dt_b, dtb_ref, alog_ref, yf_ref, yb_ref,
                hf_scr, hb_scr):
    @pl.when(pl.program_id(1) == 0)
    def _():
        hf_scr[...] = jnp.zeros_like(hf_scr)
        hb_scr[...] = jnp.zeros_like(hb_scr)

    _ssd_direction(xs_f, b_f, c_f, dt_f, dtb_ref, alog_ref, yf_ref, hf_scr, 0, True)
    _ssd_direction(xs_b, b_b, c_b, dt_b, dtb_ref, alog_ref, yb_ref, hb_scr, SSD_HEADS, False)


def _ssd_scan(act, proj, dtb, alog, j_even):
    q = SSD_CHUNK
    n_rows = act.shape[0]
    n_ctx = CTX_LEN // q
    n_lat = SEQ // q
    ctx_blk0 = N_LAT // q

    def fwd_blk(b, s):
        return jnp.where(s < n_ctx, ctx_blk0 + b * n_ctx + s, b * n_lat + (s - n_ctx))

    def bwd_blk(b, s):
        return jnp.where(s < n_ctx, ctx_blk0 + b * n_ctx + (n_ctx - 1 - s),
                         b * n_lat + (n_lat - 1 - (s - n_ctx)))

    gn = SSD_GROUPS * SSD_STATE
    b_col = SSD_INNER // gn
    c_col = b_col + 1

    def specs(blk):
        return [
            pl.BlockSpec((q, SSD_INNER), lambda b, s: (blk(b, s), 0)),
            pl.BlockSpec((q, gn), lambda b, s: (blk(b, s), b_col)),
            pl.BlockSpec((q, gn), lambda b, s: (blk(b, s), c_col)),
            pl.BlockSpec((q, LANE), lambda b, s: (blk(b, s), COL_DT)),
        ]

    small = pl.BlockSpec((None, 1, LANE), lambda b, s: (j_even, 0, 0))
    return pl.pallas_call(
        _ssd_kernel,
        grid=(BATCH, n_ctx + n_lat),
        in_specs=specs(fwd_blk) + specs(bwd_blk) + [small, small],
        out_specs=[pl.BlockSpec((q, SSD_INNER), lambda b, s: (fwd_blk(b, s), 0)),
                   pl.BlockSpec((q, SSD_INNER), lambda b, s: (bwd_blk(b, s), 0))],
        out_shape=[jax.ShapeDtypeStruct((n_rows, SSD_INNER), F32)] * 2,
        scratch_shapes=[pltpu.VMEM((SSD_STATE, SSD_INNER), F32)] * 2,
        compiler_params=_cparams(("arbitrary", "arbitrary")),
        name="ssd_scan",
    )(act, act, act, proj, act, act, act, proj, dtb, alog)


def _out_proj_kernel(attn_ref, yf_ref, yb_ref, xs_ref, z0_ref, z1_ref, dsk_ref, ng_ref, w_ref, x_ref, mod_ref,
                     o_ref):
    mix = jnp.dot(attn_ref[...], w_ref[0:NA_WIDTH, :], preferred_element_type=F32)
    gw = SSD_INNER // SSD_GROUPS
    for g, z_ref in enumerate((z0_ref, z1_ref)):
        gs = slice(g * gw, (g + 1) * gw)
        y = yf_ref[:, gs] + yb_ref[:, gs] + dsk_ref[:, gs] * xs_ref[:, gs]
        seg = y * _silu(z_ref[...])
        ms = jnp.mean(seg * seg, axis=-1, keepdims=True)
        yn = (seg * lax.rsqrt(ms + EPS) * ng_ref[:, g * gw:(g + 1) * gw]).astype(BF16)
        mix = mix + jnp.dot(yn, w_ref[NA_WIDTH + g * gw:NA_WIDTH + (g + 1) * gw, :],
                            preferred_element_type=F32)
    o_ref[...] = x_ref[...] + mod_ref[2] * mix


def _out_proj(attn, y_f, y_b, act, proj, dsk, ng, w_out, x, n_rows, mods_i, j_even):
    tm = 256
    mrow = _mod_row(tm)
    gw = SSD_INNER // SSD_GROUPS
    z_col = COL_Z * LANE // gw
    assert SSD_GROUPS == 2
    return pl.pallas_call(
        _out_proj_kernel,
        grid=(n_rows // tm,),
        in_specs=[
            pl.BlockSpec((tm, NA_WIDTH), lambda t: (t, 0)),
            pl.BlockSpec((tm, SSD_INNER), lambda t: (t, 0)),
            pl.BlockSpec((tm, SSD_INNER), lambda t: (t, 0)),
            pl.BlockSpec((tm, SSD_INNER), lambda t: (t, 0)),
            pl.BlockSpec((tm, gw), lambda t: (t, z_col)),
            pl.BlockSpec((tm, gw), lambda t: (t, z_col + 1)),
            pl.BlockSpec((None, 1, SSD_INNER), lambda t: (j_even, 0, 0)),
            pl.BlockSpec((None, 1, SSD_INNER), lambda t: (j_even, 0, 0)),
            pl.BlockSpec((None, D_MODEL, D_MODEL), lambda t: (j_even, 0, 0)),
            pl.BlockSpec((tm, D_MODEL), lambda t: (t, 0)),
            pl.BlockSpec((None, 3, 1, D_MODEL), lambda t: (mrow(t), 1, 0, 0)),
        ],
        out_specs=pl.BlockSpec((tm, D_MODEL), lambda t: (t, 0)),
        out_shape=jax.ShapeDtypeStruct((n_rows, D_MODEL), F32),
        compiler_params=_cparams(("arbitrary",)),
        name="out_proj",
    )(attn, y_f, y_b, act, proj, proj, dsk, ng, w_out, x, mods_i)


def _fourier_chan_kernel(x_ref, mod_ref, g_ref, cs_ref, p_ref, q_ref):
    h = _mod_norm(x_ref[...], g_ref[...], mod_ref[0], mod_ref[1])
    for g in range(F_GROUPS):
        gs = slice(g * F_GROUP_CH, (g + 1) * F_GROUP_CH)
        pq = jnp.dot(h[:, gs].astype(BF16), cs_ref[...], preferred_element_type=F32)
        p_ref[:, gs] = pq[:, :F_GROUP_CH].astype(BF16)
        q_ref[:, gs] = pq[:, F_GROUP_CH:].astype(BF16)


def _fourier_chan(x, n_rows, mods_i, g, cs):
    tm = 512
    mrow = _mod_row(tm)
    return pl.pallas_call(
        _fourier_chan_kernel,
        grid=(n_rows // tm,),
        in_specs=[
            pl.BlockSpec((tm, D_MODEL), lambda t: (t, 0)),
            pl.BlockSpec((None, 3, 1, D_MODEL), lambda t: (mrow(t), 1, 0, 0)),
            pl.BlockSpec((1, D_MODEL), lambda t: (0, 0)),
            pl.BlockSpec((F_GROUP_CH, 2 * F_GROUP_CH), lambda t: (0, 0)),
        ],
        out_specs=[pl.BlockSpec((tm, D_MODEL), lambda t: (t, 0))] * 2,
        out_shape=[jax.ShapeDtypeStruct((n_rows, D_MODEL), BF16)] * 2,
        compiler_params=_cparams(("arbitrary",)),
        name="fourier_chan",
    )(x, mods_i, g, cs)


DFT_SPLIT = 64


def _fourier_pos_kernel(u_ref, v_ref, p_ref, q_ref, w_ref, x_ref, mod_ref, o_ref, acc_scr, c_scr, s_scr, *,
                        scale):
    k = pl.program_id(2)

    @pl.when(k == 0)
    def _():
        acc_scr[...] = jnp.zeros_like(acc_scr)

    vr, vi = v_ref[0], v_ref[1]
    for a in range(u_ref.shape[1]):
        ur, ui = u_ref[0, a:a + 1, :], u_ref[1, a:a + 1, :]
        rows = slice(a * DFT_SPLIT, (a + 1) * DFT_SPLIT)
        c_scr[rows, :] = (ur * vr - ui * vi).astype(BF16)
        s_scr[rows, :] = (ui * vr + ur * vi).astype(BF16)
    acc_scr[...] += (jnp.dot(c_scr[...], p_ref[...], preferred_element_type=F32)
                     - jnp.dot(s_scr[...], q_ref[...], preferred_element_type=F32))

    @pl.when(k == pl.num_programs(2) - 1)
    def _():
        f = (acc_scr[...] * scale).astype(BF16)
        o_ref[...] = x_ref[...] + mod_ref[2] * jnp.dot(f, w_ref[...], preferred_element_type=F32)


def _fourier_pos(tabs, p, q, w, x, mods_i, j_odd, seq_len, row0, is_ctx):
    u_tab, v_tab = tabs
    tm = tk = min(seq_len, 512)
    nm = seq_len // tm
    blk0 = row0 // tm
    scale = float((seq_len * F_GROUP_CH) ** -0.5)
    return pl.pallas_call(
        functools.partial(_fourier_pos_kernel, scale=scale),
        grid=(BATCH, nm, seq_len // tk),
        in_specs=[
            pl.BlockSpec((2, tm // DFT_SPLIT, tk), lambda b, m, k: (0, m, k)),
            pl.BlockSpec((2, DFT_SPLIT, tk), lambda b, m, k: (0, 0, k)),
            pl.BlockSpec((tk, D_MODEL), lambda b, m, k: (blk0 + b * nm + k, 0)),
            pl.BlockSpec((tk, D_MODEL), lambda b, m, k: (blk0 + b * nm + k, 0)),
            pl.BlockSpec((None, D_MODEL, D_MODEL), lambda b, m, k: (j_odd, 0, 0)),
            pl.BlockSpec((tm, D_MODEL), lambda b, m, k: (blk0 + b * nm + m, 0)),
            pl.BlockSpec((None, 3, 1, D_MODEL), lambda b, m, k: (BATCH if is_ctx else b, 1, 0, 0)),
        ],
        out_specs=pl.BlockSpec((tm, D_MODEL), lambda b, m, k: (b * nm + m, 0)),
        out_shape=jax.ShapeDtypeStruct((BATCH * seq_len, D_MODEL), F32),
        scratch_shapes=[pltpu.VMEM((tm, D_MODEL), F32), pltpu.VMEM((tm, tk), BF16), pltpu.VMEM((tm, tk), BF16)],
        compiler_params=_cparams(("arbitrary", "arbitrary", "arbitrary")),
        name="fourier_pos",
    )(u_tab, v_tab, p, q, w, x, mods_i)


def _phase_table(freq, n):
    l = jnp.arange(n, dtype=jnp.int32)
    ang = ((freq[:, None] * l[None, :]) % n).astype(F32) * np.float32(2.0 * np.pi / n)
    return jnp.stack([jnp.cos(ang), jnp.sin(ang)])


def _dft_tables(n):
    a = jnp.arange(n // DFT_SPLIT, dtype=jnp.int32) * DFT_SPLIT
    b = jnp.arange(DFT_SPLIT, dtype=jnp.int32)
    return _phase_table(a, n), _phase_table(b, n)


def kernel(x, c, ctx, c_ctx, mod_w, mod_b, norm_g, ffn_w1, ffn_w3, ffn_w2, mix_w_in, mix_w_out, qk_g, rpb,
           conv_w, conv_b, dt_bias, a_log, ssd_d, ssd_norm_g, fourier_w):
    n_even = mix_w_in.shape[0]
    xs = jnp.concatenate([x.reshape(N_LAT, D_MODEL), ctx.reshape(BATCH * CTX_LEN, D_MODEL)], axis=0)
    mods = _modulation(c, c_ctx, mod_w, mod_b)

    ffn_w = tuple(w[0, 0].astype(BF16) for w in (ffn_w1, ffn_w3, ffn_w2))

    def next_ffn(i, which):
        nxt = 2 * i + which + 1
        return None if nxt == 2 * DEPTH else (ffn_w1, ffn_w3, ffn_w2, nxt // 2, nxt % 2)

    w_in = jnp.pad(mix_w_in, ((0, 0), (0, 0), (0, IN_W_PAD - IN_W))).astype(BF16)
    w_out = mix_w_out.astype(BF16)
    w_f = fourier_w.astype(BF16)
    qkg = qk_g.reshape(n_even, 2, 1, NA_HEAD_DIM)
    conv_w8 = jnp.pad(conv_w, ((0, 0), (0, SUBLANE - CONV_K), (0, 0)))
    conv_b2 = conv_b.reshape(n_even, 1, CONV_CH)
    pad_l = LANE - 2 * SSD_HEADS
    dtb = jnp.pad(dt_bias.reshape(n_even, 1, 2 * SSD_HEADS), ((0, 0), (0, 0), (0, pad_l)))
    alog = jnp.pad(a_log.reshape(n_even, 1, 2 * SSD_HEADS), ((0, 0), (0, 0), (0, pad_l)))
    dsk = jnp.repeat(ssd_d, SSD_HEAD_DIM, axis=-1).reshape(n_even, 1, SSD_INNER)
    ng = ssd_norm_g.reshape(n_even, 1, SSD_INNER)
    cos_t, sin_t = _rope_tables()
    chan = _phase_table(jnp.arange(F_GROUP_CH, dtype=jnp.int32), F_GROUP_CH)
    cs_chan = jnp.concatenate([chan[0], chan[1]], axis=1).astype(BF16)
    dft_lat = _dft_tables(SEQ)
    dft_ctx = _dft_tables(CTX_LEN)

    last_ctx = ((DEPTH - 1) // 2) * 2
    for i in range(DEPTH):
        use_ctx = i <= last_ctx
        ctx_out = i < last_ctx
        j = i // 2
        mods_i = mods[i]
        g = norm_g[i].reshape(3, 1, D_MODEL)
        rows_in = N_TOK if use_ctx else N_LAT
        rows_out = N_TOK if ctx_out else N_LAT
        xs, ffn_w = _ffn(xs, rows_in, mods_i, 0, g[0], ffn_w, next_ffn(i, 0))
        if i % 2 == 0:
            assert use_ctx, "even mixer layers read the context keys and SSD states"
            proj = _in_proj(xs, rows_in, mods_i, g[1], w_in, j)
            bias = _attention_bias(rpb[j])
            attn, attn_ctx = _na_attention(proj, cos_t, sin_t, qkg[j], bias)
            if ctx_out:
                attn = jnp.concatenate([attn, attn_ctx], axis=0)
            act = _conv_silu(proj, conv_w8, conv_b2, j)
            y_f, y_b = _ssd_scan(act, proj, dtb, alog, j)
            xs = _out_proj(attn, y_f, y_b, act, proj, dsk, ng, w_out, xs, rows_out, mods_i, j)
        else:
            p, q = _fourier_chan(xs, rows_out, mods_i, g[1], cs_chan)
            new = _fourier_pos(dft_lat, p, q, w_f, xs, mods_i, j, SEQ, 0, False)
            if ctx_out:
                new_ctx = _fourier_pos(dft_ctx, p, q, w_f, xs, mods_i, j, CTX_LEN, N_LAT, True)
                new = jnp.concatenate([new, new_ctx], axis=0)
            xs = new
        xs, ffn_w = _ffn(xs, rows_out, mods_i, 2, g[2], ffn_w, next_ffn(i, 1))
    return xs[:N_LAT].reshape(BATCH, SEQ, D_MODEL)
```

```python
import functools

import numpy as np
import jax
import jax.numpy as jnp
from jax import lax
from jax.experimental import pallas as pl
from jax.experimental.pallas import tpu as pltpu

F32 = jnp.float32
BF16 = jnp.bfloat16

D_MODEL = 2048
BATCH = 2
SEQ = 4096
DEPTH = 4
GRID_W = 64
GRID_ROWS = SEQ // GRID_W
CTX_LEN = 256
N_LAT = BATCH * SEQ
N_TOK = N_LAT + BATCH * CTX_LEN
NA_HEADS = 8
NA_HEAD_DIM = 128
NA_WIDTH = NA_HEADS * NA_HEAD_DIM
WIN_R = 8
WIN_C = 16
ROPE_BASE = 10000.0
SSD_HEADS = 16
SSD_HEAD_DIM = 64
SSD_INNER = SSD_HEADS * SSD_HEAD_DIM
SSD_GROUPS = 2
SSD_STATE = 128
SSD_CHUNK = 128
CONV_K = 5
CONV_CH = SSD_INNER + 2 * SSD_GROUPS * SSD_STATE
IN_W = 3 * NA_WIDTH + SSD_INNER + CONV_CH + 2 * SSD_HEADS
LANE = 128
IN_W_PAD = ((IN_W + LANE - 1) // LANE) * LANE
F_GROUPS = 4
F_GROUP_CH = D_MODEL // F_GROUPS
D_FF = 5632
N_MOD = 9
EPS = 1e-6
ATTN_SCALE = NA_HEAD_DIM ** -0.5

COL_Q = 0
COL_K = NA_WIDTH // LANE
COL_V = 2 * NA_WIDTH // LANE
COL_Z = 3 * NA_WIDTH // LANE
COL_XBC = (3 * NA_WIDTH + SSD_INNER) // LANE
COL_DT = (3 * NA_WIDTH + SSD_INNER + CONV_CH) // LANE

Q_ROWS = 8
Q_BLK = Q_ROWS * GRID_W
Q_SPLIT = 2
PART_ROWS = Q_ROWS // Q_SPLIT
PART_Q = PART_ROWS * GRID_W
K_ROWS = 12
K_BLK = K_ROWS * GRID_W
N_QBLK = GRID_ROWS // Q_ROWS

VMEM_LIMIT = 56 * 1024 * 1024


def _cparams(sem, vmem_limit=VMEM_LIMIT):
    return pltpu.CompilerParams(dimension_semantics=sem, vmem_limit_bytes=vmem_limit)


def _sigmoid(x):
    return 1.0 / (1.0 + jnp.exp(-x))


def _silu(x):
    return x * _sigmoid(x)


def _mod_norm(x, g, shift, scale):
    ms = jnp.mean(x * x, axis=-1, keepdims=True)
    y = x * lax.rsqrt(ms + EPS) * g
    return y * (1.0 + scale) + shift


def _mod_row(tile_rows):
    def f(t):
        return jnp.minimum((t * tile_rows) // SEQ, BATCH)
    return f


def _mod_kernel(c_ref, w_ref, b_ref, o_ref):
    s = _silu(c_ref[...]).astype(BF16)
    o_ref[...] = jnp.dot(s, w_ref[...].astype(BF16), preferred_element_type=F32) + b_ref[...]


def _modulation(c, c_ctx, mod_w, mod_b):
    rows = 8
    cvec = jnp.concatenate([c, c_ctx[None, :], jnp.zeros((rows - BATCH - 1, D_MODEL), F32)], axis=0)
    n = N_MOD * D_MODEL
    tn = 1024
    out = pl.pallas_call(
        _mod_kernel,
        grid=(DEPTH, n // tn),
        in_specs=[
            pl.BlockSpec((rows, D_MODEL), lambda i, j: (0, 0)),
            pl.BlockSpec((None, D_MODEL, tn), lambda i, j: (i, 0, j)),
            pl.BlockSpec((None, 1, tn), lambda i, j: (i, 0, j)),
        ],
        out_specs=pl.BlockSpec((None, rows, tn), lambda i, j: (i, 0, j)),
        out_shape=jax.ShapeDtypeStruct((DEPTH, rows, n), F32),
        compiler_params=_cparams(("arbitrary", "arbitrary")),
        name="modulation",
    )(cvec, mod_w, mod_b.reshape(DEPTH, 1, n))
    return out[:, :BATCH + 1].reshape(DEPTH, BATCH + 1, N_MOD, 1, D_MODEL)


FFN_TM = 512
FFN_TF = 512


def _ffn_kernel(x_ref, mod_ref, g_ref, w13_ref, w2_ref, o_ref, h_scr, acc_scr):
    j = pl.program_id(1)

    @pl.when(j == 0)
    def _():
        h_scr[...] = _mod_norm(x_ref[...], g_ref[...], mod_ref[0], mod_ref[1]).astype(BF16)
        acc_scr[...] = jnp.zeros_like(acc_scr)

    ab = jnp.dot(h_scr[...], w13_ref[...], preferred_element_type=F32)
    u = (_silu(ab[:, :FFN_TF]) * ab[:, FFN_TF:]).astype(BF16)
    acc_scr[...] += jnp.dot(u, w2_ref[...], preferred_element_type=F32)

    @pl.when(j == pl.num_programs(1) - 1)
    def _():
        o_ref[...] = x_ref[...] + 0.5 * mod_ref[2] * acc_scr[...]


def _ffn(x, n_rows, mods_i, sub, g, w13, w2, layer, which):
    tm, tf = FFN_TM, FFN_TF
    mrow = _mod_row(tm)
    return pl.pallas_call(
        _ffn_kernel,
        grid=(n_rows // tm, D_FF // tf),
        in_specs=[
            pl.BlockSpec((tm, D_MODEL), lambda t, j: (t, 0)),
            pl.BlockSpec((None, 3, 1, D_MODEL), lambda t, j: (mrow(t), sub, 0, 0)),
            pl.BlockSpec((1, D_MODEL), lambda t, j: (0, 0)),
            pl.BlockSpec((None, None, D_MODEL, 2 * tf), lambda t, j: (layer, which, 0, j)),
            pl.BlockSpec((None, None, tf, D_MODEL), lambda t, j: (layer, which, j, 0)),
        ],
        out_specs=pl.BlockSpec((tm, D_MODEL), lambda t, j: (t, 0)),
        out_shape=jax.ShapeDtypeStruct((n_rows, D_MODEL), F32),
        scratch_shapes=[pltpu.VMEM((tm, D_MODEL), BF16), pltpu.VMEM((tm, D_MODEL), F32)],
        compiler_params=_cparams(("arbitrary", "arbitrary")),
        name="ffn",
    )(x, mods_i, g, w13, w2)


def _norm_matmul_kernel(x_ref, mod_ref, g_ref, w_ref, o_ref, h_scr):
    @pl.when(pl.program_id(1) == 0)
    def _():
        h_scr[...] = _mod_norm(x_ref[...], g_ref[...], mod_ref[0], mod_ref[1]).astype(BF16)

    o_ref[...] = jnp.dot(h_scr[...], w_ref[...], preferred_element_type=F32)


def _in_proj(x, n_rows, mods_i, g, w_in, j_even):
    tm, tn = 512, 1920
    mrow = _mod_row(tm)
    return pl.pallas_call(
        _norm_matmul_kernel,
        grid=(n_rows // tm, IN_W_PAD // tn),
        in_specs=[
            pl.BlockSpec((tm, D_MODEL), lambda t, j: (t, 0)),
            pl.BlockSpec((None, 3, 1, D_MODEL), lambda t, j: (mrow(t), 1, 0, 0)),
            pl.BlockSpec((1, D_MODEL), lambda t, j: (0, 0)),
            pl.BlockSpec((None, D_MODEL, tn), lambda t, j: (j_even, 0, j)),
        ],
        out_specs=pl.BlockSpec((tm, tn), lambda t, j: (t, j)),
        out_shape=jax.ShapeDtypeStruct((n_rows, IN_W_PAD), F32),
        scratch_shapes=[pltpu.VMEM((tm, D_MODEL), BF16)],
        compiler_params=_cparams(("arbitrary", "arbitrary")),
        name="in_proj",
    )(x, mods_i, g, w_in)


def _head_rms(x, g):
    ms = jnp.mean(x * x, axis=-1, keepdims=True)
    return x * lax.rsqrt(ms + EPS) * g


def _rope(x, cos, sin):
    lane = lax.broadcasted_iota(jnp.int32, x.shape, 1)
    first_half = (lane & (NA_HEAD_DIM // 4)) == 0
    partner = jnp.where(first_half, pltpu.roll(x, NA_HEAD_DIM - NA_HEAD_DIM // 4, 1),
                        pltpu.roll(x, NA_HEAD_DIM // 4, 1))
    return x * cos + partner * sin


def _dot_nt(a, b):
    return lax.dot_general(a, b, (((1,), (1,)), ((), ())), preferred_element_type=F32)


def _softmax_pv(scores, values):
    m = functools.reduce(jnp.maximum, [jnp.max(s, axis=-1, keepdims=True) for s in scores])
    ps = [jnp.exp(s - m) for s in scores]
    denom = functools.reduce(jnp.add, [jnp.sum(p, axis=-1, keepdims=True) for p in ps])
    o = functools.reduce(jnp.add, [jnp.dot(p.astype(BF16), v, preferred_element_type=F32)
                                   for p, v in zip(ps, values)])
    return o / denom


def _part_key_row(j, part):
    return jnp.clip(Q_ROWS * j + PART_ROWS * part - WIN_R // 2, 0, GRID_ROWS - K_ROWS)


def _na_kernel(*refs):
    (q_ref, k_ref, v_ref, qc_ref, kc_ref, vc_ref, cos_ref, sin_ref, qkg_ref), refs = refs[:9], refs[9:]
    bias_refs, (o_ref, oc_ref, k_scr, v_scr, kc_scr, vc_scr) = refs[:Q_SPLIT], refs[Q_SPLIT:]
    j = pl.program_id(2)

    @pl.when(j == 0)
    def _():
        kn = _head_rms(k_ref[...], qkg_ref[1])
        k_scr[...] = _rope(kn, cos_ref[...], sin_ref[...]).astype(BF16)
        v_scr[...] = v_ref[...].astype(BF16)
        kcn = _head_rms(kc_ref[...], qkg_ref[1]).astype(BF16)
        vcb = vc_ref[...].astype(BF16)
        kc_scr[...] = kcn
        vc_scr[...] = vcb
        qcn = _head_rms(qc_ref[...], qkg_ref[0]).astype(BF16)
        oc_ref[...] = _softmax_pv([_dot_nt(qcn, kcn) * ATTN_SCALE], [vcb]).astype(oc_ref.dtype)

    for part in range(Q_SPLIT):
        rows = pl.ds(part * PART_Q, PART_Q)
        q0 = pl.multiple_of(j * Q_BLK + part * PART_Q, PART_Q)
        k0 = pl.multiple_of(_part_key_row(j, part) * GRID_W, GRID_W)
        kw = k_scr[pl.ds(k0, K_BLK), :]
        vw = v_scr[pl.ds(k0, K_BLK), :]
        qn = _head_rms(q_ref[rows, :], qkg_ref[0])
        qr = _rope(qn, cos_ref[pl.ds(q0, PART_Q), :], sin_ref[pl.ds(q0, PART_Q), :]).astype(BF16)
        s_win = _dot_nt(qr, kw) * ATTN_SCALE + bias_refs[part][...]
        s_ctx = _dot_nt(qn.astype(BF16), kc_scr[...]) * ATTN_SCALE
        o_ref[rows, :] = _softmax_pv([s_win, s_ctx], [vw, vc_scr[...]]).astype(o_ref.dtype)


BIAS_PARTS = ((0, 0), (N_QBLK // 2, 0), (N_QBLK - 1, Q_SPLIT - 1))


def _bias_pattern(j, part):
    first = (j == 0) & (part == 0)
    last = (j == N_QBLK - 1) & (part == Q_SPLIT - 1)
    return jnp.where(first, 0, jnp.where(last, 2, 1))


def _na_attention(proj, cos_t, sin_t, qkg, bias):
    lat_blk = SEQ // Q_BLK
    ctx_blk0 = N_LAT // CTX_LEN
    hd = NA_HEAD_DIM
    bias_specs = [pl.BlockSpec((None, None, PART_Q, K_BLK), lambda b, h, j, p=p: (h, _bias_pattern(j, p), 0, 0))
                  for p in range(Q_SPLIT)]
    return pl.pallas_call(
        _na_kernel,
        grid=(BATCH, NA_HEADS, N_QBLK),
        in_specs=[
            pl.BlockSpec((Q_BLK, hd), lambda b, h, j: (b * lat_blk + j, COL_Q + h)),
            pl.BlockSpec((SEQ, hd), lambda b, h, j: (b, COL_K + h)),
            pl.BlockSpec((SEQ, hd), lambda b, h, j: (b, COL_V + h)),
            pl.BlockSpec((CTX_LEN, hd), lambda b, h, j: (ctx_blk0 + b, COL_Q + h)),
            pl.BlockSpec((CTX_LEN, hd), lambda b, h, j: (ctx_blk0 + b, COL_K + h)),
            pl.BlockSpec((CTX_LEN, hd), lambda b, h, j: (ctx_blk0 + b, COL_V + h)),
            pl.BlockSpec((SEQ, hd), lambda b, h, j: (0, 0)),
            pl.BlockSpec((SEQ, hd), lambda b, h, j: (0, 0)),
            pl.BlockSpec((2, 1, hd), lambda b, h, j: (0, 0, 0)),
        ] + bias_specs,
        out_specs=[pl.BlockSpec((Q_BLK, hd), lambda b, h, j: (b * lat_blk + j, h)),
                   pl.BlockSpec((CTX_LEN, hd), lambda b, h, j: (b, h))],
        out_shape=[jax.ShapeDtypeStruct((N_LAT, NA_WIDTH), BF16),
                   jax.ShapeDtypeStruct((BATCH * CTX_LEN, NA_WIDTH), BF16)],
        scratch_shapes=[pltpu.VMEM((SEQ, hd), BF16), pltpu.VMEM((SEQ, hd), BF16),
                        pltpu.VMEM((CTX_LEN, hd), BF16), pltpu.VMEM((CTX_LEN, hd), BF16)],
        compiler_params=_cparams(("arbitrary", "arbitrary", "arbitrary")),
        name="na_attention",
    )(proj, proj, proj, proj, proj, proj, cos_t, sin_t, qkg, *([bias] * Q_SPLIT))


def _part_geometry(jb, part):
    rows = Q_ROWS * jb + PART_ROWS * part + np.arange(PART_ROWS)
    k_first = int(np.clip(rows[0] - WIN_R // 2, 0, GRID_ROWS - K_ROWS))
    r0 = np.clip(rows - WIN_R // 2, 0, GRID_ROWS - WIN_R)
    return rows, k_first, r0


def _attention_bias(rpb_j):
    for jb in range(N_QBLK):
        for part in range(Q_SPLIT):
            rows, k_first, r0 = _part_geometry(jb, part)
            pat = 0 if (jb, part) == (0, 0) else 2 if (jb, part) == (N_QBLK - 1, Q_SPLIT - 1) else 1
            rows_p, k_first_p, r0_p = _part_geometry(*BIAS_PARTS[pat])
            assert k_first - rows[0] == k_first_p - rows_p[0] and np.array_equal(r0 - rows, r0_p - rows_p)
    n_dr = 2 * WIN_R - 1
    qcol = np.arange(GRID_W)
    c0 = np.clip(qcol - WIN_C // 2, 0, GRID_W - WIN_C)
    col_ok = (qcol[None, :] >= c0[:, None]) & (qcol[None, :] < c0[:, None] + WIN_C)
    r = rpb_j.astype(F32)
    edge = GRID_W - WIN_C
    ep = jnp.concatenate([jnp.repeat(r[..., :1], edge, axis=-1), r, jnp.repeat(r[..., -1:], edge, axis=-1)],
                         axis=-1)
    t1 = jnp.stack([ep[..., GRID_W - 1 - qc:2 * GRID_W - 1 - qc] for qc in range(GRID_W)], axis=2)
    t1 = jnp.where(jnp.asarray(col_ok)[None, None], t1, -jnp.inf)
    band = t1.transpose(0, 2, 1, 3).reshape(NA_HEADS, GRID_W, n_dr * GRID_W)
    pad_w = K_BLK
    band = jnp.pad(band, ((0, 0), (0, 0), (pad_w, pad_w)), constant_values=-jnp.inf)
    strips, masks = [], []
    for jb, part in BIAS_PARTS:
        rows, k_first, r0 = _part_geometry(jb, part)
        kr = k_first + np.arange(K_ROWS)
        for row, row0 in zip(rows, r0):
            off = pad_w + (k_first - int(row) + WIN_R - 1) * GRID_W
            assert 0 <= off and off + K_BLK <= band.shape[-1]
            strips.append(band[:, :, off:off + K_BLK])
            masks.append(np.repeat((kr >= row0) & (kr < row0 + WIN_R), GRID_W))
    vals = jnp.stack(strips, axis=1)
    mask = np.stack(masks)[None, :, None, :]
    bias = jnp.where(jnp.asarray(mask), vals, -jnp.inf)
    return bias.reshape(NA_HEADS, len(BIAS_PARTS), PART_Q, K_BLK)


def _rope_tables():
    quarter = NA_HEAD_DIM // 4
    inv_freq = ROPE_BASE ** (-jnp.arange(quarter, dtype=F32) / quarter)
    t = jnp.arange(SEQ)
    ang_r = (t // GRID_W).astype(F32)[:, None] * inv_freq[None, :]
    ang_c = (t % GRID_W).astype(F32)[:, None] * inv_freq[None, :]
    cos_t = jnp.concatenate([jnp.cos(ang_r), jnp.cos(ang_r), jnp.cos(ang_c), jnp.cos(ang_c)], axis=-1)
    sin_t = jnp.concatenate([-jnp.sin(ang_r), jnp.sin(ang_r), -jnp.sin(ang_c), jnp.sin(ang_c)], axis=-1)
    return cos_t, sin_t


SUBLANE = 8
CONV_HALO = SUBLANE
CONV_BLK = 256


CONV_CW = 512


def _conv_kernel(*refs):
    nc = CONV_CH // CONV_CW
    x_refs, prev_refs, next_refs = refs[:nc], refs[nc:2 * nc], refs[2 * nc:3 * nc]
    w_ref, b_ref, o_ref, pad_scr = refs[3 * nc:]
    t = pl.program_id(0)
    n_lat_blk = N_LAT // CONV_BLK
    is_ctx = t >= n_lat_blk
    seq_blks = jnp.where(is_ctx, CTX_LEN // CONV_BLK, SEQ // CONV_BLK)
    pos = jnp.where(is_ctx, t - n_lat_blk, t) % seq_blks
    for c in range(nc):
        cs = slice(c * CONV_CW, (c + 1) * CONV_CW)
        pad_scr[0:CONV_HALO, :] = jnp.where(pos == 0, 0.0, prev_refs[c][...])
        pad_scr[CONV_HALO:CONV_HALO + CONV_BLK, :] = x_refs[c][...]
        pad_scr[CONV_HALO + CONV_BLK:2 * CONV_HALO + CONV_BLK, :] = jnp.where(pos == seq_blks - 1, 0.0,
                                                                               next_refs[c][...])
        xp = pad_scr[...]
        acc = jnp.zeros((CONV_BLK, CONV_CW), F32) + b_ref[:, cs]
        for k in range(CONV_K):
            lo = CONV_HALO - CONV_K // 2 + k
            acc = acc + w_ref[k:k + 1, cs] * xp[lo:lo + CONV_BLK, :]
        o_ref[:, cs] = _silu(acc)


def _conv_silu(proj, conv_w8, conv_b2, j_even):
    n_rows = proj.shape[0]
    nc = CONV_CH // CONV_CW
    col0 = COL_XBC * LANE // CONV_CW
    halo_per_blk = CONV_BLK // CONV_HALO
    last_halo = n_rows // CONV_HALO - 1
    x_specs = [pl.BlockSpec((CONV_BLK, CONV_CW), lambda t, c=c: (t, col0 + c)) for c in range(nc)]
    prev_specs = [pl.BlockSpec((CONV_HALO, CONV_CW),
                               lambda t, c=c: (jnp.maximum(t * halo_per_blk - 1, 0), col0 + c)) for c in range(nc)]
    next_specs = [pl.BlockSpec((CONV_HALO, CONV_CW),
                               lambda t, c=c: (jnp.minimum((t + 1) * halo_per_blk, last_halo), col0 + c))
                  for c in range(nc)]
    return pl.pallas_call(
        _conv_kernel,
        grid=(n_rows // CONV_BLK,),
        in_specs=x_specs + prev_specs + next_specs + [
            pl.BlockSpec((None, SUBLANE, CONV_CH), lambda t: (j_even, 0, 0)),
            pl.BlockSpec((None, 1, CONV_CH), lambda t: (j_even, 0, 0)),
        ],
        out_specs=pl.BlockSpec((CONV_BLK, CONV_CH), lambda t: (t, 0)),
        out_shape=jax.ShapeDtypeStruct((n_rows, CONV_CH), F32),
        scratch_shapes=[pltpu.VMEM((CONV_BLK + 2 * CONV_HALO, CONV_CW), F32)],
        compiler_params=_cparams(("arbitrary",)),
        name="conv_silu",
    )(*([proj] * (3 * nc)), conv_w8, conv_b2)


def _softplus(x):
    return jnp.maximum(x, 0.0) + jnp.log1p(jnp.exp(-jnp.abs(x)))


def _dot_select(a, sel):
    sel = sel.astype(BF16)
    hi = a.astype(BF16)
    r1 = a - hi.astype(F32)
    mid = r1.astype(BF16)
    lo = (r1 - mid.astype(F32)).astype(BF16)
    return (jnp.dot(hi, sel, preferred_element_type=F32) + jnp.dot(mid, sel, preferred_element_type=F32)
            + jnp.dot(lo, sel, preferred_element_type=F32))


def _select_dot(sel, a):
    sel = sel.astype(BF16)
    hi = a.astype(BF16)
    r1 = a - hi.astype(F32)
    mid = r1.astype(BF16)
    lo = (r1 - mid.astype(F32)).astype(BF16)
    return (jnp.dot(sel, hi, preferred_element_type=F32) + jnp.dot(sel, mid, preferred_element_type=F32)
            + jnp.dot(sel, lo, preferred_element_type=F32))


def _ssd_direction(xs_ref, b_ref, c_ref, dt_ref, dtb_ref, alog_ref, y_ref, h_scr, lane0, forward):
    q = SSD_CHUNK
    gw = SSD_INNER // SSD_GROUPS
    dt = _softplus(dt_ref[...] + dtb_ref[...])
    a = dt * (-jnp.exp(alog_ref[...]))
    ri = lax.broadcasted_iota(jnp.int32, (q, q), 0)
    ci = lax.broadcasted_iota(jnp.int32, (q, q), 1)
    tri = (ci <= ri) if forward else (ci >= ri)
    a_cum = _select_dot(tri.astype(F32), a)
    a_cum_t = a_cum.T
    er = lax.broadcasted_iota(jnp.int32, (LANE, SSD_INNER), 0)
    ec = lax.broadcasted_iota(jnp.int32, (LANE, SSD_INNER), 1)
    expand = ((er - lane0) == (ec // SSD_HEAD_DIM)).astype(F32)
    dt_e = _dot_select(dt, expand)
    ac_e = _dot_select(a_cum, expand)
    end = q - 1 if forward else 0
    a_end_e = ac_e[end:end + 1, :]
    xdt = xs_ref[...] * dt_e
    in_decay = jnp.exp(ac_e)
    out_decay = jnp.exp(a_end_e - ac_e)
    state_decay = jnp.exp(a_end_e)
    lane = lax.broadcasted_iota(jnp.int32, (q, LANE), 1)
    for g in range(SSD_GROUPS):
        gs = slice(g * gw, (g + 1) * gw)
        bg_t = b_ref[:, g * SSD_STATE:(g + 1) * SSD_STATE].T.astype(BF16)
        cg = c_ref[:, g * SSD_STATE:(g + 1) * SSD_STATE].astype(BF16)
        cb = jnp.dot(cg, bg_t, preferred_element_type=F32)
        h_t = h_scr[:, gs]
        y_inter = jnp.dot(cg, h_t.astype(BF16), preferred_element_type=F32) * in_decay[:, gs]
        for pair in range(gw // LANE):
            cs = slice(g * gw + pair * LANE, g * gw + (pair + 1) * LANE)
            x_pair = xdt[:, cs].astype(BF16)
            res = []
            for sub in range(LANE // SSD_HEAD_DIM):
                hl = lane0 + (g * gw + pair * LANE) // SSD_HEAD_DIM + sub
                seg = a_cum[:, hl:hl + 1] - a_cum_t[hl:hl + 1, :]
                decay = jnp.exp(jnp.where(tri, seg, -jnp.inf))
                res.append(jnp.dot((cb * decay).astype(BF16), x_pair, preferred_element_type=F32))
            y_pair = jnp.where(lane < SSD_HEAD_DIM, res[0], res[1])
            y_ref[:, cs] = y_pair + y_inter[:, pair * LANE:(pair + 1) * LANE]
        x_out = (xdt[:, gs] * out_decay[:, gs]).astype(BF16)
        h_scr[:, gs] = state_decay[:, gs] * h_t + jnp.dot(bg_t, x_out, preferred_element_type=F32)


def _ssd_kernel(xs_f, b_f, c_f, dt_f, xs_b, b_b, c_b, dt_b, dtb_ref, alog_ref, yf_ref, yb_ref,
                hf_scr, hb_scr):
    @pl.when(pl.program_id(1) == 0)
    def _():
        hf_scr[...] = jnp.zeros_like(hf_scr)
        hb_scr[...] = jnp.zeros_like(hb_scr)

    _ssd_direction(xs_f, b_f, c_f, dt_f, dtb_ref, alog_ref, yf_ref, hf_scr, 0, True)
    _ssd_direction(xs_b, b_b, c_b, dt_b, dtb_ref, alog_ref, yb_ref, hb_scr, SSD_HEADS, False)


def _ssd_scan(act, proj, dtb, alog, j_even):
    q = SSD_CHUNK
    n_rows = act.shape[0]
    n_ctx = CTX_LEN // q
    n_lat = SEQ // q
    ctx_blk0 = N_LAT // q

    def fwd_blk(b, s):
        return jnp.where(s < n_ctx, ctx_blk0 + b * n_ctx + s, b * n_lat + (s - n_ctx))

    def bwd_blk(b, s):
        return jnp.where(s < n_ctx, ctx_blk0 + b * n_ctx + (n_ctx - 1 - s),
                         b * n_lat + (n_lat - 1 - (s - n_ctx)))

    gn = SSD_GROUPS * SSD_STATE
    b_col = SSD_INNER // gn
    c_col = b_col + 1

    def specs(blk):
        return [
            pl.BlockSpec((q, SSD_INNER), lambda b, s: (blk(b, s), 0)),
            pl.BlockSpec((q, gn), lambda b, s: (blk(b, s), b_col)),
            pl.BlockSpec((q, gn), lambda b, s: (blk(b, s), c_col)),
            pl.BlockSpec((q, LANE), lambda b, s: (blk(b, s), COL_DT)),
        ]

    small = pl.BlockSpec((None, 1, LANE), lambda b, s: (j_even, 0, 0))
    return pl.pallas_call(
        _ssd_kernel,
        grid=(BATCH, n_ctx + n_lat),
        in_specs=specs(fwd_blk) + specs(bwd_blk) + [small, small],
        out_specs=[pl.BlockSpec((q, SSD_INNER), lambda b, s: (fwd_blk(b, s), 0)),
                   pl.BlockSpec((q, SSD_INNER), lambda b, s: (bwd_blk(b, s), 0))],
        out_shape=[jax.ShapeDtypeStruct((n_rows, SSD_INNER), F32)] * 2,
        scratch_shapes=[pltpu.VMEM((SSD_STATE, SSD_INNER), F32)] * 2,
        compiler_params=_cparams(("arbitrary", "arbitrary")),
        name="ssd_scan",
    )(act, act, act, proj, act, act, act, proj, dtb, alog)


def _out_proj_kernel(attn_ref, yf_ref, yb_ref, xs_ref, z0_ref, z1_ref, dsk_ref, ng_ref, w_ref, x_ref, mod_ref,
                     o_ref):
    mix = jnp.dot(attn_ref[...], w_ref[0:NA_WIDTH, :], preferred_element_type=F32)
    gw = SSD_INNER // SSD_GROUPS
    for g, z_ref in enumerate((z0_ref, z1_ref)):
        gs = slice(g * gw, (g + 1) * gw)
        y = yf_ref[:, gs] + yb_ref[:, gs] + dsk_ref[:, gs] * xs_ref[:, gs]
        seg = y * _silu(z_ref[...])
        ms = jnp.mean(seg * seg, axis=-1, keepdims=True)
        yn = (seg * lax.rsqrt(ms + EPS) * ng_ref[:, g * gw:(g + 1) * gw]).astype(BF16)
        mix = mix + jnp.dot(yn, w_ref[NA_WIDTH + g * gw:NA_WIDTH + (g + 1) * gw, :],
                            preferred_element_type=F32)
    o_ref[...] = x_ref[...] + mod_ref[2] * mix


def _out_proj(attn, y_f, y_b, act, proj, dsk, ng, w_out, x, n_rows, mods_i, j_even):
    tm = 256
    mrow = _mod_row(tm)
    gw = SSD_INNER // SSD_GROUPS
    z_col = COL_Z * LANE // gw
    assert SSD_GROUPS == 2
    return pl.pallas_call(
        _out_proj_kernel,
        grid=(n_rows // tm,),
        in_specs=[
            pl.BlockSpec((tm, NA_WIDTH), lambda t: (t, 0)),
            pl.BlockSpec((tm, SSD_INNER), lambda t: (t, 0)),
            pl.BlockSpec((tm, SSD_INNER), lambda t: (t, 0)),
            pl.BlockSpec((tm, SSD_INNER), lambda t: (t, 0)),
            pl.BlockSpec((tm, gw), lambda t: (t, z_col)),
            pl.BlockSpec((tm, gw), lambda t: (t, z_col + 1)),
            pl.BlockSpec((None, 1, SSD_INNER), lambda t: (j_even, 0, 0)),
            pl.BlockSpec((None, 1, SSD_INNER), lambda t: (j_even, 0, 0)),
            pl.BlockSpec((None, D_MODEL, D_MODEL), lambda t: (j_even, 0, 0)),
            pl.BlockSpec((tm, D_MODEL), lambda t: (t, 0)),
            pl.BlockSpec((None, 3, 1, D_MODEL), lambda t: (mrow(t), 1, 0, 0)),
        ],
        out_specs=pl.BlockSpec((tm, D_MODEL), lambda t: (t, 0)),
        out_shape=jax.ShapeDtypeStruct((n_rows, D_MODEL), F32),
        compiler_params=_cparams(("arbitrary",)),
        name="out_proj",
    )(attn, y_f, y_b, act, proj, proj, dsk, ng, w_out, x, mods_i)


def _fourier_chan_kernel(x_ref, mod_ref, g_ref, cs_ref, p_ref, q_ref):
    h = _mod_norm(x_ref[...], g_ref[...], mod_ref[0], mod_ref[1])
    for g in range(F_GROUPS):
        gs = slice(g * F_GROUP_CH, (g + 1) * F_GROUP_CH)
        pq = jnp.dot(h[:, gs].astype(BF16), cs_ref[...], preferred_element_type=F32)
        p_ref[:, gs] = pq[:, :F_GROUP_CH].astype(BF16)
        q_ref[:, gs] = pq[:, F_GROUP_CH:].astype(BF16)


def _fourier_chan(x, n_rows, mods_i, g, cs):
    tm = 512
    mrow = _mod_row(tm)
    return pl.pallas_call(
        _fourier_chan_kernel,
        grid=(n_rows // tm,),
        in_specs=[
            pl.BlockSpec((tm, D_MODEL), lambda t: (t, 0)),
            pl.BlockSpec((None, 3, 1, D_MODEL), lambda t: (mrow(t), 1, 0, 0)),
            pl.BlockSpec((1, D_MODEL), lambda t: (0, 0)),
            pl.BlockSpec((F_GROUP_CH, 2 * F_GROUP_CH), lambda t: (0, 0)),
        ],
        out_specs=[pl.BlockSpec((tm, D_MODEL), lambda t: (t, 0))] * 2,
        out_shape=[jax.ShapeDtypeStruct((n_rows, D_MODEL), BF16)] * 2,
        compiler_params=_cparams(("arbitrary",)),
        name="fourier_chan",
    )(x, mods_i, g, cs)


DFT_SPLIT = 64


def _fourier_pos_kernel(u_ref, v_ref, p_ref, q_ref, w_ref, x_ref, mod_ref, o_ref, acc_scr, c_scr, s_scr, *,
                        scale):
    k = pl.program_id(2)

    @pl.when(k == 0)
    def _():
        acc_scr[...] = jnp.zeros_like(acc_scr)

    vr, vi = v_ref[0], v_ref[1]
    for a in range(u_ref.shape[1]):
        ur, ui = u_ref[0, a:a + 1, :], u_ref[1, a:a + 1, :]
        rows = slice(a * DFT_SPLIT, (a + 1) * DFT_SPLIT)
        c_scr[rows, :] = (ur * vr - ui * vi).astype(BF16)
        s_scr[rows, :] = (ui * vr + ur * vi).astype(BF16)
    acc_scr[...] += (jnp.dot(c_scr[...], p_ref[...], preferred_element_type=F32)
                     - jnp.dot(s_scr[...], q_ref[...], preferred_element_type=F32))

    @pl.when(k == pl.num_programs(2) - 1)
    def _():
        f = (acc_scr[...] * scale).astype(BF16)
        o_ref[...] = x_ref[...] + mod_ref[2] * jnp.dot(f, w_ref[...], preferred_element_type=F32)


def _fourier_pos(tabs, p, q, w, x, mods_i, j_odd, seq_len, row0, is_ctx):
    u_tab, v_tab = tabs
    tm = tk = min(seq_len, 512)
    nm = seq_len // tm
    blk0 = row0 // tm
    scale = float((seq_len * F_GROUP_CH) ** -0.5)
    return pl.pallas_call(
        functools.partial(_fourier_pos_kernel, scale=scale),
        grid=(BATCH, nm, seq_len // tk),
        in_specs=[
            pl.BlockSpec((2, tm // DFT_SPLIT, tk), lambda b, m, k: (0, m, k)),
            pl.BlockSpec((2, DFT_SPLIT, tk), lambda b, m, k: (0, 0, k)),
            pl.BlockSpec((tk, D_MODEL), lambda b, m, k: (blk0 + b * nm + k, 0)),
            pl.BlockSpec((tk, D_MODEL), lambda b, m, k: (blk0 + b * nm + k, 0)),
            pl.BlockSpec((None, D_MODEL, D_MODEL), lambda b, m, k: (j_odd, 0, 0)),
            pl.BlockSpec((tm, D_MODEL), lambda b, m, k: (blk0 + b * nm + m, 0)),
            pl.BlockSpec((None, 3, 1, D_MODEL), lambda b, m, k: (BATCH if is_ctx else b, 1, 0, 0)),
        ],
        out_specs=pl.BlockSpec((tm, D_MODEL), lambda b, m, k: (b * nm + m, 0)),
        out_shape=jax.ShapeDtypeStruct((BATCH * seq_len, D_MODEL), F32),
        scratch_shapes=[pltpu.VMEM((tm, D_MODEL), F32), pltpu.VMEM((tm, tk), BF16), pltpu.VMEM((tm, tk), BF16)],
        compiler_params=_cparams(("arbitrary", "arbitrary", "arbitrary")),
        name="fourier_pos",
    )(u_tab, v_tab, p, q, w, x, mods_i)


def _phase_table(freq, n):
    l = jnp.arange(n, dtype=jnp.int32)
    ang = ((freq[:, None] * l[None, :]) % n).astype(F32) * np.float32(2.0 * np.pi / n)
    return jnp.stack([jnp.cos(ang), jnp.sin(ang)])


def _dft_tables(n):
    a = jnp.arange(n // DFT_SPLIT, dtype=jnp.int32) * DFT_SPLIT
    b = jnp.arange(DFT_SPLIT, dtype=jnp.int32)
    return _phase_table(a, n), _phase_table(b, n)


def kernel(x, c, ctx, c_ctx, mod_w, mod_b, norm_g, ffn_w1, ffn_w3, ffn_w2, mix_w_in, mix_w_out, qk_g, rpb,
           conv_w, conv_b, dt_bias, a_log, ssd_d, ssd_norm_g, fourier_w):
    n_even = mix_w_in.shape[0]
    xs = jnp.concatenate([x.reshape(N_LAT, D_MODEL), ctx.reshape(BATCH * CTX_LEN, D_MODEL)], axis=0)
    mods = _modulation(c, c_ctx, mod_w, mod_b)

    nj = D_FF // FFN_TF
    w13 = jnp.concatenate([ffn_w1.reshape(DEPTH, 2, D_MODEL, nj, FFN_TF),
                           ffn_w3.reshape(DEPTH, 2, D_MODEL, nj, FFN_TF)], axis=-1)
    w13 = w13.astype(BF16).reshape(DEPTH, 2, D_MODEL, 2 * D_FF)
    w2 = ffn_w2.astype(BF16)
    w_in = jnp.pad(mix_w_in, ((0, 0), (0, 0), (0, IN_W_PAD - IN_W))).astype(BF16)
    w_out = mix_w_out.astype(BF16)
    w_f = fourier_w.astype(BF16)
    qkg = qk_g.reshape(n_even, 2, 1, NA_HEAD_DIM)
    conv_w8 = jnp.pad(conv_w, ((0, 0), (0, SUBLANE - CONV_K), (0, 0)))
    conv_b2 = conv_b.reshape(n_even, 1, CONV_CH)
    pad_l = LANE - 2 * SSD_HEADS
    dtb = jnp.pad(dt_bias.reshape(n_even, 1, 2 * SSD_HEADS), ((0, 0), (0, 0), (0, pad_l)))
    alog = jnp.pad(a_log.reshape(n_even, 1, 2 * SSD_HEADS), ((0, 0), (0, 0), (0, pad_l)))
    dsk = jnp.repeat(ssd_d, SSD_HEAD_DIM, axis=-1).reshape(n_even, 1, SSD_INNER)
    ng = ssd_norm_g.reshape(n_even, 1, SSD_INNER)
    cos_t, sin_t = _rope_tables()
    chan = _phase_table(jnp.arange(F_GROUP_CH, dtype=jnp.int32), F_GROUP_CH)
    cs_chan = jnp.concatenate([chan[0], chan[1]], axis=1).astype(BF16)
    dft_lat = _dft_tables(SEQ)
    dft_ctx = _dft_tables(CTX_LEN)

    last_ctx = ((DEPTH - 1) // 2) * 2
    for i in range(DEPTH):
        use_ctx = i <= last_ctx
        ctx_out = i < last_ctx
        j = i // 2
        mods_i = mods[i]
        g = norm_g[i].reshape(3, 1, D_MODEL)
        rows_in = N_TOK if use_ctx else N_LAT
        rows_out = N_TOK if ctx_out else N_LAT
        xs = _ffn(xs, rows_in, mods_i, 0, g[0], w13, w2, i, 0)
        if i % 2 == 0:
            assert use_ctx, "even mixer layers read the context keys and SSD states"
            proj = _in_proj(xs, rows_in, mods_i, g[1], w_in, j)
            bias = _attention_bias(rpb[j])
            attn, attn_ctx = _na_attention(proj, cos_t, sin_t, qkg[j], bias)
            if ctx_out:
                attn = jnp.concatenate([attn, attn_ctx], axis=0)
            act = _conv_silu(proj, conv_w8, conv_b2, j)
            y_f, y_b = _ssd_scan(act, proj, dtb, alog, j)
            xs = _out_proj(attn, y_f, y_b, act, proj, dsk, ng, w_out, xs, rows_out, mods_i, j)
        else:
            p, q = _fourier_chan(xs, rows_out, mods_i, g[1], cs_chan)
            new = _fourier_pos(dft_lat, p, q, w_f, xs, mods_i, j, SEQ, 0, False)
            if ctx_out:
                new_ctx = _fourier_pos(dft_ctx, p, q, w_f, xs, mods_i, j, CTX_LEN, N_LAT, True)
                new = jnp.concatenate([new, new_ctx], axis=0)
            xs = new
        xs = _ffn(xs, rows_out, mods_i, 2, g[2], w13, w2, i, 1)
    return xs[:N_LAT].reshape(BATCH, SEQ, D_MODEL)
```

```python
import functools

import numpy as np
import jax
import jax.numpy as jnp
from jax import lax
from jax.experimental import pallas as pl
from jax.experimental.pallas import tpu as pltpu

F32 = jnp.float32
BF16 = jnp.bfloat16

D_MODEL = 2048
BATCH = 2
SEQ = 4096
DEPTH = 4
GRID_W = 64
GRID_ROWS = SEQ // GRID_W
CTX_LEN = 256
N_LAT = BATCH * SEQ
N_TOK = N_LAT + BATCH * CTX_LEN
NA_HEADS = 8
NA_HEAD_DIM = 128
NA_WIDTH = NA_HEADS * NA_HEAD_DIM
WIN_R = 8
WIN_C = 16
ROPE_BASE = 10000.0
SSD_HEADS = 16
SSD_HEAD_DIM = 64
SSD_INNER = SSD_HEADS * SSD_HEAD_DIM
SSD_GROUPS = 2
SSD_STATE = 128
SSD_CHUNK = 128
CONV_K = 5
CONV_CH = SSD_INNER + 2 * SSD_GROUPS * SSD_STATE
IN_W = 3 * NA_WIDTH + SSD_INNER + CONV_CH + 2 * SSD_HEADS
LANE = 128
IN_W_PAD = ((IN_W + LANE - 1) // LANE) * LANE
F_GROUPS = 4
F_GROUP_CH = D_MODEL // F_GROUPS
D_FF = 5632
N_MOD = 9
EPS = 1e-6
ATTN_SCALE = NA_HEAD_DIM ** -0.5

COL_Q = 0
COL_K = NA_WIDTH // LANE
COL_V = 2 * NA_WIDTH // LANE
COL_Z = 3 * NA_WIDTH // LANE
COL_XBC = (3 * NA_WIDTH + SSD_INNER) // LANE
COL_DT = (3 * NA_WIDTH + SSD_INNER + CONV_CH) // LANE

Q_ROWS = 8
Q_BLK = Q_ROWS * GRID_W
Q_SPLIT = 2
PART_ROWS = Q_ROWS // Q_SPLIT
PART_Q = PART_ROWS * GRID_W
K_ROWS = 12
K_BLK = K_ROWS * GRID_W
N_QBLK = GRID_ROWS // Q_ROWS

VMEM_LIMIT = 56 * 1024 * 1024


def _cparams(sem, vmem_limit=VMEM_LIMIT):
    return pltpu.CompilerParams(dimension_semantics=sem, vmem_limit_bytes=vmem_limit)


def _sigmoid(x):
    return 1.0 / (1.0 + jnp.exp(-x))


def _silu(x):
    return x * _sigmoid(x)


def _mod_norm(x, g, shift, scale):
    ms = jnp.mean(x * x, axis=-1, keepdims=True)
    y = x * lax.rsqrt(ms + EPS) * g
    return y * (1.0 + scale) + shift


def _mod_row(tile_rows):
    def f(t):
        return jnp.minimum((t * tile_rows) // SEQ, BATCH)
    return f


def _mod_kernel(c_ref, w_ref, b_ref, o_ref):
    s = _silu(c_ref[...]).astype(BF16)
    o_ref[...] = jnp.dot(s, w_ref[...].astype(BF16), preferred_element_type=F32) + b_ref[...]


def _modulation(c, c_ctx, mod_w, mod_b):
    rows = 8
    cvec = jnp.concatenate([c, c_ctx[None, :], jnp.zeros((rows - BATCH - 1, D_MODEL), F32)], axis=0)
    n = N_MOD * D_MODEL
    tn = 1024
    out = pl.pallas_call(
        _mod_kernel,
        grid=(DEPTH, n // tn),
        in_specs=[
            pl.BlockSpec((rows, D_MODEL), lambda i, j: (0, 0)),
            pl.BlockSpec((None, D_MODEL, tn), lambda i, j: (i, 0, j)),
            pl.BlockSpec((None, 1, tn), lambda i, j: (i, 0, j)),
        ],
        out_specs=pl.BlockSpec((None, rows, tn), lambda i, j: (i, 0, j)),
        out_shape=jax.ShapeDtypeStruct((DEPTH, rows, n), F32),
        compiler_params=_cparams(("arbitrary", "arbitrary")),
        name="modulation",
    )(cvec, mod_w, mod_b.reshape(DEPTH, 1, n))
    return out[:, :BATCH + 1].reshape(DEPTH, BATCH + 1, N_MOD, 1, D_MODEL)


FFN_TM = 1024
FFN_TF = 512
FFN_VMEM_LIMIT = 63 * 1024 * 1024


def _ffn_accumulate(h_scr, w1_ref, w3_ref, w2_ref, o_ref, rows):
    h = h_scr[0:rows, :]
    a = jnp.dot(h, w1_ref[...], preferred_element_type=F32)
    b = jnp.dot(h, w3_ref[...], preferred_element_type=F32)
    u = (_silu(a) * b).astype(BF16)
    o_ref[0:rows, :] += jnp.dot(u, w2_ref[...], preferred_element_type=F32)


def _ffn_kernel(x_ref, mod_ref, g_ref, w1_ref, w3_ref, w2_ref, o_ref, h_scr, *, tail_rows):
    t = pl.program_id(0)
    j = pl.program_id(1)
    last_t = pl.num_programs(0) - 1

    @pl.when(j == 0)
    def _():
        h_scr[...] = _mod_norm(x_ref[...], g_ref[...], mod_ref[0], mod_ref[1]).astype(BF16)
        o_ref[...] = jnp.zeros_like(o_ref)

    if tail_rows == FFN_TM:
        _ffn_accumulate(h_scr, w1_ref, w3_ref, w2_ref, o_ref, FFN_TM)
    else:
        @pl.when(t < last_t)
        def _():
            _ffn_accumulate(h_scr, w1_ref, w3_ref, w2_ref, o_ref, FFN_TM)

        @pl.when(t == last_t)
        def _():
            _ffn_accumulate(h_scr, w1_ref, w3_ref, w2_ref, o_ref, tail_rows)

    @pl.when(j == pl.num_programs(1) - 1)
    def _():
        o_ref[...] = x_ref[...] + 0.5 * mod_ref[2] * o_ref[...]


def _ffn(x, n_rows, mods_i, sub, g, w1, w3, w2, layer, which):
    tm, tf = FFN_TM, FFN_TF
    mrow = _mod_row(tm)
    nt = pl.cdiv(n_rows, tm)
    tail_rows = n_rows - (nt - 1) * tm
    return pl.pallas_call(
        functools.partial(_ffn_kernel, tail_rows=tail_rows),
        grid=(nt, D_FF // tf),
        in_specs=[
            pl.BlockSpec((tm, D_MODEL), lambda t, j: (t, 0)),
            pl.BlockSpec((None, 3, 1, D_MODEL), lambda t, j: (mrow(t), sub, 0, 0)),
            pl.BlockSpec((1, D_MODEL), lambda t, j: (0, 0)),
            pl.BlockSpec((None, None, D_MODEL, tf), lambda t, j: (layer, which, 0, j)),
            pl.BlockSpec((None, None, D_MODEL, tf), lambda t, j: (layer, which, 0, j)),
            pl.BlockSpec((None, None, tf, D_MODEL), lambda t, j: (layer, which, j, 0)),
        ],
        out_specs=pl.BlockSpec((tm, D_MODEL), lambda t, j: (t, 0)),
        out_shape=jax.ShapeDtypeStruct((n_rows, D_MODEL), F32),
        scratch_shapes=[pltpu.VMEM((tm, D_MODEL), BF16)],
        compiler_params=_cparams(("arbitrary", "arbitrary"), FFN_VMEM_LIMIT),
        name="ffn",
    )(x, mods_i, g, w1, w3, w2)


def _norm_matmul_kernel(x_ref, mod_ref, g_ref, w_ref, o_ref, h_scr):
    @pl.when(pl.program_id(1) == 0)
    def _():
        h_scr[...] = _mod_norm(x_ref[...], g_ref[...], mod_ref[0], mod_ref[1]).astype(BF16)

    o_ref[...] = jnp.dot(h_scr[...], w_ref[...], preferred_element_type=F32)


def _in_proj(x, n_rows, mods_i, g, w_in, j_even):
    tm, tn = 512, 1920
    mrow = _mod_row(tm)
    return pl.pallas_call(
        _norm_matmul_kernel,
        grid=(n_rows // tm, IN_W_PAD // tn),
        in_specs=[
            pl.BlockSpec((tm, D_MODEL), lambda t, j: (t, 0)),
            pl.BlockSpec((None, 3, 1, D_MODEL), lambda t, j: (mrow(t), 1, 0, 0)),
            pl.BlockSpec((1, D_MODEL), lambda t, j: (0, 0)),
            pl.BlockSpec((None, D_MODEL, tn), lambda t, j: (j_even, 0, j)),
        ],
        out_specs=pl.BlockSpec((tm, tn), lambda t, j: (t, j)),
        out_shape=jax.ShapeDtypeStruct((n_rows, IN_W_PAD), F32),
        scratch_shapes=[pltpu.VMEM((tm, D_MODEL), BF16)],
        compiler_params=_cparams(("arbitrary", "arbitrary")),
        name="in_proj",
    )(x, mods_i, g, w_in)


def _head_rms(x, g):
    ms = jnp.mean(x * x, axis=-1, keepdims=True)
    return x * lax.rsqrt(ms + EPS) * g


def _rope(x, cos, sin):
    lane = lax.broadcasted_iota(jnp.int32, x.shape, 1)
    first_half = (lane & (NA_HEAD_DIM // 4)) == 0
    partner = jnp.where(first_half, pltpu.roll(x, NA_HEAD_DIM - NA_HEAD_DIM // 4, 1),
                        pltpu.roll(x, NA_HEAD_DIM // 4, 1))
    return x * cos + partner * sin


def _dot_nt(a, b):
    return lax.dot_general(a, b, (((1,), (1,)), ((), ())), preferred_element_type=F32)


def _softmax_pv(scores, values):
    m = functools.reduce(jnp.maximum, [jnp.max(s, axis=-1, keepdims=True) for s in scores])
    ps = [jnp.exp(s - m) for s in scores]
    denom = functools.reduce(jnp.add, [jnp.sum(p, axis=-1, keepdims=True) for p in ps])
    o = functools.reduce(jnp.add, [jnp.dot(p.astype(BF16), v, preferred_element_type=F32)
                                   for p, v in zip(ps, values)])
    return o / denom


def _part_key_row(j, part):
    return jnp.clip(Q_ROWS * j + PART_ROWS * part - WIN_R // 2, 0, GRID_ROWS - K_ROWS)


def _na_kernel(*refs):
    (q_ref, k_ref, v_ref, qc_ref, kc_ref, vc_ref, cos_ref, sin_ref, qkg_ref), refs = refs[:9], refs[9:]
    bias_refs, (o_ref, oc_ref, k_scr, v_scr, kc_scr, vc_scr) = refs[:Q_SPLIT], refs[Q_SPLIT:]
    j = pl.program_id(2)

    @pl.when(j == 0)
    def _():
        kn = _head_rms(k_ref[...], qkg_ref[1])
        k_scr[...] = _rope(kn, cos_ref[...], sin_ref[...]).astype(BF16)
        v_scr[...] = v_ref[...].astype(BF16)
        kcn = _head_rms(kc_ref[...], qkg_ref[1]).astype(BF16)
        vcb = vc_ref[...].astype(BF16)
        kc_scr[...] = kcn
        vc_scr[...] = vcb
        qcn = _head_rms(qc_ref[...], qkg_ref[0]).astype(BF16)
        oc_ref[...] = _softmax_pv([_dot_nt(qcn, kcn) * ATTN_SCALE], [vcb]).astype(oc_ref.dtype)

    for part in range(Q_SPLIT):
        rows = pl.ds(part * PART_Q, PART_Q)
        q0 = pl.multiple_of(j * Q_BLK + part * PART_Q, PART_Q)
        k0 = pl.multiple_of(_part_key_row(j, part) * GRID_W, GRID_W)
        kw = k_scr[pl.ds(k0, K_BLK), :]
        vw = v_scr[pl.ds(k0, K_BLK), :]
        qn = _head_rms(q_ref[rows, :], qkg_ref[0])
        qr = _rope(qn, cos_ref[pl.ds(q0, PART_Q), :], sin_ref[pl.ds(q0, PART_Q), :]).astype(BF16)
        s_win = _dot_nt(qr, kw) * ATTN_SCALE + bias_refs[part][...]
        s_ctx = _dot_nt(qn.astype(BF16), kc_scr[...]) * ATTN_SCALE
        o_ref[rows, :] = _softmax_pv([s_win, s_ctx], [vw, vc_scr[...]]).astype(o_ref.dtype)


BIAS_PARTS = ((0, 0), (N_QBLK // 2, 0), (N_QBLK - 1, Q_SPLIT - 1))


def _bias_pattern(j, part):
    first = (j == 0) & (part == 0)
    last = (j == N_QBLK - 1) & (part == Q_SPLIT - 1)
    return jnp.where(first, 0, jnp.where(last, 2, 1))


def _na_attention(proj, cos_t, sin_t, qkg, bias):
    lat_blk = SEQ // Q_BLK
    ctx_blk0 = N_LAT // CTX_LEN
    hd = NA_HEAD_DIM
    bias_specs = [pl.BlockSpec((None, None, PART_Q, K_BLK), lambda b, h, j, p=p: (h, _bias_pattern(j, p), 0, 0))
                  for p in range(Q_SPLIT)]
    return pl.pallas_call(
        _na_kernel,
        grid=(BATCH, NA_HEADS, N_QBLK),
        in_specs=[
            pl.BlockSpec((Q_BLK, hd), lambda b, h, j: (b * lat_blk + j, COL_Q + h)),
            pl.BlockSpec((SEQ, hd), lambda b, h, j: (b, COL_K + h)),
            pl.BlockSpec((SEQ, hd), lambda b, h, j: (b, COL_V + h)),
            pl.BlockSpec((CTX_LEN, hd), lambda b, h, j: (ctx_blk0 + b, COL_Q + h)),
            pl.BlockSpec((CTX_LEN, hd), lambda b, h, j: (ctx_blk0 + b, COL_K + h)),
            pl.BlockSpec((CTX_LEN, hd), lambda b, h, j: (ctx_blk0 + b, COL_V + h)),
            pl.BlockSpec((SEQ, hd), lambda b, h, j: (0, 0)),
            pl.BlockSpec((SEQ, hd), lambda b, h, j: (0, 0)),
            pl.BlockSpec((2, 1, hd), lambda b, h, j: (0, 0, 0)),
        ] + bias_specs,
        out_specs=[pl.BlockSpec((Q_BLK, hd), lambda b, h, j: (b * lat_blk + j, h)),
                   pl.BlockSpec((CTX_LEN, hd), lambda b, h, j: (b, h))],
        out_shape=[jax.ShapeDtypeStruct((N_LAT, NA_WIDTH), BF16),
                   jax.ShapeDtypeStruct((BATCH * CTX_LEN, NA_WIDTH), BF16)],
        scratch_shapes=[pltpu.VMEM((SEQ, hd), BF16), pltpu.VMEM((SEQ, hd), BF16),
                        pltpu.VMEM((CTX_LEN, hd), BF16), pltpu.VMEM((CTX_LEN, hd), BF16)],
        compiler_params=_cparams(("arbitrary", "arbitrary", "arbitrary")),
        name="na_attention",
    )(proj, proj, proj, proj, proj, proj, cos_t, sin_t, qkg, *([bias] * Q_SPLIT))


def _part_geometry(jb, part):
    rows = Q_ROWS * jb + PART_ROWS * part + np.arange(PART_ROWS)
    k_first = int(np.clip(rows[0] - WIN_R // 2, 0, GRID_ROWS - K_ROWS))
    r0 = np.clip(rows - WIN_R // 2, 0, GRID_ROWS - WIN_R)
    return rows, k_first, r0


def _attention_bias(rpb_j):
    for jb in range(N_QBLK):
        for part in range(Q_SPLIT):
            rows, k_first, r0 = _part_geometry(jb, part)
            pat = 0 if (jb, part) == (0, 0) else 2 if (jb, part) == (N_QBLK - 1, Q_SPLIT - 1) else 1
            rows_p, k_first_p, r0_p = _part_geometry(*BIAS_PARTS[pat])
            assert k_first - rows[0] == k_first_p - rows_p[0] and np.array_equal(r0 - rows, r0_p - rows_p)
    n_dr = 2 * WIN_R - 1
    qcol = np.arange(GRID_W)
    c0 = np.clip(qcol - WIN_C // 2, 0, GRID_W - WIN_C)
    col_ok = (qcol[None, :] >= c0[:, None]) & (qcol[None, :] < c0[:, None] + WIN_C)
    r = rpb_j.astype(F32)
    edge = GRID_W - WIN_C
    ep = jnp.concatenate([jnp.repeat(r[..., :1], edge, axis=-1), r, jnp.repeat(r[..., -1:], edge, axis=-1)],
                         axis=-1)
    t1 = jnp.stack([ep[..., GRID_W - 1 - qc:2 * GRID_W - 1 - qc] for qc in range(GRID_W)], axis=2)
    t1 = jnp.where(jnp.asarray(col_ok)[None, None], t1, -jnp.inf)
    band = t1.transpose(0, 2, 1, 3).reshape(NA_HEADS, GRID_W, n_dr * GRID_W)
    pad_w = K_BLK
    band = jnp.pad(band, ((0, 0), (0, 0), (pad_w, pad_w)), constant_values=-jnp.inf)
    strips, masks = [], []
    for jb, part in BIAS_PARTS:
        rows, k_first, r0 = _part_geometry(jb, part)
        kr = k_first + np.arange(K_ROWS)
        for row, row0 in zip(rows, r0):
            off = pad_w + (k_first - int(row) + WIN_R - 1) * GRID_W
            assert 0 <= off and off + K_BLK <= band.shape[-1]
            strips.append(band[:, :, off:off + K_BLK])
            masks.append(np.repeat((kr >= row0) & (kr < row0 + WIN_R), GRID_W))
    vals = jnp.stack(strips, axis=1)
    mask = np.stack(masks)[None, :, None, :]
    bias = jnp.where(jnp.asarray(mask), vals, -jnp.inf)
    return bias.reshape(NA_HEADS, len(BIAS_PARTS), PART_Q, K_BLK)


def _rope_tables():
    quarter = NA_HEAD_DIM // 4
    inv_freq = ROPE_BASE ** (-jnp.arange(quarter, dtype=F32) / quarter)
    t = jnp.arange(SEQ)
    ang_r = (t // GRID_W).astype(F32)[:, None] * inv_freq[None, :]
    ang_c = (t % GRID_W).astype(F32)[:, None] * inv_freq[None, :]
    cos_t = jnp.concatenate([jnp.cos(ang_r), jnp.cos(ang_r), jnp.cos(ang_c), jnp.cos(ang_c)], axis=-1)
    sin_t = jnp.concatenate([-jnp.sin(ang_r), jnp.sin(ang_r), -jnp.sin(ang_c), jnp.sin(ang_c)], axis=-1)
    return cos_t, sin_t


SUBLANE = 8
CONV_HALO = SUBLANE
CONV_BLK = 256


CONV_CW = 512


def _conv_kernel(*refs):
    nc = CONV_CH // CONV_CW
    x_refs, prev_refs, next_refs = refs[:nc], refs[nc:2 * nc], refs[2 * nc:3 * nc]
    w_ref, b_ref, o_ref, pad_scr = refs[3 * nc:]
    t = pl.program_id(0)
    n_lat_blk = N_LAT // CONV_BLK
    is_ctx = t >= n_lat_blk
    seq_blks = jnp.where(is_ctx, CTX_LEN // CONV_BLK, SEQ // CONV_BLK)
    pos = jnp.where(is_ctx, t - n_lat_blk, t) % seq_blks
    for c in range(nc):
        cs = slice(c * CONV_CW, (c + 1) * CONV_CW)
        pad_scr[0:CONV_HALO, :] = jnp.where(pos == 0, 0.0, prev_refs[c][...])
        pad_scr[CONV_HALO:CONV_HALO + CONV_BLK, :] = x_refs[c][...]
        pad_scr[CONV_HALO + CONV_BLK:2 * CONV_HALO + CONV_BLK, :] = jnp.where(pos == seq_blks - 1, 0.0,
                                                                               next_refs[c][...])
        xp = pad_scr[...]
        acc = jnp.zeros((CONV_BLK, CONV_CW), F32) + b_ref[:, cs]
        for k in range(CONV_K):
            lo = CONV_HALO - CONV_K // 2 + k
            acc = acc + w_ref[k:k + 1, cs] * xp[lo:lo + CONV_BLK, :]
        o_ref[:, cs] = _silu(acc)


def _conv_silu(proj, conv_w8, conv_b2, j_even):
    n_rows = proj.shape[0]
    nc = CONV_CH // CONV_CW
    col0 = COL_XBC * LANE // CONV_CW
    halo_per_blk = CONV_BLK // CONV_HALO
    last_halo = n_rows // CONV_HALO - 1
    x_specs = [pl.BlockSpec((CONV_BLK, CONV_CW), lambda t, c=c: (t, col0 + c)) for c in range(nc)]
    prev_specs = [pl.BlockSpec((CONV_HALO, CONV_CW),
                               lambda t, c=c: (jnp.maximum(t * halo_per_blk - 1, 0), col0 + c)) for c in range(nc)]
    next_specs = [pl.BlockSpec((CONV_HALO, CONV_CW),
                               lambda t, c=c: (jnp.minimum((t + 1) * halo_per_blk, last_halo), col0 + c))
                  for c in range(nc)]
    return pl.pallas_call(
        _conv_kernel,
        grid=(n_rows // CONV_BLK,),
        in_specs=x_specs + prev_specs + next_specs + [
            pl.BlockSpec((None, SUBLANE, CONV_CH), lambda t: (j_even, 0, 0)),
            pl.BlockSpec((None, 1, CONV_CH), lambda t: (j_even, 0, 0)),
        ],
        out_specs=pl.BlockSpec((CONV_BLK, CONV_CH), lambda t: (t, 0)),
        out_shape=jax.ShapeDtypeStruct((n_rows, CONV_CH), F32),
        scratch_shapes=[pltpu.VMEM((CONV_BLK + 2 * CONV_HALO, CONV_CW), F32)],
        compiler_params=_cparams(("arbitrary",)),
        name="conv_silu",
    )(*([proj] * (3 * nc)), conv_w8, conv_b2)


def _softplus(x):
    return jnp.maximum(x, 0.0) + jnp.log1p(jnp.exp(-jnp.abs(x)))


def _dot_select(a, sel):
    sel = sel.astype(BF16)
    hi = a.astype(BF16)
    r1 = a - hi.astype(F32)
    mid = r1.astype(BF16)
    lo = (r1 - mid.astype(F32)).astype(BF16)
    return (jnp.dot(hi, sel, preferred_element_type=F32) + jnp.dot(mid, sel, preferred_element_type=F32)
            + jnp.dot(lo, sel, preferred_element_type=F32))


def _select_dot(sel, a):
    sel = sel.astype(BF16)
    hi = a.astype(BF16)
    r1 = a - hi.astype(F32)
    mid = r1.astype(BF16)
    lo = (r1 - mid.astype(F32)).astype(BF16)
    return (jnp.dot(sel, hi, preferred_element_type=F32) + jnp.dot(sel, mid, preferred_element_type=F32)
            + jnp.dot(sel, lo, preferred_element_type=F32))


def _ssd_direction(xs_ref, b_ref, c_ref, dt_ref, dtb_ref, alog_ref, y_ref, h_scr, lane0, forward):
    q = SSD_CHUNK
    gw = SSD_INNER // SSD_GROUPS
    dt = _softplus(dt_ref[...] + dtb_ref[...])
    a = dt * (-jnp.exp(alog_ref[...]))
    ri = lax.broadcasted_iota(jnp.int32, (q, q), 0)
    ci = lax.broadcasted_iota(jnp.int32, (q, q), 1)
    tri = (ci <= ri) if forward else (ci >= ri)
    a_cum = _select_dot(tri.astype(F32), a)
    a_cum_t = a_cum.T
    er = lax.broadcasted_iota(jnp.int32, (LANE, SSD_INNER), 0)
    ec = lax.broadcasted_iota(jnp.int32, (LANE, SSD_INNER), 1)
    expand = ((er - lane0) == (ec // SSD_HEAD_DIM)).astype(F32)
    dt_e = _dot_select(dt, expand)
    ac_e = _dot_select(a_cum, expand)
    end = q - 1 if forward else 0
    a_end_e = ac_e[end:end + 1, :]
    xdt = xs_ref[...] * dt_e
    in_decay = jnp.exp(ac_e)
    out_decay = jnp.exp(a_end_e - ac_e)
    state_decay = jnp.exp(a_end_e)
    lane = lax.broadcasted_iota(jnp.int32, (q, LANE), 1)
    for g in range(SSD_GROUPS):
        gs = slice(g * gw, (g + 1) * gw)
        bg_t = b_ref[:, g * SSD_STATE:(g + 1) * SSD_STATE].T.astype(BF16)
        cg = c_ref[:, g * SSD_STATE:(g + 1) * SSD_STATE].astype(BF16)
        cb = jnp.dot(cg, bg_t, preferred_element_type=F32)
        h_t = h_scr[:, gs]
        y_inter = jnp.dot(cg, h_t.astype(BF16), preferred_element_type=F32) * in_decay[:, gs]
        for pair in range(gw // LANE):
            cs = slice(g * gw + pair * LANE, g * gw + (pair + 1) * LANE)
            x_pair = xdt[:, cs].astype(BF16)
            res = []
            for sub in range(LANE // SSD_HEAD_DIM):
                hl = lane0 + (g * gw + pair * LANE) // SSD_HEAD_DIM + sub
                seg = a_cum[:, hl:hl + 1] - a_cum_t[hl:hl + 1, :]
                decay = jnp.exp(jnp.where(tri, seg, -jnp.inf))
                res.append(jnp.dot((cb * decay).astype(BF16), x_pair, preferred_element_type=F32))
            y_pair = jnp.where(lane < SSD_HEAD_DIM, res[0], res[1])
            y_ref[:, cs] = y_pair + y_inter[:, pair * LANE:(pair + 1) * LANE]
        x_out = (xdt[:, gs] * out_decay[:, gs]).astype(BF16)
        h_scr[:, gs] = state_decay[:, gs] * h_t + jnp.dot(bg_t, x_out, preferred_element_type=F32)


def _ssd_kernel(xs_f, b_f, c_f, dt_f, xs_b, b_b, c_b, dt_b, dtb_ref, alog_ref, yf_ref, yb_ref,
                hf_scr, hb_scr):
    @pl.when(pl.program_id(1) == 0)
    def _():
        hf_scr[...] = jnp.zeros_like(hf_scr)
        hb_scr[...] = jnp.zeros_like(hb_scr)

    _ssd_direction(xs_f, b_f, c_f, dt_f, dtb_ref, alog_ref, yf_ref, hf_scr, 0, True)
    _ssd_direction(xs_b, b_b, c_b, dt_b, dtb_ref, alog_ref, yb_ref, hb_scr, SSD_HEADS, False)


def _ssd_scan(act, proj, dtb, alog, j_even):
    q = SSD_CHUNK
    n_rows = act.shape[0]
    n_ctx = CTX_LEN // q
    n_lat = SEQ // q
    ctx_blk0 = N_LAT // q

    def fwd_blk(b, s):
        return jnp.where(s < n_ctx, ctx_blk0 + b * n_ctx + s, b * n_lat + (s - n_ctx))

    def bwd_blk(b, s):
        return jnp.where(s < n_ctx, ctx_blk0 + b * n_ctx + (n_ctx - 1 - s),
                         b * n_lat + (n_lat - 1 - (s - n_ctx)))

    gn = SSD_GROUPS * SSD_STATE
    b_col = SSD_INNER // gn
    c_col = b_col + 1

    def specs(blk):
        return [
            pl.BlockSpec((q, SSD_INNER), lambda b, s: (blk(b, s), 0)),
            pl.BlockSpec((q, gn), lambda b, s: (blk(b, s), b_col)),
            pl.BlockSpec((q, gn), lambda b, s: (blk(b, s), c_col)),
            pl.BlockSpec((q, LANE), lambda b, s: (blk(b, s), COL_DT)),
        ]

    small = pl.BlockSpec((None, 1, LANE), lambda b, s: (j_even, 0, 0))
    return pl.pallas_call(
        _ssd_kernel,
        grid=(BATCH, n_ctx + n_lat),
        in_specs=specs(fwd_blk) + specs(bwd_blk) + [small, small],
        out_specs=[pl.BlockSpec((q, SSD_INNER), lambda b, s: (fwd_blk(b, s), 0)),
                   pl.BlockSpec((q, SSD_INNER), lambda b, s: (bwd_blk(b, s), 0))],
        out_shape=[jax.ShapeDtypeStruct((n_rows, SSD_INNER), F32)] * 2,
        scratch_shapes=[pltpu.VMEM((SSD_STATE, SSD_INNER), F32)] * 2,
        compiler_params=_cparams(("arbitrary", "arbitrary")),
        name="ssd_scan",
    )(act, act, act, proj, act, act, act, proj, dtb, alog)


def _out_proj_kernel(attn_ref, yf_ref, yb_ref, xs_ref, z0_ref, z1_ref, dsk_ref, ng_ref, w_ref, x_ref, mod_ref,
                     o_ref):
    mix = jnp.dot(attn_ref[...], w_ref[0:NA_WIDTH, :], preferred_element_type=F32)
    gw = SSD_INNER // SSD_GROUPS
    for g, z_ref in enumerate((z0_ref, z1_ref)):
        gs = slice(g * gw, (g + 1) * gw)
        y = yf_ref[:, gs] + yb_ref[:, gs] + dsk_ref[:, gs] * xs_ref[:, gs]
        seg = y * _silu(z_ref[...])
        ms = jnp.mean(seg * seg, axis=-1, keepdims=True)
        yn = (seg * lax.rsqrt(ms + EPS) * ng_ref[:, g * gw:(g + 1) * gw]).astype(BF16)
        mix = mix + jnp.dot(yn, w_ref[NA_WIDTH + g * gw:NA_WIDTH + (g + 1) * gw, :],
                            preferred_element_type=F32)
    o_ref[...] = x_ref[...] + mod_ref[2] * mix


def _out_proj(attn, y_f, y_b, act, proj, dsk, ng, w_out, x, n_rows, mods_i, j_even):
    tm = 256
    mrow = _mod_row(tm)
    gw = SSD_INNER // SSD_GROUPS
    z_col = COL_Z * LANE // gw
    assert SSD_GROUPS == 2
    return pl.pallas_call(
        _out_proj_kernel,
        grid=(n_rows // tm,),
        in_specs=[
            pl.BlockSpec((tm, NA_WIDTH), lambda t: (t, 0)),
            pl.BlockSpec((tm, SSD_INNER), lambda t: (t, 0)),
            pl.BlockSpec((tm, SSD_INNER), lambda t: (t, 0)),
            pl.BlockSpec((tm, SSD_INNER), lambda t: (t, 0)),
            pl.BlockSpec((tm, gw), lambda t: (t, z_col)),
            pl.BlockSpec((tm, gw), lambda t: (t, z_col + 1)),
            pl.BlockSpec((None, 1, SSD_INNER), lambda t: (j_even, 0, 0)),
            pl.BlockSpec((None, 1, SSD_INNER), lambda t: (j_even, 0, 0)),
            pl.BlockSpec((None, D_MODEL, D_MODEL), lambda t: (j_even, 0, 0)),
            pl.BlockSpec((tm, D_MODEL), lambda t: (t, 0)),
            pl.BlockSpec((None, 3, 1, D_MODEL), lambda t: (mrow(t), 1, 0, 0)),
        ],
        out_specs=pl.BlockSpec((tm, D_MODEL), lambda t: (t, 0)),
        out_shape=jax.ShapeDtypeStruct((n_rows, D_MODEL), F32),
        compiler_params=_cparams(("arbitrary",)),
        name="out_proj",
    )(attn, y_f, y_b, act, proj, proj, dsk, ng, w_out, x, mods_i)


def _fourier_chan_kernel(x_ref, mod_ref, g_ref, cs_ref, p_ref, q_ref):
    h = _mod_norm(x_ref[...], g_ref[...], mod_ref[0], mod_ref[1])
    for g in range(F_GROUPS):
        gs = slice(g * F_GROUP_CH, (g + 1) * F_GROUP_CH)
        pq = jnp.dot(h[:, gs].astype(BF16), cs_ref[...], preferred_element_type=F32)
        p_ref[:, gs] = pq[:, :F_GROUP_CH].astype(BF16)
        q_ref[:, gs] = pq[:, F_GROUP_CH:].astype(BF16)


def _fourier_chan(x, n_rows, mods_i, g, cs):
    tm = 512
    mrow = _mod_row(tm)
    return pl.pallas_call(
        _fourier_chan_kernel,
        grid=(n_rows // tm,),
        in_specs=[
            pl.BlockSpec((tm, D_MODEL), lambda t: (t, 0)),
            pl.BlockSpec((None, 3, 1, D_MODEL), lambda t: (mrow(t), 1, 0, 0)),
            pl.BlockSpec((1, D_MODEL), lambda t: (0, 0)),
            pl.BlockSpec((F_GROUP_CH, 2 * F_GROUP_CH), lambda t: (0, 0)),
        ],
        out_specs=[pl.BlockSpec((tm, D_MODEL), lambda t: (t, 0))] * 2,
        out_shape=[jax.ShapeDtypeStruct((n_rows, D_MODEL), BF16)] * 2,
        compiler_params=_cparams(("arbitrary",)),
        name="fourier_chan",
    )(x, mods_i, g, cs)


DFT_SPLIT = 64


def _fourier_pos_kernel(u_ref, v_ref, p_ref, q_ref, w_ref, x_ref, mod_ref, o_ref, acc_scr, c_scr, s_scr, *,
                        scale):
    k = pl.program_id(2)

    @pl.when(k == 0)
    def _():
        acc_scr[...] = jnp.zeros_like(acc_scr)

    vr, vi = v_ref[0], v_ref[1]
    for a in range(u_ref.shape[1]):
        ur, ui = u_ref[0, a:a + 1, :], u_ref[1, a:a + 1, :]
        rows = slice(a * DFT_SPLIT, (a + 1) * DFT_SPLIT)
        c_scr[rows, :] = (ur * vr - ui * vi).astype(BF16)
        s_scr[rows, :] = (ui * vr + ur * vi).astype(BF16)
    acc_scr[...] += (jnp.dot(c_scr[...], p_ref[...], preferred_element_type=F32)
                     - jnp.dot(s_scr[...], q_ref[...], preferred_element_type=F32))

    @pl.when(k == pl.num_programs(2) - 1)
    def _():
        f = (acc_scr[...] * scale).astype(BF16)
        o_ref[...] = x_ref[...] + mod_ref[2] * jnp.dot(f, w_ref[...], preferred_element_type=F32)


def _fourier_pos(tabs, p, q, w, x, mods_i, j_odd, seq_len, row0, is_ctx):
    u_tab, v_tab = tabs
    tm = tk = min(seq_len, 512)
    nm = seq_len // tm
    blk0 = row0 // tm
    scale = float((seq_len * F_GROUP_CH) ** -0.5)
    return pl.pallas_call(
        functools.partial(_fourier_pos_kernel, scale=scale),
        grid=(BATCH, nm, seq_len // tk),
        in_specs=[
            pl.BlockSpec((2, tm // DFT_SPLIT, tk), lambda b, m, k: (0, m, k)),
            pl.BlockSpec((2, DFT_SPLIT, tk), lambda b, m, k: (0, 0, k)),
            pl.BlockSpec((tk, D_MODEL), lambda b, m, k: (blk0 + b * nm + k, 0)),
            pl.BlockSpec((tk, D_MODEL), lambda b, m, k: (blk0 + b * nm + k, 0)),
            pl.BlockSpec((None, D_MODEL, D_MODEL), lambda b, m, k: (j_odd, 0, 0)),
            pl.BlockSpec((tm, D_MODEL), lambda b, m, k: (blk0 + b * nm + m, 0)),
            pl.BlockSpec((None, 3, 1, D_MODEL), lambda b, m, k: (BATCH if is_ctx else b, 1, 0, 0)),
        ],
        out_specs=pl.BlockSpec((tm, D_MODEL), lambda b, m, k: (b * nm + m, 0)),
        out_shape=jax.ShapeDtypeStruct((BATCH * seq_len, D_MODEL), F32),
        scratch_shapes=[pltpu.VMEM((tm, D_MODEL), F32), pltpu.VMEM((tm, tk), BF16), pltpu.VMEM((tm, tk), BF16)],
        compiler_params=_cparams(("arbitrary", "arbitrary", "arbitrary")),
        name="fourier_pos",
    )(u_tab, v_tab, p, q, w, x, mods_i)


def _phase_table(freq, n):
    l = jnp.arange(n, dtype=jnp.int32)
    ang = ((freq[:, None] * l[None, :]) % n).astype(F32) * np.float32(2.0 * np.pi / n)
    return jnp.stack([jnp.cos(ang), jnp.sin(ang)])


def _dft_tables(n):
    a = jnp.arange(n // DFT_SPLIT, dtype=jnp.int32) * DFT_SPLIT
    b = jnp.arange(DFT_SPLIT, dtype=jnp.int32)
    return _phase_table(a, n), _phase_table(b, n)


def kernel(x, c, ctx, c_ctx, mod_w, mod_b, norm_g, ffn_w1, ffn_w3, ffn_w2, mix_w_in, mix_w_out, qk_g, rpb,
           conv_w, conv_b, dt_bias, a_log, ssd_d, ssd_norm_g, fourier_w):
    n_even = mix_w_in.shape[0]
    xs = jnp.concatenate([x.reshape(N_LAT, D_MODEL), ctx.reshape(BATCH * CTX_LEN, D_MODEL)], axis=0)
    mods = _modulation(c, c_ctx, mod_w, mod_b)

    w1 = ffn_w1.astype(BF16)
    w3 = ffn_w3.astype(BF16)
    w2 = ffn_w2.astype(BF16)
    w_in = jnp.pad(mix_w_in, ((0, 0), (0, 0), (0, IN_W_PAD - IN_W))).astype(BF16)
    w_out = mix_w_out.astype(BF16)
    w_f = fourier_w.astype(BF16)
    qkg = qk_g.reshape(n_even, 2, 1, NA_HEAD_DIM)
    conv_w8 = jnp.pad(conv_w, ((0, 0), (0, SUBLANE - CONV_K), (0, 0)))
    conv_b2 = conv_b.reshape(n_even, 1, CONV_CH)
    pad_l = LANE - 2 * SSD_HEADS
    dtb = jnp.pad(dt_bias.reshape(n_even, 1, 2 * SSD_HEADS), ((0, 0), (0, 0), (0, pad_l)))
    alog = jnp.pad(a_log.reshape(n_even, 1, 2 * SSD_HEADS), ((0, 0), (0, 0), (0, pad_l)))
    dsk = jnp.repeat(ssd_d, SSD_HEAD_DIM, axis=-1).reshape(n_even, 1, SSD_INNER)
    ng = ssd_norm_g.reshape(n_even, 1, SSD_INNER)
    cos_t, sin_t = _rope_tables()
    chan = _phase_table(jnp.arange(F_GROUP_CH, dtype=jnp.int32), F_GROUP_CH)
    cs_chan = jnp.concatenate([chan[0], chan[1]], axis=1).astype(BF16)
    dft_lat = _dft_tables(SEQ)
    dft_ctx = _dft_tables(CTX_LEN)

    last_ctx = ((DEPTH - 1) // 2) * 2
    for i in range(DEPTH):
        use_ctx = i <= last_ctx
        ctx_out = i < last_ctx
        j = i // 2
        mods_i = mods[i]
        g = norm_g[i].reshape(3, 1, D_MODEL)
        rows_in = N_TOK if use_ctx else N_LAT
        rows_out = N_TOK if ctx_out else N_LAT
        xs = _ffn(xs, rows_in, mods_i, 0, g[0], w1, w3, w2, i, 0)
        if i % 2 == 0:
            assert use_ctx, "even mixer layers read the context keys and SSD states"
            proj = _in_proj(xs, rows_in, mods_i, g[1], w_in, j)
            bias = _attention_bias(rpb[j])
            attn, attn_ctx = _na_attention(proj, cos_t, sin_t, qkg[j], bias)
            if ctx_out:
                attn = jnp.concatenate([attn, attn_ctx], axis=0)
            act = _conv_silu(proj, conv_w8, conv_b2, j)
            y_f, y_b = _ssd_scan(act, proj, dtb, alog, j)
            xs = _out_proj(attn, y_f, y_b, act, proj, dsk, ng, w_out, xs, rows_out, mods_i, j)
        else:
            p, q = _fourier_chan(xs, rows_out, mods_i, g[1], cs_chan)
            new = _fourier_pos(dft_lat, p, q, w_f, xs, mods_i, j, SEQ, 0, False)
            if ctx_out:
                new_ctx = _fourier_pos(dft_ctx, p, q, w_f, xs, mods_i, j, CTX_LEN, N_LAT, True)
                new = jnp.concatenate([new, new_ctx], axis=0)
            xs = new
        xs = _ffn(xs, rows_out, mods_i, 2, g[2], w1, w3, w2, i, 1)
    return xs[:N_LAT].reshape(BATCH, SEQ, D_MODEL)
```

```python
import functools

import numpy as np
import jax
import jax.numpy as jnp
from jax import lax
from jax.experimental import pallas as pl
from jax.experimental.pallas import tpu as pltpu

F32 = jnp.float32
BF16 = jnp.bfloat16

D_MODEL = 2048
BATCH = 2
SEQ = 4096
DEPTH = 4
GRID_W = 64
GRID_ROWS = SEQ // GRID_W
CTX_LEN = 256
N_LAT = BATCH * SEQ
N_TOK = N_LAT + BATCH * CTX_LEN
NA_HEADS = 8
NA_HEAD_DIM = 128
NA_WIDTH = NA_HEADS * NA_HEAD_DIM
WIN_R = 8
WIN_C = 16
ROPE_BASE = 10000.0
SSD_HEADS = 16
SSD_HEAD_DIM = 64
SSD_INNER = SSD_HEADS * SSD_HEAD_DIM
SSD_GROUPS = 2
SSD_STATE = 128
SSD_CHUNK = 128
CONV_K = 5
CONV_CH = SSD_INNER + 2 * SSD_GROUPS * SSD_STATE
IN_W = 3 * NA_WIDTH + SSD_INNER + CONV_CH + 2 * SSD_HEADS
LANE = 128
IN_W_PAD = ((IN_W + LANE - 1) // LANE) * LANE
F_GROUPS = 4
F_GROUP_CH = D_MODEL // F_GROUPS
D_FF = 5632
N_MOD = 9
EPS = 1e-6
ATTN_SCALE = NA_HEAD_DIM ** -0.5

COL_Q = 0
COL_K = NA_WIDTH // LANE
COL_V = 2 * NA_WIDTH // LANE
COL_Z = 3 * NA_WIDTH // LANE
COL_XBC = (3 * NA_WIDTH + SSD_INNER) // LANE
COL_DT = (3 * NA_WIDTH + SSD_INNER + CONV_CH) // LANE

Q_ROWS = 16
Q_BLK = Q_ROWS * GRID_W
Q_SPLIT = 4
PART_ROWS = Q_ROWS // Q_SPLIT
PART_Q = PART_ROWS * GRID_W
K_ROWS = 12
K_BLK = K_ROWS * GRID_W
N_QBLK = GRID_ROWS // Q_ROWS

VMEM_LIMIT = 56 * 1024 * 1024


def _cparams(sem, vmem_limit=VMEM_LIMIT):
    return pltpu.CompilerParams(dimension_semantics=sem, vmem_limit_bytes=vmem_limit)


def _sigmoid(x):
    return 1.0 / (1.0 + jnp.exp(-x))


def _silu(x):
    return x * _sigmoid(x)


def _mod_norm(x, g, shift, scale):
    ms = jnp.mean(x * x, axis=-1, keepdims=True)
    y = x * lax.rsqrt(ms + EPS) * g
    return y * (1.0 + scale) + shift


def _mod_row(tile_rows):
    def f(t):
        return jnp.minimum((t * tile_rows) // SEQ, BATCH)
    return f


def _mod_kernel(c_ref, w_ref, b_ref, o_ref):
    s = _silu(c_ref[...]).astype(BF16)
    o_ref[...] = jnp.dot(s, w_ref[...].astype(BF16), preferred_element_type=F32) + b_ref[...]


def _modulation(c, c_ctx, mod_w, mod_b):
    rows = 8
    cvec = jnp.concatenate([c, c_ctx[None, :], jnp.zeros((rows - BATCH - 1, D_MODEL), F32)], axis=0)
    n = N_MOD * D_MODEL
    tn = 2048
    out = pl.pallas_call(
        _mod_kernel,
        grid=(DEPTH, n // tn),
        in_specs=[
            pl.BlockSpec((rows, D_MODEL), lambda i, j: (0, 0)),
            pl.BlockSpec((None, D_MODEL, tn), lambda i, j: (i, 0, j)),
            pl.BlockSpec((None, 1, tn), lambda i, j: (i, 0, j)),
        ],
        out_specs=pl.BlockSpec((None, rows, tn), lambda i, j: (i, 0, j)),
        out_shape=jax.ShapeDtypeStruct((DEPTH, rows, n), F32),
        compiler_params=_cparams(("arbitrary", "arbitrary")),
        name="modulation",
    )(cvec, mod_w, mod_b.reshape(DEPTH, 1, n))
    return out[:, :BATCH + 1].reshape(DEPTH, BATCH + 1, N_MOD, 1, D_MODEL)


FFN_TM = 1024
FFN_TF = 512
FFN_VMEM_LIMIT = 63 * 1024 * 1024


def _ffn_accumulate(h_scr, w1_ref, w3_ref, w2_ref, o_ref, rows):
    h = h_scr[0:rows, :]
    a = jnp.dot(h, w1_ref[...], preferred_element_type=F32)
    b = jnp.dot(h, w3_ref[...], preferred_element_type=F32)
    u = (_silu(a) * b).astype(BF16)
    o_ref[0:rows, :] += jnp.dot(u, w2_ref[...], preferred_element_type=F32)


def _ffn_kernel(x_ref, mod_ref, g_ref, w1_ref, w3_ref, w2_ref, o_ref, h_scr, *, tail_rows):
    t = pl.program_id(0)
    j = pl.program_id(1)
    last_t = pl.num_programs(0) - 1

    @pl.when(j == 0)
    def _():
        h_scr[...] = _mod_norm(x_ref[...], g_ref[...], mod_ref[0], mod_ref[1]).astype(BF16)
        o_ref[...] = jnp.zeros_like(o_ref)

    if tail_rows == FFN_TM:
        _ffn_accumulate(h_scr, w1_ref, w3_ref, w2_ref, o_ref, FFN_TM)
    else:
        @pl.when(t < last_t)
        def _():
            _ffn_accumulate(h_scr, w1_ref, w3_ref, w2_ref, o_ref, FFN_TM)

        @pl.when(t == last_t)
        def _():
            _ffn_accumulate(h_scr, w1_ref, w3_ref, w2_ref, o_ref, tail_rows)

    @pl.when(j == pl.num_programs(1) - 1)
    def _():
        o_ref[...] = x_ref[...] + 0.5 * mod_ref[2] * o_ref[...]


def _ffn(x, n_rows, mods_i, sub, g, w1, w3, w2, layer, which):
    tm, tf = FFN_TM, FFN_TF
    mrow = _mod_row(tm)
    nt = pl.cdiv(n_rows, tm)
    tail_rows = n_rows - (nt - 1) * tm
    return pl.pallas_call(
        functools.partial(_ffn_kernel, tail_rows=tail_rows),
        grid=(nt, D_FF // tf),
        in_specs=[
            pl.BlockSpec((tm, D_MODEL), lambda t, j: (t, 0)),
            pl.BlockSpec((None, 3, 1, D_MODEL), lambda t, j: (mrow(t), sub, 0, 0)),
            pl.BlockSpec((1, D_MODEL), lambda t, j: (0, 0)),
            pl.BlockSpec((None, None, D_MODEL, tf), lambda t, j: (layer, which, 0, j)),
            pl.BlockSpec((None, None, D_MODEL, tf), lambda t, j: (layer, which, 0, j)),
            pl.BlockSpec((None, None, tf, D_MODEL), lambda t, j: (layer, which, j, 0)),
        ],
        out_specs=pl.BlockSpec((tm, D_MODEL), lambda t, j: (t, 0)),
        out_shape=jax.ShapeDtypeStruct((n_rows, D_MODEL), F32),
        scratch_shapes=[pltpu.VMEM((tm, D_MODEL), BF16)],
        compiler_params=_cparams(("arbitrary", "arbitrary"), FFN_VMEM_LIMIT),
        name="ffn",
    )(x, mods_i, g, w1, w3, w2)


def _norm_matmul_kernel(x_ref, mod_ref, g_ref, w_ref, o_ref, h_scr):
    @pl.when(pl.program_id(1) == 0)
    def _():
        h_scr[...] = _mod_norm(x_ref[...], g_ref[...], mod_ref[0], mod_ref[1]).astype(BF16)

    o_ref[...] = jnp.dot(h_scr[...], w_ref[...], preferred_element_type=F32)


def _in_proj(x, n_rows, mods_i, g, w_in, j_even):
    tm, tn = 512, 1920
    mrow = _mod_row(tm)
    return pl.pallas_call(
        _norm_matmul_kernel,
        grid=(n_rows // tm, IN_W_PAD // tn),
        in_specs=[
            pl.BlockSpec((tm, D_MODEL), lambda t, j: (t, 0)),
            pl.BlockSpec((None, 3, 1, D_MODEL), lambda t, j: (mrow(t), 1, 0, 0)),
            pl.BlockSpec((1, D_MODEL), lambda t, j: (0, 0)),
            pl.BlockSpec((None, D_MODEL, tn), lambda t, j: (j_even, 0, j)),
        ],
        out_specs=pl.BlockSpec((tm, tn), lambda t, j: (t, j)),
        out_shape=jax.ShapeDtypeStruct((n_rows, IN_W_PAD), F32),
        scratch_shapes=[pltpu.VMEM((tm, D_MODEL), BF16)],
        compiler_params=_cparams(("arbitrary", "arbitrary")),
        name="in_proj",
    )(x, mods_i, g, w_in)


def _head_rms(x, g):
    ms = jnp.mean(x * x, axis=-1, keepdims=True)
    return x * lax.rsqrt(ms + EPS) * g


def _rope(x, cos, sin):
    lane = lax.broadcasted_iota(jnp.int32, x.shape, 1)
    first_half = (lane & (NA_HEAD_DIM // 4)) == 0
    partner = jnp.where(first_half, pltpu.roll(x, NA_HEAD_DIM - NA_HEAD_DIM // 4, 1),
                        pltpu.roll(x, NA_HEAD_DIM // 4, 1))
    return x * cos + partner * sin


def _dot_nt(a, b):
    return lax.dot_general(a, b, (((1,), (1,)), ((), ())), preferred_element_type=F32)


def _softmax_pv(scores, values):
    m = functools.reduce(jnp.maximum, [jnp.max(s, axis=-1, keepdims=True) for s in scores])
    ps = [jnp.exp(s - m) for s in scores]
    denom = functools.reduce(jnp.add, [jnp.sum(p, axis=-1, keepdims=True) for p in ps])
    o = functools.reduce(jnp.add, [jnp.dot(p.astype(BF16), v, preferred_element_type=F32)
                                   for p, v in zip(ps, values)])
    return o / denom


def _part_key_row(j, part):
    return jnp.clip(Q_ROWS * j + PART_ROWS * part - WIN_R // 2, 0, GRID_ROWS - K_ROWS)


def _na_kernel(*refs):
    (q_ref, k_ref, v_ref, qc_ref, kc_ref, vc_ref, cos_ref, sin_ref, qkg_ref), refs = refs[:9], refs[9:]
    bias_refs, (o_ref, oc_ref, k_scr, v_scr, kc_scr, vc_scr) = refs[:Q_SPLIT], refs[Q_SPLIT:]
    j = pl.program_id(2)

    @pl.when(j == 0)
    def _():
        kn = _head_rms(k_ref[...], qkg_ref[1])
        k_scr[...] = _rope(kn, cos_ref[...], sin_ref[...]).astype(BF16)
        v_scr[...] = v_ref[...].astype(BF16)
        kcn = _head_rms(kc_ref[...], qkg_ref[1]).astype(BF16)
        vcb = vc_ref[...].astype(BF16)
        kc_scr[...] = kcn
        vc_scr[...] = vcb
        qcn = _head_rms(qc_ref[...], qkg_ref[0]).astype(BF16)
        oc_ref[...] = _softmax_pv([_dot_nt(qcn, kcn) * ATTN_SCALE], [vcb]).astype(oc_ref.dtype)

    for part in range(Q_SPLIT):
        rows = pl.ds(part * PART_Q, PART_Q)
        q0 = pl.multiple_of(j * Q_BLK + part * PART_Q, PART_Q)
        k0 = pl.multiple_of(_part_key_row(j, part) * GRID_W, GRID_W)
        kw = k_scr[pl.ds(k0, K_BLK), :]
        vw = v_scr[pl.ds(k0, K_BLK), :]
        qn = _head_rms(q_ref[rows, :], qkg_ref[0])
        qr = _rope(qn, cos_ref[pl.ds(q0, PART_Q), :], sin_ref[pl.ds(q0, PART_Q), :]).astype(BF16)
        s_win = _dot_nt(qr, kw) * ATTN_SCALE + bias_refs[part][...]
        s_ctx = _dot_nt(qn.astype(BF16), kc_scr[...]) * ATTN_SCALE
        o_ref[rows, :] = _softmax_pv([s_win, s_ctx], [vw, vc_scr[...]]).astype(o_ref.dtype)


BIAS_PARTS = ((0, 0), (N_QBLK // 2, 0), (N_QBLK - 1, Q_SPLIT - 1))


def _bias_pattern(j, part):
    first = (j == 0) & (part == 0)
    last = (j == N_QBLK - 1) & (part == Q_SPLIT - 1)
    return jnp.where(first, 0, jnp.where(last, 2, 1))


def _na_attention(proj, cos_t, sin_t, qkg, bias):
    lat_blk = SEQ // Q_BLK
    ctx_blk0 = N_LAT // CTX_LEN
    hd = NA_HEAD_DIM
    bias_specs = [pl.BlockSpec((None, None, PART_Q, K_BLK), lambda b, h, j, p=p: (h, _bias_pattern(j, p), 0, 0))
                  for p in range(Q_SPLIT)]
    return pl.pallas_call(
        _na_kernel,
        grid=(BATCH, NA_HEADS, N_QBLK),
        in_specs=[
            pl.BlockSpec((Q_BLK, hd), lambda b, h, j: (b * lat_blk + j, COL_Q + h)),
            pl.BlockSpec((SEQ, hd), lambda b, h, j: (b, COL_K + h)),
            pl.BlockSpec((SEQ, hd), lambda b, h, j: (b, COL_V + h)),
            pl.BlockSpec((CTX_LEN, hd), lambda b, h, j: (ctx_blk0 + b, COL_Q + h)),
            pl.BlockSpec((CTX_LEN, hd), lambda b, h, j: (ctx_blk0 + b, COL_K + h)),
            pl.BlockSpec((CTX_LEN, hd), lambda b, h, j: (ctx_blk0 + b, COL_V + h)),
            pl.BlockSpec((SEQ, hd), lambda b, h, j: (0, 0)),
            pl.BlockSpec((SEQ, hd), lambda b, h, j: (0, 0)),
            pl.BlockSpec((2, 1, hd), lambda b, h, j: (0, 0, 0)),
        ] + bias_specs,
        out_specs=[pl.BlockSpec((Q_BLK, hd), lambda b, h, j: (b * lat_blk + j, h)),
                   pl.BlockSpec((CTX_LEN, hd), lambda b, h, j: (b, h))],
        out_shape=[jax.ShapeDtypeStruct((N_LAT, NA_WIDTH), BF16),
                   jax.ShapeDtypeStruct((BATCH * CTX_LEN, NA_WIDTH), BF16)],
        scratch_shapes=[pltpu.VMEM((SEQ, hd), BF16), pltpu.VMEM((SEQ, hd), BF16),
                        pltpu.VMEM((CTX_LEN, hd), BF16), pltpu.VMEM((CTX_LEN, hd), BF16)],
        compiler_params=_cparams(("arbitrary", "arbitrary", "arbitrary")),
        name="na_attention",
    )(proj, proj, proj, proj, proj, proj, cos_t, sin_t, qkg, *([bias] * Q_SPLIT))


def _part_geometry(jb, part):
    rows = Q_ROWS * jb + PART_ROWS * part + np.arange(PART_ROWS)
    k_first = int(np.clip(rows[0] - WIN_R // 2, 0, GRID_ROWS - K_ROWS))
    r0 = np.clip(rows - WIN_R // 2, 0, GRID_ROWS - WIN_R)
    return rows, k_first, r0


def _attention_bias(rpb_j):
    for jb in range(N_QBLK):
        for part in range(Q_SPLIT):
            rows, k_first, r0 = _part_geometry(jb, part)
            pat = 0 if (jb, part) == (0, 0) else 2 if (jb, part) == (N_QBLK - 1, Q_SPLIT - 1) else 1
            rows_p, k_first_p, r0_p = _part_geometry(*BIAS_PARTS[pat])
            assert k_first - rows[0] == k_first_p - rows_p[0] and np.array_equal(r0 - rows, r0_p - rows_p)
    n_dr = 2 * WIN_R - 1
    qcol = np.arange(GRID_W)
    c0 = np.clip(qcol - WIN_C // 2, 0, GRID_W - WIN_C)
    col_ok = (qcol[None, :] >= c0[:, None]) & (qcol[None, :] < c0[:, None] + WIN_C)
    r = rpb_j.astype(F32)
    edge = GRID_W - WIN_C
    ep = jnp.concatenate([jnp.repeat(r[..., :1], edge, axis=-1), r, jnp.repeat(r[..., -1:], edge, axis=-1)],
                         axis=-1)
    t1 = jnp.stack([ep[..., GRID_W - 1 - qc:2 * GRID_W - 1 - qc] for qc in range(GRID_W)], axis=2)
    t1 = jnp.where(jnp.asarray(col_ok)[None, None], t1, -jnp.inf)
    band = t1.transpose(0, 2, 1, 3).reshape(NA_HEADS, GRID_W, n_dr * GRID_W)
    pad_w = K_BLK
    band = jnp.pad(band, ((0, 0), (0, 0), (pad_w, pad_w)), constant_values=-jnp.inf)
    strips, masks = [], []
    for jb, part in BIAS_PARTS:
        rows, k_first, r0 = _part_geometry(jb, part)
        kr = k_first + np.arange(K_ROWS)
        for row, row0 in zip(rows, r0):
            off = pad_w + (k_first - int(row) + WIN_R - 1) * GRID_W
            assert 0 <= off and off + K_BLK <= band.shape[-1]
            strips.append(band[:, :, off:off + K_BLK])
            masks.append(np.repeat((kr >= row0) & (kr < row0 + WIN_R), GRID_W))
    vals = jnp.stack(strips, axis=1)
    mask = np.stack(masks)[None, :, None, :]
    bias = jnp.where(jnp.asarray(mask), vals, -jnp.inf)
    return bias.reshape(NA_HEADS, len(BIAS_PARTS), PART_Q, K_BLK)


def _rope_tables():
    quarter = NA_HEAD_DIM // 4
    inv_freq = ROPE_BASE ** (-jnp.arange(quarter, dtype=F32) / quarter)
    t = jnp.arange(SEQ)
    ang_r = (t // GRID_W).astype(F32)[:, None] * inv_freq[None, :]
    ang_c = (t % GRID_W).astype(F32)[:, None] * inv_freq[None, :]
    cos_t = jnp.concatenate([jnp.cos(ang_r), jnp.cos(ang_r), jnp.cos(ang_c), jnp.cos(ang_c)], axis=-1)
    sin_t = jnp.concatenate([-jnp.sin(ang_r), jnp.sin(ang_r), -jnp.sin(ang_c), jnp.sin(ang_c)], axis=-1)
    return cos_t, sin_t


SUBLANE = 8
CONV_HALO = SUBLANE
CONV_BLK = 256


CONV_CW = 512


def _conv_kernel(*refs):
    nc = CONV_CH // CONV_CW
    x_refs, prev_refs, next_refs = refs[:nc], refs[nc:2 * nc], refs[2 * nc:3 * nc]
    w_ref, b_ref, o_ref, pad_scr = refs[3 * nc:]
    t = pl.program_id(0)
    n_lat_blk = N_LAT // CONV_BLK
    is_ctx = t >= n_lat_blk
    seq_blks = jnp.where(is_ctx, CTX_LEN // CONV_BLK, SEQ // CONV_BLK)
    pos = jnp.where(is_ctx, t - n_lat_blk, t) % seq_blks
    for c in range(nc):
        cs = slice(c * CONV_CW, (c + 1) * CONV_CW)
        pad_scr[0:CONV_HALO, :] = jnp.where(pos == 0, 0.0, prev_refs[c][...])
        pad_scr[CONV_HALO:CONV_HALO + CONV_BLK, :] = x_refs[c][...]
        pad_scr[CONV_HALO + CONV_BLK:2 * CONV_HALO + CONV_BLK, :] = jnp.where(pos == seq_blks - 1, 0.0,
                                                                               next_refs[c][...])
        xp = pad_scr[...]
        acc = jnp.zeros((CONV_BLK, CONV_CW), F32) + b_ref[:, cs]
        for k in range(CONV_K):
            lo = CONV_HALO - CONV_K // 2 + k
            acc = acc + w_ref[k:k + 1, cs] * xp[lo:lo + CONV_BLK, :]
        o_ref[:, cs] = _silu(acc)


def _conv_silu(proj, conv_w8, conv_b2, j_even):
    n_rows = proj.shape[0]
    nc = CONV_CH // CONV_CW
    col0 = COL_XBC * LANE // CONV_CW
    halo_per_blk = CONV_BLK // CONV_HALO
    last_halo = n_rows // CONV_HALO - 1
    x_specs = [pl.BlockSpec((CONV_BLK, CONV_CW), lambda t, c=c: (t, col0 + c)) for c in range(nc)]
    prev_specs = [pl.BlockSpec((CONV_HALO, CONV_CW),
                               lambda t, c=c: (jnp.maximum(t * halo_per_blk - 1, 0), col0 + c)) for c in range(nc)]
    next_specs = [pl.BlockSpec((CONV_HALO, CONV_CW),
                               lambda t, c=c: (jnp.minimum((t + 1) * halo_per_blk, last_halo), col0 + c))
                  for c in range(nc)]
    return pl.pallas_call(
        _conv_kernel,
        grid=(n_rows // CONV_BLK,),
        in_specs=x_specs + prev_specs + next_specs + [
            pl.BlockSpec((None, SUBLANE, CONV_CH), lambda t: (j_even, 0, 0)),
            pl.BlockSpec((None, 1, CONV_CH), lambda t: (j_even, 0, 0)),
        ],
        out_specs=pl.BlockSpec((CONV_BLK, CONV_CH), lambda t: (t, 0)),
        out_shape=jax.ShapeDtypeStruct((n_rows, CONV_CH), F32),
        scratch_shapes=[pltpu.VMEM((CONV_BLK + 2 * CONV_HALO, CONV_CW), F32)],
        compiler_params=_cparams(("arbitrary",)),
        name="conv_silu",
    )(*([proj] * (3 * nc)), conv_w8, conv_b2)


def _softplus(x):
    return jnp.maximum(x, 0.0) + jnp.log1p(jnp.exp(-jnp.abs(x)))


def _dot_select(a, sel):
    sel = sel.astype(BF16)
    hi = a.astype(BF16)
    r1 = a - hi.astype(F32)
    mid = r1.astype(BF16)
    lo = (r1 - mid.astype(F32)).astype(BF16)
    return (jnp.dot(hi, sel, preferred_element_type=F32) + jnp.dot(mid, sel, preferred_element_type=F32)
            + jnp.dot(lo, sel, preferred_element_type=F32))


def _select_dot(sel, a):
    sel = sel.astype(BF16)
    hi = a.astype(BF16)
    r1 = a - hi.astype(F32)
    mid = r1.astype(BF16)
    lo = (r1 - mid.astype(F32)).astype(BF16)
    return (jnp.dot(sel, hi, preferred_element_type=F32) + jnp.dot(sel, mid, preferred_element_type=F32)
            + jnp.dot(sel, lo, preferred_element_type=F32))


def _ssd_direction(xs_ref, b_ref, c_ref, dt_ref, dtb_ref, alog_ref, y_ref, h_scr, lane0, forward):
    q = SSD_CHUNK
    gw = SSD_INNER // SSD_GROUPS
    dt = _softplus(dt_ref[...] + dtb_ref[...])
    a = dt * (-jnp.exp(alog_ref[...]))
    ri = lax.broadcasted_iota(jnp.int32, (q, q), 0)
    ci = lax.broadcasted_iota(jnp.int32, (q, q), 1)
    tri = (ci <= ri) if forward else (ci >= ri)
    a_cum = _select_dot(tri.astype(F32), a)
    a_cum_t = a_cum.T
    er = lax.broadcasted_iota(jnp.int32, (LANE, SSD_INNER), 0)
    ec = lax.broadcasted_iota(jnp.int32, (LANE, SSD_INNER), 1)
    expand = ((er - lane0) == (ec // SSD_HEAD_DIM)).astype(F32)
    dt_e = _dot_select(dt, expand)
    ac_e = _dot_select(a_cum, expand)
    end = q - 1 if forward else 0
    a_end_e = ac_e[end:end + 1, :]
    xdt = xs_ref[...] * dt_e
    in_decay = jnp.exp(ac_e)
    out_decay = jnp.exp(a_end_e - ac_e)
    state_decay = jnp.exp(a_end_e)
    lane = lax.broadcasted_iota(jnp.int32, (q, LANE), 1)
    for g in range(SSD_GROUPS):
        gs = slice(g * gw, (g + 1) * gw)
        bg_t = b_ref[:, g * SSD_STATE:(g + 1) * SSD_STATE].T.astype(BF16)
        cg = c_ref[:, g * SSD_STATE:(g + 1) * SSD_STATE].astype(BF16)
        cb = jnp.dot(cg, bg_t, preferred_element_type=F32)
        h_t = h_scr[:, gs]
        y_inter = jnp.dot(cg, h_t.astype(BF16), preferred_element_type=F32) * in_decay[:, gs]
        for pair in range(gw // LANE):
            cs = slice(g * gw + pair * LANE, g * gw + (pair + 1) * LANE)
            x_pair = xdt[:, cs].astype(BF16)
            res = []
            for sub in range(LANE // SSD_HEAD_DIM):
                hl = lane0 + (g * gw + pair * LANE) // SSD_HEAD_DIM + sub
                seg = a_cum[:, hl:hl + 1] - a_cum_t[hl:hl + 1, :]
                decay = jnp.exp(jnp.where(tri, seg, -jnp.inf))
                res.append(jnp.dot((cb * decay).astype(BF16), x_pair, preferred_element_type=F32))
            y_pair = jnp.where(lane < SSD_HEAD_DIM, res[0], res[1])
            y_ref[:, cs] = y_pair + y_inter[:, pair * LANE:(pair + 1) * LANE]
        x_out = (xdt[:, gs] * out_decay[:, gs]).astype(BF16)
        h_scr[:, gs] = state_decay[:, gs] * h_t + jnp.dot(bg_t, x_out, preferred_element_type=F32)


def _ssd_kernel(xs_f, b_f, c_f, dt_f, xs_b, b_b, c_b, dt_b, dtb_ref, alog_ref, yf_ref, yb_ref,
                hf_scr, hb_scr):
    @pl.when(pl.program_id(1) == 0)
    def _():
        hf_scr[...] = jnp.zeros_like(hf_scr)
        hb_scr[...] = jnp.zeros_like(hb_scr)

    _ssd_direction(xs_f, b_f, c_f, dt_f, dtb_ref, alog_ref, yf_ref, hf_scr, 0, True)
    _ssd_direction(xs_b, b_b, c_b, dt_b, dtb_ref, alog_ref, yb_ref, hb_scr, SSD_HEADS, False)


def _ssd_scan(act, proj, dtb, alog, j_even):
    q = SSD_CHUNK
    n_rows = act.shape[0]
    n_ctx = CTX_LEN // q
    n_lat = SEQ // q
    ctx_blk0 = N_LAT // q

    def fwd_blk(b, s):
        return jnp.where(s < n_ctx, ctx_blk0 + b * n_ctx + s, b * n_lat + (s - n_ctx))

    def bwd_blk(b, s):
        return jnp.where(s < n_ctx, ctx_blk0 + b * n_ctx + (n_ctx - 1 - s),
                         b * n_lat + (n_lat - 1 - (s - n_ctx)))

    gn = SSD_GROUPS * SSD_STATE
    b_col = SSD_INNER // gn
    c_col = b_col + 1

    def specs(blk):
        return [
            pl.BlockSpec((q, SSD_INNER), lambda b, s: (blk(b, s), 0)),
            pl.BlockSpec((q, gn), lambda b, s: (blk(b, s), b_col)),
            pl.BlockSpec((q, gn), lambda b, s: (blk(b, s), c_col)),
            pl.BlockSpec((q, LANE), lambda b, s: (blk(b, s), COL_DT)),
        ]

    small = pl.BlockSpec((None, 1, LANE), lambda b, s: (j_even, 0, 0))
    return pl.pallas_call(
        _ssd_kernel,
        grid=(BATCH, n_ctx + n_lat),
        in_specs=specs(fwd_blk) + specs(bwd_blk) + [small, small],
        out_specs=[pl.BlockSpec((q, SSD_INNER), lambda b, s: (fwd_blk(b, s), 0)),
                   pl.BlockSpec((q, SSD_INNER), lambda b, s: (bwd_blk(b, s), 0))],
        out_shape=[jax.ShapeDtypeStruct((n_rows, SSD_INNER), F32)] * 2,
        scratch_shapes=[pltpu.VMEM((SSD_STATE, SSD_INNER), F32)] * 2,
        compiler_params=_cparams(("arbitrary", "arbitrary")),
        name="ssd_scan",
    )(act, act, act, proj, act, act, act, proj, dtb, alog)


def _out_proj_kernel(attn_ref, yf_ref, yb_ref, xs_ref, z0_ref, z1_ref, dsk_ref, ng_ref, w_ref, x_ref, mod_ref,
                     o_ref):
    mix = jnp.dot(attn_ref[...], w_ref[0:NA_WIDTH, :], preferred_element_type=F32)
    gw = SSD_INNER // SSD_GROUPS
    for g, z_ref in enumerate((z0_ref, z1_ref)):
        gs = slice(g * gw, (g + 1) * gw)
        y = yf_ref[:, gs] + yb_ref[:, gs] + dsk_ref[:, gs] * xs_ref[:, gs]
        seg = y * _silu(z_ref[...])
        ms = jnp.mean(seg * seg, axis=-1, keepdims=True)
        yn = (seg * lax.rsqrt(ms + EPS) * ng_ref[:, g * gw:(g + 1) * gw]).astype(BF16)
        mix = mix + jnp.dot(yn, w_ref[NA_WIDTH + g * gw:NA_WIDTH + (g + 1) * gw, :],
                            preferred_element_type=F32)
    o_ref[...] = x_ref[...] + mod_ref[2] * mix


def _out_proj(attn, y_f, y_b, act, proj, dsk, ng, w_out, x, n_rows, mods_i, j_even):
    tm = 512
    mrow = _mod_row(tm)
    gw = SSD_INNER // SSD_GROUPS
    z_col = COL_Z * LANE // gw
    assert SSD_GROUPS == 2
    return pl.pallas_call(
        _out_proj_kernel,
        grid=(n_rows // tm,),
        in_specs=[
            pl.BlockSpec((tm, NA_WIDTH), lambda t: (t, 0)),
            pl.BlockSpec((tm, SSD_INNER), lambda t: (t, 0)),
            pl.BlockSpec((tm, SSD_INNER), lambda t: (t, 0)),
            pl.BlockSpec((tm, SSD_INNER), lambda t: (t, 0)),
            pl.BlockSpec((tm, gw), lambda t: (t, z_col)),
            pl.BlockSpec((tm, gw), lambda t: (t, z_col + 1)),
            pl.BlockSpec((None, 1, SSD_INNER), lambda t: (j_even, 0, 0)),
            pl.BlockSpec((None, 1, SSD_INNER), lambda t: (j_even, 0, 0)),
            pl.BlockSpec((None, D_MODEL, D_MODEL), lambda t: (j_even, 0, 0)),
            pl.BlockSpec((tm, D_MODEL), lambda t: (t, 0)),
            pl.BlockSpec((None, 3, 1, D_MODEL), lambda t: (mrow(t), 1, 0, 0)),
        ],
        out_specs=pl.BlockSpec((tm, D_MODEL), lambda t: (t, 0)),
        out_shape=jax.ShapeDtypeStruct((n_rows, D_MODEL), F32),
        compiler_params=_cparams(("arbitrary",)),
        name="out_proj",
    )(attn, y_f, y_b, act, proj, proj, dsk, ng, w_out, x, mods_i)


FOURIER_TILE = 512


def _fourier_chan_kernel(x_ref, mod_ref, g_ref, cs_ref, pq_ref):
    h = _mod_norm(x_ref[...], g_ref[...], mod_ref[0], mod_ref[1])
    for g in range(F_GROUPS):
        gs = slice(g * F_GROUP_CH, (g + 1) * F_GROUP_CH)
        pq = jnp.dot(h[:, gs].astype(BF16), cs_ref[...], preferred_element_type=F32)
        pq_ref[0, :, gs] = pq[:, :F_GROUP_CH].astype(BF16)
        pq_ref[1, :, gs] = pq[:, F_GROUP_CH:].astype(BF16)


def _fourier_chan(x, n_rows, mods_i, g, cs):
    tm = FOURIER_TILE
    mrow = _mod_row(tm)
    return pl.pallas_call(
        _fourier_chan_kernel,
        grid=(n_rows // tm,),
        in_specs=[
            pl.BlockSpec((tm, D_MODEL), lambda t: (t, 0)),
            pl.BlockSpec((None, 3, 1, D_MODEL), lambda t: (mrow(t), 1, 0, 0)),
            pl.BlockSpec((1, D_MODEL), lambda t: (0, 0)),
            pl.BlockSpec((F_GROUP_CH, 2 * F_GROUP_CH), lambda t: (0, 0)),
        ],
        out_specs=pl.BlockSpec((None, 2, tm, D_MODEL), lambda t: (t, 0, 0, 0)),
        out_shape=jax.ShapeDtypeStruct((n_rows // tm, 2, tm, D_MODEL), BF16),
        compiler_params=_cparams(("arbitrary",)),
        name="fourier_chan",
    )(x, mods_i, g, cs)


DFT_SPLIT = 64


def _fourier_pos_kernel(u_ref, v_ref, pq_ref, w_ref, x_ref, mod_ref, o_ref, acc_scr, cs_scr, *, scale):
    k = pl.program_id(2)
    tk = v_ref.shape[-1]

    @pl.when(k == 0)
    def _():
        acc_scr[...] = jnp.zeros_like(acc_scr)

    vr, vi = v_ref[0], v_ref[1]
    for a in range(u_ref.shape[1]):
        ur, ui = u_ref[0, a:a + 1, :], u_ref[1, a:a + 1, :]
        rows = slice(a * DFT_SPLIT, (a + 1) * DFT_SPLIT)
        cs_scr[rows, 0:tk] = (ur * vr - ui * vi).astype(BF16)
        cs_scr[rows, tk:2 * tk] = (-(ui * vr + ur * vi)).astype(BF16)
    pq = pq_ref[...].reshape(2 * tk, pq_ref.shape[-1])
    acc_scr[...] += jnp.dot(cs_scr[...], pq, preferred_element_type=F32)

    @pl.when(k == pl.num_programs(2) - 1)
    def _():
        f = (acc_scr[...] * scale).astype(BF16)
        o_ref[...] = x_ref[...] + mod_ref[2] * jnp.dot(f, w_ref[...], preferred_element_type=F32)


def _fourier_pos(tabs, pq, w, x, mods_i, j_odd, seq_len, row0, is_ctx):
    u_tab, v_tab = tabs
    tm = tk = min(seq_len, FOURIER_TILE)
    nm = seq_len // tm
    blk0 = row0 // tm
    per_tile = FOURIER_TILE // tk
    scale = float((seq_len * F_GROUP_CH) ** -0.5)

    def pq_index(b, m, k):
        blk = blk0 + b * nm + k
        return blk // per_tile, 0, blk % per_tile, 0

    return pl.pallas_call(
        functools.partial(_fourier_pos_kernel, scale=scale),
        grid=(BATCH, nm, seq_len // tk),
        in_specs=[
            pl.BlockSpec((2, tm // DFT_SPLIT, tk), lambda b, m, k: (0, m, k)),
            pl.BlockSpec((2, DFT_SPLIT, tk), lambda b, m, k: (0, 0, k)),
            pl.BlockSpec((None, 2, tk, D_MODEL), pq_index),
            pl.BlockSpec((None, D_MODEL, D_MODEL), lambda b, m, k: (j_odd, 0, 0)),
            pl.BlockSpec((tm, D_MODEL), lambda b, m, k: (blk0 + b * nm + m, 0)),
            pl.BlockSpec((None, 3, 1, D_MODEL), lambda b, m, k: (BATCH if is_ctx else b, 1, 0, 0)),
        ],
        out_specs=pl.BlockSpec((tm, D_MODEL), lambda b, m, k: (b * nm + m, 0)),
        out_shape=jax.ShapeDtypeStruct((BATCH * seq_len, D_MODEL), F32),
        scratch_shapes=[pltpu.VMEM((tm, D_MODEL), F32), pltpu.VMEM((tm, 2 * tk), BF16)],
        compiler_params=_cparams(("arbitrary", "arbitrary", "arbitrary")),
        name="fourier_pos",
    )(u_tab, v_tab, pq, w, x, mods_i)


def _phase_table(freq, n):
    l = jnp.arange(n, dtype=jnp.int32)
    ang = ((freq[:, None] * l[None, :]) % n).astype(F32) * np.float32(2.0 * np.pi / n)
    return jnp.stack([jnp.cos(ang), jnp.sin(ang)])


def _dft_tables(n):
    a = jnp.arange(n // DFT_SPLIT, dtype=jnp.int32) * DFT_SPLIT
    b = jnp.arange(DFT_SPLIT, dtype=jnp.int32)
    return _phase_table(a, n), _phase_table(b, n)


def kernel(x, c, ctx, c_ctx, mod_w, mod_b, norm_g, ffn_w1, ffn_w3, ffn_w2, mix_w_in, mix_w_out, qk_g, rpb,
           conv_w, conv_b, dt_bias, a_log, ssd_d, ssd_norm_g, fourier_w):
    n_even = mix_w_in.shape[0]
    xs = jnp.concatenate([x.reshape(N_LAT, D_MODEL), ctx.reshape(BATCH * CTX_LEN, D_MODEL)], axis=0)
    mods = _modulation(c, c_ctx, mod_w, mod_b)

    w1 = ffn_w1.astype(BF16)
    w3 = ffn_w3.astype(BF16)
    w2 = ffn_w2.astype(BF16)
    w_in = jnp.pad(mix_w_in, ((0, 0), (0, 0), (0, IN_W_PAD - IN_W))).astype(BF16)
    w_out = mix_w_out.astype(BF16)
    w_f = fourier_w.astype(BF16)
    qkg = qk_g.reshape(n_even, 2, 1, NA_HEAD_DIM)
    conv_w8 = jnp.pad(conv_w, ((0, 0), (0, SUBLANE - CONV_K), (0, 0)))
    conv_b2 = conv_b.reshape(n_even, 1, CONV_CH)
    pad_l = LANE - 2 * SSD_HEADS
    dtb = jnp.pad(dt_bias.reshape(n_even, 1, 2 * SSD_HEADS), ((0, 0), (0, 0), (0, pad_l)))
    alog = jnp.pad(a_log.reshape(n_even, 1, 2 * SSD_HEADS), ((0, 0), (0, 0), (0, pad_l)))
    dsk = jnp.repeat(ssd_d, SSD_HEAD_DIM, axis=-1).reshape(n_even, 1, SSD_INNER)
    ng = ssd_norm_g.reshape(n_even, 1, SSD_INNER)
    cos_t, sin_t = _rope_tables()
    chan = _phase_table(jnp.arange(F_GROUP_CH, dtype=jnp.int32), F_GROUP_CH)
    cs_chan = jnp.concatenate([chan[0], chan[1]], axis=1).astype(BF16)
    dft_lat = _dft_tables(SEQ)
    dft_ctx = _dft_tables(CTX_LEN)

    last_ctx = ((DEPTH - 1) // 2) * 2
    for i in range(DEPTH):
        use_ctx = i <= last_ctx
        ctx_out = i < last_ctx
        j = i // 2
        mods_i = mods[i]
        g = norm_g[i].reshape(3, 1, D_MODEL)
        rows_in = N_TOK if use_ctx else N_LAT
        rows_out = N_TOK if ctx_out else N_LAT
        xs = _ffn(xs, rows_in, mods_i, 0, g[0], w1, w3, w2, i, 0)
        if i % 2 == 0:
            assert use_ctx, "even mixer layers read the context keys and SSD states"
            proj = _in_proj(xs, rows_in, mods_i, g[1], w_in, j)
            bias = _attention_bias(rpb[j])
            attn, attn_ctx = _na_attention(proj, cos_t, sin_t, qkg[j], bias)
            if ctx_out:
                attn = jnp.concatenate([attn, attn_ctx], axis=0)
            act = _conv_silu(proj, conv_w8, conv_b2, j)
            y_f, y_b = _ssd_scan(act, proj, dtb, alog, j)
            xs = _out_proj(attn, y_f, y_b, act, proj, dsk, ng, w_out, xs, rows_out, mods_i, j)
        else:
            pq = _fourier_chan(xs, rows_out, mods_i, g[1], cs_chan)
            new = _fourier_pos(dft_lat, pq, w_f, xs, mods_i, j, SEQ, 0, False)
            if ctx_out:
                new_ctx = _fourier_pos(dft_ctx, pq, w_f, xs, mods_i, j, CTX_LEN, N_LAT, True)
                new = jnp.concatenate([new, new_ctx], axis=0)
            xs = new
        xs = _ffn(xs, rows_out, mods_i, 2, g[2], w1, w3, w2, i, 1)
    return xs[:N_LAT].reshape(BATCH, SEQ, D_MODEL)
```

```python
import functools

import numpy as np
import jax
import jax.numpy as jnp
from jax import lax
from jax.experimental import pallas as pl
from jax.experimental.pallas import tpu as pltpu

F32 = jnp.float32
BF16 = jnp.bfloat16

D_MODEL = 2048
BATCH = 2
SEQ = 4096
DEPTH = 4
GRID_W = 64
GRID_ROWS = SEQ // GRID_W
CTX_LEN = 256
N_LAT = BATCH * SEQ
N_TOK = N_LAT + BATCH * CTX_LEN
NA_HEADS = 8
NA_HEAD_DIM = 128
NA_WIDTH = NA_HEADS * NA_HEAD_DIM
WIN_R = 8
WIN_C = 16
ROPE_BASE = 10000.0
SSD_HEADS = 16
SSD_HEAD_DIM = 64
SSD_INNER = SSD_HEADS * SSD_HEAD_DIM
SSD_GROUPS = 2
SSD_STATE = 128
SSD_CHUNK = 128
CONV_K = 5
CONV_CH = SSD_INNER + 2 * SSD_GROUPS * SSD_STATE
IN_W = 3 * NA_WIDTH + SSD_INNER + CONV_CH + 2 * SSD_HEADS
LANE = 128
IN_W_PAD = ((IN_W + LANE - 1) // LANE) * LANE
F_GROUPS = 4
F_GROUP_CH = D_MODEL // F_GROUPS
D_FF = 5632
N_MOD = 9
EPS = 1e-6
ATTN_SCALE = NA_HEAD_DIM ** -0.5

COL_Q = 0
COL_K = NA_WIDTH // LANE
COL_V = 2 * NA_WIDTH // LANE
COL_Z = 3 * NA_WIDTH // LANE
COL_XBC = (3 * NA_WIDTH + SSD_INNER) // LANE
COL_DT = (3 * NA_WIDTH + SSD_INNER + CONV_CH) // LANE

Q_ROWS = 16
Q_BLK = Q_ROWS * GRID_W
Q_SPLIT = 4
PART_ROWS = Q_ROWS // Q_SPLIT
PART_Q = PART_ROWS * GRID_W
K_ROWS = 12
K_BLK = K_ROWS * GRID_W
N_QBLK = GRID_ROWS // Q_ROWS

VMEM_LIMIT = 56 * 1024 * 1024


def _cparams(sem, vmem_limit=VMEM_LIMIT):
    return pltpu.CompilerParams(dimension_semantics=sem, vmem_limit_bytes=vmem_limit)


def _sigmoid(x):
    return 1.0 / (1.0 + jnp.exp(-x))


def _silu(x):
    return x * _sigmoid(x)


def _mod_norm(x, g, shift, scale):
    ms = jnp.mean(x * x, axis=-1, keepdims=True)
    y = x * lax.rsqrt(ms + EPS) * g
    return y * (1.0 + scale) + shift


def _mod_row(tile_rows):
    def f(t):
        return jnp.minimum((t * tile_rows) // SEQ, BATCH)
    return f


def _mod_kernel(c_ref, w_ref, b_ref, o_ref):
    s = _silu(c_ref[...]).astype(BF16)
    o_ref[...] = jnp.dot(s, w_ref[...].astype(BF16), preferred_element_type=F32) + b_ref[...]


def _modulation(c, c_ctx, mod_w, mod_b):
    rows = 8
    cvec = jnp.concatenate([c, c_ctx[None, :], jnp.zeros((rows - BATCH - 1, D_MODEL), F32)], axis=0)
    n = N_MOD * D_MODEL
    tn = 2048
    out = pl.pallas_call(
        _mod_kernel,
        grid=(DEPTH, n // tn),
        in_specs=[
            pl.BlockSpec((rows, D_MODEL), lambda i, j: (0, 0)),
            pl.BlockSpec((None, D_MODEL, tn), lambda i, j: (i, 0, j)),
            pl.BlockSpec((None, 1, tn), lambda i, j: (i, 0, j)),
        ],
        out_specs=pl.BlockSpec((None, rows, tn), lambda i, j: (i, 0, j)),
        out_shape=jax.ShapeDtypeStruct((DEPTH, rows, n), F32),
        compiler_params=_cparams(("arbitrary", "arbitrary")),
        name="modulation",
    )(cvec, mod_w, mod_b.reshape(DEPTH, 1, n))
    return out[:, :BATCH + 1].reshape(DEPTH, BATCH + 1, N_MOD, 1, D_MODEL)


FFN_TM = 1024
FFN_TF = 512
FFN_VMEM_LIMIT = 63 * 1024 * 1024


def _ffn_accumulate(h_scr, w1_ref, w3_ref, w2_ref, o_ref, rows):
    h = h_scr[0:rows, :]
    a = jnp.dot(h, w1_ref[...], preferred_element_type=F32)
    b = jnp.dot(h, w3_ref[...], preferred_element_type=F32)
    u = (_silu(a) * b).astype(BF16)
    o_ref[0:rows, :] += jnp.dot(u, w2_ref[...], preferred_element_type=F32)


def _ffn_kernel(x_ref, mod_ref, g_ref, w1_ref, w3_ref, w2_ref, o_ref, h_scr, *, tail_rows):
    t = pl.program_id(0)
    j = pl.program_id(1)
    last_t = pl.num_programs(0) - 1

    @pl.when(j == 0)
    def _():
        h_scr[...] = _mod_norm(x_ref[...], g_ref[...], mod_ref[0], mod_ref[1]).astype(BF16)
        o_ref[...] = jnp.zeros_like(o_ref)

    if tail_rows == FFN_TM:
        _ffn_accumulate(h_scr, w1_ref, w3_ref, w2_ref, o_ref, FFN_TM)
    else:
        @pl.when(t < last_t)
        def _():
            _ffn_accumulate(h_scr, w1_ref, w3_ref, w2_ref, o_ref, FFN_TM)

        @pl.when(t == last_t)
        def _():
            _ffn_accumulate(h_scr, w1_ref, w3_ref, w2_ref, o_ref, tail_rows)

    @pl.when(j == pl.num_programs(1) - 1)
    def _():
        o_ref[...] = x_ref[...] + 0.5 * mod_ref[2] * o_ref[...]


def _ffn(x, n_rows, mods_i, sub, g, w1, w3, w2, layer, which):
    tm, tf = FFN_TM, FFN_TF
    mrow = _mod_row(tm)
    nt = pl.cdiv(n_rows, tm)
    tail_rows = n_rows - (nt - 1) * tm
    return pl.pallas_call(
        functools.partial(_ffn_kernel, tail_rows=tail_rows),
        grid=(nt, D_FF // tf),
        in_specs=[
            pl.BlockSpec((tm, D_MODEL), lambda t, j: (t, 0)),
            pl.BlockSpec((None, 3, 1, D_MODEL), lambda t, j: (mrow(t), sub, 0, 0)),
            pl.BlockSpec((1, D_MODEL), lambda t, j: (0, 0)),
            pl.BlockSpec((None, None, D_MODEL, tf), lambda t, j: (layer, which, 0, j)),
            pl.BlockSpec((None, None, D_MODEL, tf), lambda t, j: (layer, which, 0, j)),
            pl.BlockSpec((None, None, tf, D_MODEL), lambda t, j: (layer, which, j, 0)),
        ],
        out_specs=pl.BlockSpec((tm, D_MODEL), lambda t, j: (t, 0)),
        out_shape=jax.ShapeDtypeStruct((n_rows, D_MODEL), F32),
        scratch_shapes=[pltpu.VMEM((tm, D_MODEL), BF16)],
        compiler_params=_cparams(("arbitrary", "arbitrary"), FFN_VMEM_LIMIT),
        name="ffn",
    )(x, mods_i, g, w1, w3, w2)


def _norm_matmul_kernel(x_ref, mod_ref, g_ref, w_ref, o_ref, h_scr):
    @pl.when(pl.program_id(1) == 0)
    def _():
        h_scr[...] = _mod_norm(x_ref[...], g_ref[...], mod_ref[0], mod_ref[1]).astype(BF16)

    o_ref[...] = jnp.dot(h_scr[...], w_ref[...], preferred_element_type=F32)


def _in_proj(x, n_rows, mods_i, g, w_in, j_even):
    tm, tn = 512, 1920
    mrow = _mod_row(tm)
    return pl.pallas_call(
        _norm_matmul_kernel,
        grid=(n_rows // tm, IN_W_PAD // tn),
        in_specs=[
            pl.BlockSpec((tm, D_MODEL), lambda t, j: (t, 0)),
            pl.BlockSpec((None, 3, 1, D_MODEL), lambda t, j: (mrow(t), 1, 0, 0)),
            pl.BlockSpec((1, D_MODEL), lambda t, j: (0, 0)),
            pl.BlockSpec((None, D_MODEL, tn), lambda t, j: (j_even, 0, j)),
        ],
        out_specs=pl.BlockSpec((tm, tn), lambda t, j: (t, j)),
        out_shape=jax.ShapeDtypeStruct((n_rows, IN_W_PAD), F32),
        scratch_shapes=[pltpu.VMEM((tm, D_MODEL), BF16)],
        compiler_params=_cparams(("arbitrary", "arbitrary")),
        name="in_proj",
    )(x, mods_i, g, w_in)


def _head_rms(x, g):
    ms = jnp.mean(x * x, axis=-1, keepdims=True)
    return x * lax.rsqrt(ms + EPS) * g


def _rope(x, cos, sin):
    lane = lax.broadcasted_iota(jnp.int32, x.shape, 1)
    first_half = (lane & (NA_HEAD_DIM // 4)) == 0
    partner = jnp.where(first_half, pltpu.roll(x, NA_HEAD_DIM - NA_HEAD_DIM // 4, 1),
                        pltpu.roll(x, NA_HEAD_DIM // 4, 1))
    return x * cos + partner * sin


def _dot_nt(a, b):
    return lax.dot_general(a, b, (((1,), (1,)), ((), ())), preferred_element_type=F32)


def _softmax_pv(scores, values):
    m = functools.reduce(jnp.maximum, [jnp.max(s, axis=-1, keepdims=True) for s in scores])
    ps = [jnp.exp(s - m) for s in scores]
    denom = functools.reduce(jnp.add, [jnp.sum(p, axis=-1, keepdims=True) for p in ps])
    o = functools.reduce(jnp.add, [jnp.dot(p.astype(BF16), v, preferred_element_type=F32)
                                   for p, v in zip(ps, values)])
    return o / denom


def _part_key_row(j, part):
    return jnp.clip(Q_ROWS * j + PART_ROWS * part - WIN_R // 2, 0, GRID_ROWS - K_ROWS)


def _na_kernel(*refs):
    (q_ref, k_ref, v_ref, qc_ref, kc_ref, vc_ref, cos_ref, sin_ref, qkg_ref), refs = refs[:9], refs[9:]
    bias_refs, (o_ref, oc_ref, k_scr, v_scr, kc_scr, vc_scr) = refs[:Q_SPLIT], refs[Q_SPLIT:]
    j = pl.program_id(2)

    @pl.when(j == 0)
    def _():
        kn = _head_rms(k_ref[...], qkg_ref[1])
        k_scr[...] = _rope(kn, cos_ref[...], sin_ref[...]).astype(BF16)
        v_scr[...] = v_ref[...].astype(BF16)
        kcn = _head_rms(kc_ref[...], qkg_ref[1]).astype(BF16)
        vcb = vc_ref[...].astype(BF16)
        kc_scr[...] = kcn
        vc_scr[...] = vcb
        qcn = _head_rms(qc_ref[...], qkg_ref[0]).astype(BF16)
        oc_ref[...] = _softmax_pv([_dot_nt(qcn, kcn) * ATTN_SCALE], [vcb]).astype(oc_ref.dtype)

    for part in range(Q_SPLIT):
        rows = pl.ds(part * PART_Q, PART_Q)
        q0 = pl.multiple_of(j * Q_BLK + part * PART_Q, PART_Q)
        k0 = pl.multiple_of(_part_key_row(j, part) * GRID_W, GRID_W)
        kw = k_scr[pl.ds(k0, K_BLK), :]
        vw = v_scr[pl.ds(k0, K_BLK), :]
        qn = _head_rms(q_ref[rows, :], qkg_ref[0])
        qr = _rope(qn, cos_ref[pl.ds(q0, PART_Q), :], sin_ref[pl.ds(q0, PART_Q), :]).astype(BF16)
        s_win = _dot_nt(qr, kw) * ATTN_SCALE + bias_refs[part][...]
        s_ctx = _dot_nt(qn.astype(BF16), kc_scr[...]) * ATTN_SCALE
        o_ref[rows, :] = _softmax_pv([s_win, s_ctx], [vw, vc_scr[...]]).astype(o_ref.dtype)


BIAS_PARTS = ((0, 0), (N_QBLK // 2, 0), (N_QBLK - 1, Q_SPLIT - 1))


def _bias_pattern(j, part):
    first = (j == 0) & (part == 0)
    last = (j == N_QBLK - 1) & (part == Q_SPLIT - 1)
    return jnp.where(first, 0, jnp.where(last, 2, 1))


def _na_attention(proj, cos_t, sin_t, qkg, bias):
    lat_blk = SEQ // Q_BLK
    ctx_blk0 = N_LAT // CTX_LEN
    hd = NA_HEAD_DIM
    bias_specs = [pl.BlockSpec((None, None, PART_Q, K_BLK), lambda b, h, j, p=p: (h, _bias_pattern(j, p), 0, 0))
                  for p in range(Q_SPLIT)]
    return pl.pallas_call(
        _na_kernel,
        grid=(BATCH, NA_HEADS, N_QBLK),
        in_specs=[
            pl.BlockSpec((Q_BLK, hd), lambda b, h, j: (b * lat_blk + j, COL_Q + h)),
            pl.BlockSpec((SEQ, hd), lambda b, h, j: (b, COL_K + h)),
            pl.BlockSpec((SEQ, hd), lambda b, h, j: (b, COL_V + h)),
            pl.BlockSpec((CTX_LEN, hd), lambda b, h, j: (ctx_blk0 + b, COL_Q + h)),
            pl.BlockSpec((CTX_LEN, hd), lambda b, h, j: (ctx_blk0 + b, COL_K + h)),
            pl.BlockSpec((CTX_LEN, hd), lambda b, h, j: (ctx_blk0 + b, COL_V + h)),
            pl.BlockSpec((SEQ, hd), lambda b, h, j: (0, 0)),
            pl.BlockSpec((SEQ, hd), lambda b, h, j: (0, 0)),
            pl.BlockSpec((2, 1, hd), lambda b, h, j: (0, 0, 0)),
        ] + bias_specs,
        out_specs=[pl.BlockSpec((Q_BLK, hd), lambda b, h, j: (b * lat_blk + j, h)),
                   pl.BlockSpec((CTX_LEN, hd), lambda b, h, j: (b, h))],
        out_shape=[jax.ShapeDtypeStruct((N_LAT, NA_WIDTH), BF16),
                   jax.ShapeDtypeStruct((BATCH * CTX_LEN, NA_WIDTH), BF16)],
        scratch_shapes=[pltpu.VMEM((SEQ, hd), BF16), pltpu.VMEM((SEQ, hd), BF16),
                        pltpu.VMEM((CTX_LEN, hd), BF16), pltpu.VMEM((CTX_LEN, hd), BF16)],
        compiler_params=_cparams(("arbitrary", "arbitrary", "arbitrary")),
        name="na_attention",
    )(proj, proj, proj, proj, proj, proj, cos_t, sin_t, qkg, *([bias] * Q_SPLIT))


def _part_geometry(jb, part):
    rows = Q_ROWS * jb + PART_ROWS * part + np.arange(PART_ROWS)
    k_first = int(np.clip(rows[0] - WIN_R // 2, 0, GRID_ROWS - K_ROWS))
    r0 = np.clip(rows - WIN_R // 2, 0, GRID_ROWS - WIN_R)
    return rows, k_first, r0


def _attention_bias(rpb_j):
    for jb in range(N_QBLK):
        for part in range(Q_SPLIT):
            rows, k_first, r0 = _part_geometry(jb, part)
            pat = 0 if (jb, part) == (0, 0) else 2 if (jb, part) == (N_QBLK - 1, Q_SPLIT - 1) else 1
            rows_p, k_first_p, r0_p = _part_geometry(*BIAS_PARTS[pat])
            assert k_first - rows[0] == k_first_p - rows_p[0] and np.array_equal(r0 - rows, r0_p - rows_p)
    n_dr = 2 * WIN_R - 1
    qcol = np.arange(GRID_W)
    c0 = np.clip(qcol - WIN_C // 2, 0, GRID_W - WIN_C)
    col_ok = (qcol[None, :] >= c0[:, None]) & (qcol[None, :] < c0[:, None] + WIN_C)
    r = rpb_j.astype(F32)
    edge = GRID_W - WIN_C
    ep = jnp.concatenate([jnp.repeat(r[..., :1], edge, axis=-1), r, jnp.repeat(r[..., -1:], edge, axis=-1)],
                         axis=-1)
    t1 = jnp.stack([ep[..., GRID_W - 1 - qc:2 * GRID_W - 1 - qc] for qc in range(GRID_W)], axis=2)
    t1 = jnp.where(jnp.asarray(col_ok)[None, None], t1, -jnp.inf)
    band = t1.transpose(0, 2, 1, 3).reshape(NA_HEADS, GRID_W, n_dr * GRID_W)
    pad_w = K_BLK
    band = jnp.pad(band, ((0, 0), (0, 0), (pad_w, pad_w)), constant_values=-jnp.inf)
    strips, masks = [], []
    for jb, part in BIAS_PARTS:
        rows, k_first, r0 = _part_geometry(jb, part)
        kr = k_first + np.arange(K_ROWS)
        for row, row0 in zip(rows, r0):
            off = pad_w + (k_first - int(row) + WIN_R - 1) * GRID_W
            assert 0 <= off and off + K_BLK <= band.shape[-1]
            strips.append(band[:, :, off:off + K_BLK])
            masks.append(np.repeat((kr >= row0) & (kr < row0 + WIN_R), GRID_W))
    vals = jnp.stack(strips, axis=1)
    mask = np.stack(masks)[None, :, None, :]
    bias = jnp.where(jnp.asarray(mask), vals, -jnp.inf)
    return bias.reshape(NA_HEADS, len(BIAS_PARTS), PART_Q, K_BLK)


def _rope_tables():
    quarter = NA_HEAD_DIM // 4
    inv_freq = ROPE_BASE ** (-jnp.arange(quarter, dtype=F32) / quarter)
    t = jnp.arange(SEQ)
    ang_r = (t // GRID_W).astype(F32)[:, None] * inv_freq[None, :]
    ang_c = (t % GRID_W).astype(F32)[:, None] * inv_freq[None, :]
    cos_t = jnp.concatenate([jnp.cos(ang_r), jnp.cos(ang_r), jnp.cos(ang_c), jnp.cos(ang_c)], axis=-1)
    sin_t = jnp.concatenate([-jnp.sin(ang_r), jnp.sin(ang_r), -jnp.sin(ang_c), jnp.sin(ang_c)], axis=-1)
    return cos_t, sin_t


SUBLANE = 8
CONV_HALO = SUBLANE
CONV_BLK = 256


CONV_CW = 512


def _conv_kernel(*refs):
    nc = CONV_CH // CONV_CW
    x_refs, prev_refs, next_refs = refs[:nc], refs[nc:2 * nc], refs[2 * nc:3 * nc]
    w_ref, b_ref, o_ref, pad_scr = refs[3 * nc:]
    t = pl.program_id(0)
    n_lat_blk = N_LAT // CONV_BLK
    is_ctx = t >= n_lat_blk
    seq_blks = jnp.where(is_ctx, CTX_LEN // CONV_BLK, SEQ // CONV_BLK)
    pos = jnp.where(is_ctx, t - n_lat_blk, t) % seq_blks
    for c in range(nc):
        cs = slice(c * CONV_CW, (c + 1) * CONV_CW)
        pad_scr[0:CONV_HALO, :] = jnp.where(pos == 0, 0.0, prev_refs[c][...])
        pad_scr[CONV_HALO:CONV_HALO + CONV_BLK, :] = x_refs[c][...]
        pad_scr[CONV_HALO + CONV_BLK:2 * CONV_HALO + CONV_BLK, :] = jnp.where(pos == seq_blks - 1, 0.0,
                                                                               next_refs[c][...])
        xp = pad_scr[...]
        acc = jnp.zeros((CONV_BLK, CONV_CW), F32) + b_ref[:, cs]
        for k in range(CONV_K):
            lo = CONV_HALO - CONV_K // 2 + k
            acc = acc + w_ref[k:k + 1, cs] * xp[lo:lo + CONV_BLK, :]
        o_ref[:, cs] = _silu(acc)


def _conv_silu(proj, conv_w8, conv_b2, j_even):
    n_rows = proj.shape[0]
    nc = CONV_CH // CONV_CW
    col0 = COL_XBC * LANE // CONV_CW
    halo_per_blk = CONV_BLK // CONV_HALO
    last_halo = n_rows // CONV_HALO - 1
    x_specs = [pl.BlockSpec((CONV_BLK, CONV_CW), lambda t, c=c: (t, col0 + c)) for c in range(nc)]
    prev_specs = [pl.BlockSpec((CONV_HALO, CONV_CW),
                               lambda t, c=c: (jnp.maximum(t * halo_per_blk - 1, 0), col0 + c)) for c in range(nc)]
    next_specs = [pl.BlockSpec((CONV_HALO, CONV_CW),
                               lambda t, c=c: (jnp.minimum((t + 1) * halo_per_blk, last_halo), col0 + c))
                  for c in range(nc)]
    return pl.pallas_call(
        _conv_kernel,
        grid=(n_rows // CONV_BLK,),
        in_specs=x_specs + prev_specs + next_specs + [
            pl.BlockSpec((None, SUBLANE, CONV_CH), lambda t: (j_even, 0, 0)),
            pl.BlockSpec((None, 1, CONV_CH), lambda t: (j_even, 0, 0)),
        ],
        out_specs=pl.BlockSpec((CONV_BLK, CONV_CH), lambda t: (t, 0)),
        out_shape=jax.ShapeDtypeStruct((n_rows, CONV_CH), F32),
        scratch_shapes=[pltpu.VMEM((CONV_BLK + 2 * CONV_HALO, CONV_CW), F32)],
        compiler_params=_cparams(("arbitrary",)),
        name="conv_silu",
    )(*([proj] * (3 * nc)), conv_w8, conv_b2)


def _softplus(x):
    return jnp.maximum(x, 0.0) + jnp.log1p(jnp.exp(-jnp.abs(x)))


def _dot_select(a, sel):
    sel = sel.astype(BF16)
    hi = a.astype(BF16)
    r1 = a - hi.astype(F32)
    mid = r1.astype(BF16)
    lo = (r1 - mid.astype(F32)).astype(BF16)
    return (jnp.dot(hi, sel, preferred_element_type=F32) + jnp.dot(mid, sel, preferred_element_type=F32)
            + jnp.dot(lo, sel, preferred_element_type=F32))


def _select_dot(sel, a):
    sel = sel.astype(BF16)
    hi = a.astype(BF16)
    r1 = a - hi.astype(F32)
    mid = r1.astype(BF16)
    lo = (r1 - mid.astype(F32)).astype(BF16)
    return (jnp.dot(sel, hi, preferred_element_type=F32) + jnp.dot(sel, mid, preferred_element_type=F32)
            + jnp.dot(sel, lo, preferred_element_type=F32))


def _ssd_direction(xs_ref, b_ref, c_ref, dt_ref, dtb_ref, alog_ref, y_ref, h_scr, lane0, forward):
    q = SSD_CHUNK
    gw = SSD_INNER // SSD_GROUPS
    dt = _softplus(dt_ref[...] + dtb_ref[...])
    a = dt * (-jnp.exp(alog_ref[...]))
    ri = lax.broadcasted_iota(jnp.int32, (q, q), 0)
    ci = lax.broadcasted_iota(jnp.int32, (q, q), 1)
    tri = (ci <= ri) if forward else (ci >= ri)
    a_cum = _select_dot(tri.astype(F32), a)
    a_cum_t = a_cum.T
    er = lax.broadcasted_iota(jnp.int32, (LANE, SSD_INNER), 0)
    ec = lax.broadcasted_iota(jnp.int32, (LANE, SSD_INNER), 1)
    expand = ((er - lane0) == (ec // SSD_HEAD_DIM)).astype(F32)
    dt_e = _dot_select(dt, expand)
    ac_e = _dot_select(a_cum, expand)
    end = q - 1 if forward else 0
    a_end_e = ac_e[end:end + 1, :]
    xdt = xs_ref[...] * dt_e
    in_decay = jnp.exp(ac_e)
    out_decay = jnp.exp(a_end_e - ac_e)
    state_decay = jnp.exp(a_end_e)
    lane = lax.broadcasted_iota(jnp.int32, (q, LANE), 1)
    for g in range(SSD_GROUPS):
        gs = slice(g * gw, (g + 1) * gw)
        bg_t = b_ref[:, g * SSD_STATE:(g + 1) * SSD_STATE].T.astype(BF16)
        cg = c_ref[:, g * SSD_STATE:(g + 1) * SSD_STATE].astype(BF16)
        cb = jnp.dot(cg, bg_t, preferred_element_type=F32)
        h_t = h_scr[:, gs]
        y_inter = jnp.dot(cg, h_t.astype(BF16), preferred_element_type=F32) * in_decay[:, gs]
        for pair in range(gw // LANE):
            cs = slice(g * gw + pair * LANE, g * gw + (pair + 1) * LANE)
            x_pair = xdt[:, cs].astype(BF16)
            res = []
            for sub in range(LANE // SSD_HEAD_DIM):
                hl = lane0 + (g * gw + pair * LANE) // SSD_HEAD_DIM + sub
                seg = a_cum[:, hl:hl + 1] - a_cum_t[hl:hl + 1, :]
                decay = jnp.exp(jnp.where(tri, seg, -jnp.inf))
                res.append(jnp.dot((cb * decay).astype(BF16), x_pair, preferred_element_type=F32))
            y_pair = jnp.where(lane < SSD_HEAD_DIM, res[0], res[1])
            y_ref[:, cs] = y_pair + y_inter[:, pair * LANE:(pair + 1) * LANE]
        x_out = (xdt[:, gs] * out_decay[:, gs]).astype(BF16)
        h_scr[:, gs] = state_decay[:, gs] * h_t + jnp.dot(bg_t, x_out, preferred_element_type=F32)


def _ssd_kernel(xs_f, b_f, c_f, dt_f, xs_b, b_b, c_b, dt_b, dtb_ref, alog_ref, yf_ref, yb_ref,
                hf_scr, hb_scr):
    @pl.when(pl.program_id(1) == 0)
    def _():
        hf_scr[...] = jnp.zeros_like(hf_scr)
        hb_scr[...] = jnp.zeros_like(hb_scr)

    _ssd_direction(xs_f, b_f, c_f, dt_f, dtb_ref, alog_ref, yf_ref, hf_scr, 0, True)
    _ssd_direction(xs_b, b_b, c_b, dt_b, dtb_ref, alog_ref, yb_ref, hb_scr, SSD_HEADS, False)


def _ssd_scan(act, proj, dtb, alog, j_even):
    q = SSD_CHUNK
    n_rows = act.shape[0]
    n_ctx = CTX_LEN // q
    n_lat = SEQ // q
    ctx_blk0 = N_LAT // q

    def fwd_blk(b, s):
        return jnp.where(s < n_ctx, ctx_blk0 + b * n_ctx + s, b * n_lat + (s - n_ctx))

    def bwd_blk(b, s):
        return jnp.where(s < n_ctx, ctx_blk0 + b * n_ctx + (n_ctx - 1 - s),
                         b * n_lat + (n_lat - 1 - (s - n_ctx)))

    gn = SSD_GROUPS * SSD_STATE
    b_col = SSD_INNER // gn
    c_col = b_col + 1

    def specs(blk):
        return [
            pl.BlockSpec((q, SSD_INNER), lambda b, s: (blk(b, s), 0)),
            pl.BlockSpec((q, gn), lambda b, s: (blk(b, s), b_col)),
            pl.BlockSpec((q, gn), lambda b, s: (blk(b, s), c_col)),
            pl.BlockSpec((q, LANE), lambda b, s: (blk(b, s), COL_DT)),
        ]

    small = pl.BlockSpec((None, 1, LANE), lambda b, s: (j_even, 0, 0))
    return pl.pallas_call(
        _ssd_kernel,
        grid=(BATCH, n_ctx + n_lat),
        in_specs=specs(fwd_blk) + specs(bwd_blk) + [small, small],
        out_specs=[pl.BlockSpec((q, SSD_INNER), lambda b, s: (fwd_blk(b, s), 0)),
                   pl.BlockSpec((q, SSD_INNER), lambda b, s: (bwd_blk(b, s), 0))],
        out_shape=[jax.ShapeDtypeStruct((n_rows, SSD_INNER), F32)] * 2,
        scratch_shapes=[pltpu.VMEM((SSD_STATE, SSD_INNER), F32)] * 2,
        compiler_params=_cparams(("arbitrary", "arbitrary")),
        name="ssd_scan",
    )(act, act, act, proj, act, act, act, proj, dtb, alog)


def _out_proj_kernel(attn_ref, yf_ref, yb_ref, xs_ref, z0_ref, z1_ref, dsk_ref, ng_ref, w_ref, x_ref, mod_ref,
                     o_ref):
    mix = jnp.dot(attn_ref[...], w_ref[0:NA_WIDTH, :], preferred_element_type=F32)
    gw = SSD_INNER // SSD_GROUPS
    for g, z_ref in enumerate((z0_ref, z1_ref)):
        gs = slice(g * gw, (g + 1) * gw)
        y = yf_ref[:, gs] + yb_ref[:, gs] + dsk_ref[:, gs] * xs_ref[:, gs]
        seg = y * _silu(z_ref[...])
        ms = jnp.mean(seg * seg, axis=-1, keepdims=True)
        yn = (seg * lax.rsqrt(ms + EPS) * ng_ref[:, g * gw:(g + 1) * gw]).astype(BF16)
        mix = mix + jnp.dot(yn, w_ref[NA_WIDTH + g * gw:NA_WIDTH + (g + 1) * gw, :],
                            preferred_element_type=F32)
    o_ref[...] = x_ref[...] + mod_ref[2] * mix


def _out_proj(attn, y_f, y_b, act, proj, dsk, ng, w_out, x, n_rows, mods_i, j_even):
    tm = 512
    mrow = _mod_row(tm)
    gw = SSD_INNER // SSD_GROUPS
    z_col = COL_Z * LANE // gw
    assert SSD_GROUPS == 2
    return pl.pallas_call(
        _out_proj_kernel,
        grid=(n_rows // tm,),
        in_specs=[
            pl.BlockSpec((tm, NA_WIDTH), lambda t: (t, 0)),
            pl.BlockSpec((tm, SSD_INNER), lambda t: (t, 0)),
            pl.BlockSpec((tm, SSD_INNER), lambda t: (t, 0)),
            pl.BlockSpec((tm, SSD_INNER), lambda t: (t, 0)),
            pl.BlockSpec((tm, gw), lambda t: (t, z_col)),
            pl.BlockSpec((tm, gw), lambda t: (t, z_col + 1)),
            pl.BlockSpec((None, 1, SSD_INNER), lambda t: (j_even, 0, 0)),
            pl.BlockSpec((None, 1, SSD_INNER), lambda t: (j_even, 0, 0)),
            pl.BlockSpec((None, D_MODEL, D_MODEL), lambda t: (j_even, 0, 0)),
            pl.BlockSpec((tm, D_MODEL), lambda t: (t, 0)),
            pl.BlockSpec((None, 3, 1, D_MODEL), lambda t: (mrow(t), 1, 0, 0)),
        ],
        out_specs=pl.BlockSpec((tm, D_MODEL), lambda t: (t, 0)),
        out_shape=jax.ShapeDtypeStruct((n_rows, D_MODEL), F32),
        compiler_params=_cparams(("arbitrary",)),
        name="out_proj",
    )(attn, y_f, y_b, act, proj, proj, dsk, ng, w_out, x, mods_i)


FOURIER_TILE = 512


def _fourier_chan_kernel(x_ref, mod_ref, g_ref, cs_ref, pq_ref):
    h = _mod_norm(x_ref[...], g_ref[...], mod_ref[0], mod_ref[1])
    for g in range(F_GROUPS):
        gs = slice(g * F_GROUP_CH, (g + 1) * F_GROUP_CH)
        pq = jnp.dot(h[:, gs].astype(BF16), cs_ref[...], preferred_element_type=F32)
        pq_ref[0, :, gs] = pq[:, :F_GROUP_CH].astype(BF16)
        pq_ref[1, :, gs] = pq[:, F_GROUP_CH:].astype(BF16)


def _fourier_chan(x, n_rows, mods_i, g, cs):
    tm = FOURIER_TILE
    mrow = _mod_row(tm)
    return pl.pallas_call(
        _fourier_chan_kernel,
        grid=(n_rows // tm,),
        in_specs=[
            pl.BlockSpec((tm, D_MODEL), lambda t: (t, 0)),
            pl.BlockSpec((None, 3, 1, D_MODEL), lambda t: (mrow(t), 1, 0, 0)),
            pl.BlockSpec((1, D_MODEL), lambda t: (0, 0)),
            pl.BlockSpec((F_GROUP_CH, 2 * F_GROUP_CH), lambda t: (0, 0)),
        ],
        out_specs=pl.BlockSpec((None, 2, tm, D_MODEL), lambda t: (t, 0, 0, 0)),
        out_shape=jax.ShapeDtypeStruct((n_rows // tm, 2, tm, D_MODEL), BF16),
        compiler_params=_cparams(("arbitrary",)),
        name="fourier_chan",
    )(x, mods_i, g, cs)


DFT_SPLIT = 64


FOURIER_POS_SUB = 2


def _fourier_pos_kernel(*refs, scale, n_sub):
    u_ref, v_ref = refs[:2]
    pq_refs = refs[2:2 + n_sub]
    w_ref, x_ref, mod_ref, o_ref, acc_scr, cs_scr = refs[2 + n_sub:]
    k = pl.program_id(2)
    tb = v_ref.shape[-1] // n_sub

    @pl.when(k == 0)
    def _():
        acc_scr[...] = jnp.zeros_like(acc_scr)

    part = None
    for i, pq_ref in enumerate(pq_refs):
        ls = slice(i * tb, (i + 1) * tb)
        vr, vi = v_ref[0, :, ls], v_ref[1, :, ls]
        for a in range(u_ref.shape[1]):
            ur, ui = u_ref[0, a:a + 1, ls], u_ref[1, a:a + 1, ls]
            rows = slice(a * DFT_SPLIT, (a + 1) * DFT_SPLIT)
            cs_scr[i, rows, 0:tb] = (ur * vr - ui * vi).astype(BF16)
            cs_scr[i, rows, tb:2 * tb] = (-(ui * vr + ur * vi)).astype(BF16)
        pq = pq_ref[...].reshape(2 * tb, pq_ref.shape[-1])
        d = jnp.dot(cs_scr[i], pq, preferred_element_type=F32)
        part = d if part is None else part + d
    acc_scr[...] += part

    @pl.when(k == pl.num_programs(2) - 1)
    def _():
        f = (acc_scr[...] * scale).astype(BF16)
        o_ref[...] = x_ref[...] + mod_ref[2] * jnp.dot(f, w_ref[...], preferred_element_type=F32)


def _fourier_pos(tabs, pq, w, x, mods_i, j_odd, seq_len, row0, is_ctx):
    u_tab, v_tab = tabs
    tm = tb = min(seq_len, FOURIER_TILE)
    n_sub = min(FOURIER_POS_SUB, seq_len // tb)
    tk = n_sub * tb
    nm = seq_len // tm
    blk0 = row0 // tb
    per_tile = FOURIER_TILE // tb
    scale = float((seq_len * F_GROUP_CH) ** -0.5)

    def pq_spec(i):
        def index(b, m, k):
            blk = blk0 + b * nm + k * n_sub + i
            return blk // per_tile, 0, blk % per_tile, 0
        return pl.BlockSpec((None, 2, tb, D_MODEL), index)

    return pl.pallas_call(
        functools.partial(_fourier_pos_kernel, scale=scale, n_sub=n_sub),
        grid=(BATCH, nm, seq_len // tk),
        in_specs=[
            pl.BlockSpec((2, tm // DFT_SPLIT, tk), lambda b, m, k: (0, m, k)),
            pl.BlockSpec((2, DFT_SPLIT, tk), lambda b, m, k: (0, 0, k)),
            *[pq_spec(i) for i in range(n_sub)],
            pl.BlockSpec((None, D_MODEL, D_MODEL), lambda b, m, k: (j_odd, 0, 0)),
            pl.BlockSpec((tm, D_MODEL), lambda b, m, k: (blk0 + b * nm + m, 0)),
            pl.BlockSpec((None, 3, 1, D_MODEL), lambda b, m, k: (BATCH if is_ctx else b, 1, 0, 0)),
        ],
        out_specs=pl.BlockSpec((tm, D_MODEL), lambda b, m, k: (b * nm + m, 0)),
        out_shape=jax.ShapeDtypeStruct((BATCH * seq_len, D_MODEL), F32),
        scratch_shapes=[pltpu.VMEM((tm, D_MODEL), F32), pltpu.VMEM((n_sub, tm, 2 * tb), BF16)],
        compiler_params=_cparams(("arbitrary", "arbitrary", "arbitrary")),
        name="fourier_pos",
    )(u_tab, v_tab, *([pq] * n_sub), w, x, mods_i)


def _phase_table(freq, n):
    l = jnp.arange(n, dtype=jnp.int32)
    ang = ((freq[:, None] * l[None, :]) % n).astype(F32) * np.float32(2.0 * np.pi / n)
    return jnp.stack([jnp.cos(ang), jnp.sin(ang)])


def _dft_tables(n):
    a = jnp.arange(n // DFT_SPLIT, dtype=jnp.int32) * DFT_SPLIT
    b = jnp.arange(DFT_SPLIT, dtype=jnp.int32)
    return _phase_table(a, n), _phase_table(b, n)


def kernel(x, c, ctx, c_ctx, mod_w, mod_b, norm_g, ffn_w1, ffn_w3, ffn_w2, mix_w_in, mix_w_out, qk_g, rpb,
           conv_w, conv_b, dt_bias, a_log, ssd_d, ssd_norm_g, fourier_w):
    n_even = mix_w_in.shape[0]
    xs = jnp.concatenate([x.reshape(N_LAT, D_MODEL), ctx.reshape(BATCH * CTX_LEN, D_MODEL)], axis=0)
    mods = _modulation(c, c_ctx, mod_w, mod_b)

    w1 = ffn_w1.astype(BF16)
    w3 = ffn_w3.astype(BF16)
    w2 = ffn_w2.astype(BF16)
    w_in = jnp.pad(mix_w_in, ((0, 0), (0, 0), (0, IN_W_PAD - IN_W))).astype(BF16)
    w_out = mix_w_out.astype(BF16)
    w_f = fourier_w.astype(BF16)
    qkg = qk_g.reshape(n_even, 2, 1, NA_HEAD_DIM)
    conv_w8 = jnp.pad(conv_w, ((0, 0), (0, SUBLANE - CONV_K), (0, 0)))
    conv_b2 = conv_b.reshape(n_even, 1, CONV_CH)
    pad_l = LANE - 2 * SSD_HEADS
    dtb = jnp.pad(dt_bias.reshape(n_even, 1, 2 * SSD_HEADS), ((0, 0), (0, 0), (0, pad_l)))
    alog = jnp.pad(a_log.reshape(n_even, 1, 2 * SSD_HEADS), ((0, 0), (0, 0), (0, pad_l)))
    dsk = jnp.repeat(ssd_d, SSD_HEAD_DIM, axis=-1).reshape(n_even, 1, SSD_INNER)
    ng = ssd_norm_g.reshape(n_even, 1, SSD_INNER)
    cos_t, sin_t = _rope_tables()
    chan = _phase_table(jnp.arange(F_GROUP_CH, dtype=jnp.int32), F_GROUP_CH)
    cs_chan = jnp.concatenate([chan[0], chan[1]], axis=1).astype(BF16)
    dft_lat = _dft_tables(SEQ)
    dft_ctx = _dft_tables(CTX_LEN)

    last_ctx = ((DEPTH - 1) // 2) * 2
    for i in range(DEPTH):
        use_ctx = i <= last_ctx
        ctx_out = i < last_ctx
        j = i // 2
        mods_i = mods[i]
        g = norm_g[i].reshape(3, 1, D_MODEL)
        rows_in = N_TOK if use_ctx else N_LAT
        rows_out = N_TOK if ctx_out else N_LAT
        xs = _ffn(xs, rows_in, mods_i, 0, g[0], w1, w3, w2, i, 0)
        if i % 2 == 0:
            assert use_ctx, "even mixer layers read the context keys and SSD states"
            proj = _in_proj(xs, rows_in, mods_i, g[1], w_in, j)
            bias = _attention_bias(rpb[j])
            attn, attn_ctx = _na_attention(proj, cos_t, sin_t, qkg[j], bias)
            if ctx_out:
                attn = jnp.concatenate([attn, attn_ctx], axis=0)
            act = _conv_silu(proj, conv_w8, conv_b2, j)
            y_f, y_b = _ssd_scan(act, proj, dtb, alog, j)
            xs = _out_proj(attn, y_f, y_b, act, proj, dsk, ng, w_out, xs, rows_out, mods_i, j)
        else:
            pq = _fourier_chan(xs, rows_out, mods_i, g[1], cs_chan)
            new = _fourier_pos(dft_lat, pq, w_f, xs, mods_i, j, SEQ, 0, False)
            if ctx_out:
                new_ctx = _fourier_pos(dft_ctx, pq, w_f, xs, mods_i, j, CTX_LEN, N_LAT, True)
                new = jnp.concatenate([new, new_ctx], axis=0)
            xs = new
        xs = _ffn(xs, rows_out, mods_i, 2, g[2], w1, w3, w2, i, 1)
    return xs[:N_LAT].reshape(BATCH, SEQ, D_MODEL)
```

```python
import functools

import numpy as np
import jax
import jax.numpy as jnp
from jax import lax
from jax.experimental import pallas as pl
from jax.experimental.pallas import tpu as pltpu

F32 = jnp.float32
BF16 = jnp.bfloat16

D_MODEL = 2048
BATCH = 2
SEQ = 4096
DEPTH = 4
GRID_W = 64
GRID_ROWS = SEQ // GRID_W
CTX_LEN = 256
N_LAT = BATCH * SEQ
N_TOK = N_LAT + BATCH * CTX_LEN
NA_HEADS = 8
NA_HEAD_DIM = 128
NA_WIDTH = NA_HEADS * NA_HEAD_DIM
WIN_R = 8
WIN_C = 16
ROPE_BASE = 10000.0
SSD_HEADS = 16
SSD_HEAD_DIM = 64
SSD_INNER = SSD_HEADS * SSD_HEAD_DIM
SSD_GROUPS = 2
SSD_STATE = 128
SSD_CHUNK = 128
CONV_K = 5
CONV_CH = SSD_INNER + 2 * SSD_GROUPS * SSD_STATE
IN_W = 3 * NA_WIDTH + SSD_INNER + CONV_CH + 2 * SSD_HEADS
LANE = 128
SUBLANE = 8
IN_W_PAD = ((IN_W + LANE - 1) // LANE) * LANE
F_GROUPS = 4
F_GROUP_CH = D_MODEL // F_GROUPS
D_FF = 5632
N_MOD = 9
EPS = 1e-6
ATTN_SCALE = NA_HEAD_DIM ** -0.5

COL_Q = 0
COL_K = NA_WIDTH // LANE
COL_V = 2 * NA_WIDTH // LANE
COL_Z = 3 * NA_WIDTH // LANE
COL_XBC = (3 * NA_WIDTH + SSD_INNER) // LANE
COL_DT = (3 * NA_WIDTH + SSD_INNER + CONV_CH) // LANE

Q_ROWS = 32
Q_BLK = Q_ROWS * GRID_W
Q_SPLIT = 8
PART_ROWS = Q_ROWS // Q_SPLIT
PART_Q = PART_ROWS * GRID_W
K_ROWS = 12
K_BLK = K_ROWS * GRID_W
N_QBLK = GRID_ROWS // Q_ROWS

VMEM_LIMIT = 56 * 1024 * 1024


def _cparams(sem, vmem_limit=VMEM_LIMIT):
    return pltpu.CompilerParams(dimension_semantics=sem, vmem_limit_bytes=vmem_limit)


def _sigmoid(x):
    return 1.0 / (1.0 + jnp.exp(-x))


def _silu(x):
    return x * _sigmoid(x)


def _mod_norm(x, g, shift, scale):
    ms = jnp.mean(x * x, axis=-1, keepdims=True)
    y = x * lax.rsqrt(ms + EPS) * g
    return y * (1.0 + scale) + shift


def _mod_row(tile_rows):
    def f(t):
        return jnp.minimum((t * tile_rows) // SEQ, BATCH)
    return f


def _mod_kernel(c_ref, w_ref, b_ref, o_ref):
    s = _silu(c_ref[...]).astype(BF16)
    o_ref[...] = jnp.dot(s, w_ref[...].astype(BF16), preferred_element_type=F32) + b_ref[...]


def _modulation(c, c_ctx, mod_w, mod_b):
    rows = SUBLANE
    cvec = jnp.concatenate([c, c_ctx[None, :], jnp.zeros((rows - BATCH - 1, D_MODEL), F32)], axis=0)
    n = N_MOD * D_MODEL
    tn = 2048
    out = pl.pallas_call(
        _mod_kernel,
        grid=(DEPTH, n // tn),
        in_specs=[
            pl.BlockSpec((rows, D_MODEL), lambda i, j: (0, 0)),
            pl.BlockSpec((None, D_MODEL, tn), lambda i, j: (i, 0, j)),
            pl.BlockSpec((None, 1, tn), lambda i, j: (i, 0, j)),
        ],
        out_specs=pl.BlockSpec((None, rows, tn), lambda i, j: (i, 0, j)),
        out_shape=jax.ShapeDtypeStruct((DEPTH, rows, n), F32),
        compiler_params=_cparams(("arbitrary", "arbitrary")),
        name="modulation",
    )(cvec, mod_w, mod_b.reshape(DEPTH, 1, n))
    return out[:, :BATCH + 1].reshape(DEPTH, BATCH + 1, N_MOD, 1, D_MODEL)


FFN_TM = 1024
FFN_TF = 512
FFN_VMEM_LIMIT = 63 * 1024 * 1024


def _ffn_accumulate(h_scr, w1_ref, w3_ref, w2_ref, o_ref, rows):
    h = h_scr[0:rows, :]
    a = jnp.dot(h, w1_ref[...], preferred_element_type=F32)
    b = jnp.dot(h, w3_ref[...], preferred_element_type=F32)
    u = (_silu(a) * b).astype(BF16)
    o_ref[0:rows, :] += jnp.dot(u, w2_ref[...], preferred_element_type=F32)


def _ffn_kernel(x_ref, mod_ref, g_ref, w1_ref, w3_ref, w2_ref, o_ref, h_scr, *, tail_rows):
    t = pl.program_id(0)
    j = pl.program_id(1)
    last_t = pl.num_programs(0) - 1

    @pl.when(j == 0)
    def _():
        h_scr[...] = _mod_norm(x_ref[...], g_ref[...], mod_ref[0], mod_ref[1]).astype(BF16)
        o_ref[...] = jnp.zeros_like(o_ref)

    if tail_rows == FFN_TM:
        _ffn_accumulate(h_scr, w1_ref, w3_ref, w2_ref, o_ref, FFN_TM)
    else:
        @pl.when(t < last_t)
        def _():
            _ffn_accumulate(h_scr, w1_ref, w3_ref, w2_ref, o_ref, FFN_TM)

        @pl.when(t == last_t)
        def _():
            _ffn_accumulate(h_scr, w1_ref, w3_ref, w2_ref, o_ref, tail_rows)

    @pl.when(j == pl.num_programs(1) - 1)
    def _():
        o_ref[...] = x_ref[...] + 0.5 * mod_ref[2] * o_ref[...]


def _ffn(x, n_rows, mods_i, sub, g, w1, w3, w2, layer, which):
    tm, tf = FFN_TM, FFN_TF
    mrow = _mod_row(tm)
    nt = pl.cdiv(n_rows, tm)
    tail_rows = n_rows - (nt - 1) * tm
    return pl.pallas_call(
        functools.partial(_ffn_kernel, tail_rows=tail_rows),
        grid=(nt, D_FF // tf),
        in_specs=[
            pl.BlockSpec((tm, D_MODEL), lambda t, j: (t, 0)),
            pl.BlockSpec((None, 3, 1, D_MODEL), lambda t, j: (mrow(t), sub, 0, 0)),
            pl.BlockSpec((1, D_MODEL), lambda t, j: (0, 0)),
            pl.BlockSpec((None, None, D_MODEL, tf), lambda t, j: (layer, which, 0, j)),
            pl.BlockSpec((None, None, D_MODEL, tf), lambda t, j: (layer, which, 0, j)),
            pl.BlockSpec((None, None, tf, D_MODEL), lambda t, j: (layer, which, j, 0)),
        ],
        out_specs=pl.BlockSpec((tm, D_MODEL), lambda t, j: (t, 0)),
        out_shape=jax.ShapeDtypeStruct((n_rows, D_MODEL), F32),
        scratch_shapes=[pltpu.VMEM((tm, D_MODEL), BF16)],
        compiler_params=_cparams(("arbitrary", "arbitrary"), FFN_VMEM_LIMIT),
        name="ffn",
    )(x, mods_i, g, w1, w3, w2)


def _norm_matmul_kernel(x_ref, mod_ref, g_ref, w_ref, o_ref, h_scr):
    @pl.when(pl.program_id(1) == 0)
    def _():
        h_scr[...] = _mod_norm(x_ref[...], g_ref[...], mod_ref[0], mod_ref[1]).astype(BF16)

    o_ref[...] = jnp.dot(h_scr[...], w_ref[...], preferred_element_type=F32)


def _in_proj(x, n_rows, mods_i, g, w_in, j_even):
    tm, tn = 512, 1920
    mrow = _mod_row(tm)
    return pl.pallas_call(
        _norm_matmul_kernel,
        grid=(n_rows // tm, IN_W_PAD // tn),
        in_specs=[
            pl.BlockSpec((tm, D_MODEL), lambda t, j: (t, 0)),
            pl.BlockSpec((None, 3, 1, D_MODEL), lambda t, j: (mrow(t), 1, 0, 0)),
            pl.BlockSpec((1, D_MODEL), lambda t, j: (0, 0)),
            pl.BlockSpec((None, D_MODEL, tn), lambda t, j: (j_even, 0, j)),
        ],
        out_specs=pl.BlockSpec((tm, tn), lambda t, j: (t, j)),
        out_shape=jax.ShapeDtypeStruct((n_rows, IN_W_PAD), F32),
        scratch_shapes=[pltpu.VMEM((tm, D_MODEL), BF16)],
        compiler_params=_cparams(("arbitrary", "arbitrary")),
        name="in_proj",
    )(x, mods_i, g, w_in)


def _head_rms(x, g):
    ms = jnp.mean(x * x, axis=-1, keepdims=True)
    return x * lax.rsqrt(ms + EPS) * g


def _rope(x, cos, sin):
    lane = lax.broadcasted_iota(jnp.int32, x.shape, 1)
    first_half = (lane & (NA_HEAD_DIM // 4)) == 0
    partner = jnp.where(first_half, pltpu.roll(x, NA_HEAD_DIM - NA_HEAD_DIM // 4, 1),
                        pltpu.roll(x, NA_HEAD_DIM // 4, 1))
    return x * cos + partner * sin


def _dot_nt(a, b):
    return lax.dot_general(a, b, (((1,), (1,)), ((), ())), preferred_element_type=F32)


def _softmax_pv(scores, values):
    m = functools.reduce(jnp.maximum, [jnp.max(s, axis=-1, keepdims=True) for s in scores])
    ps = [jnp.exp(s - m) for s in scores]
    denom = functools.reduce(jnp.add, [jnp.sum(p, axis=-1, keepdims=True) for p in ps])
    o = functools.reduce(jnp.add, [jnp.dot(p.astype(BF16), v, preferred_element_type=F32)
                                   for p, v in zip(ps, values)])
    return o / denom


def _part_key_row(j, part):
    return jnp.clip(Q_ROWS * j + PART_ROWS * part - WIN_R // 2, 0, GRID_ROWS - K_ROWS)


def _na_kernel(*refs):
    (q_ref, k_ref, v_ref, qc_ref, kc_ref, vc_ref, cos_ref, sin_ref, qkg_ref), refs = refs[:9], refs[9:]
    bias_refs, (o_ref, oc_ref, k_scr, v_scr, kc_scr, vc_scr) = refs[:Q_SPLIT], refs[Q_SPLIT:]
    j = pl.program_id(2)

    @pl.when(j == 0)
    def _():
        kn = _head_rms(k_ref[...], qkg_ref[1])
        k_scr[...] = _rope(kn, cos_ref[...], sin_ref[...]).astype(BF16)
        v_scr[...] = v_ref[...].astype(BF16)
        kcn = _head_rms(kc_ref[...], qkg_ref[1]).astype(BF16)
        vcb = vc_ref[...].astype(BF16)
        kc_scr[...] = kcn
        vc_scr[...] = vcb
        qcn = _head_rms(qc_ref[...], qkg_ref[0]).astype(BF16)
        oc_ref[...] = _softmax_pv([_dot_nt(qcn, kcn) * ATTN_SCALE], [vcb]).astype(oc_ref.dtype)

    for part in range(Q_SPLIT):
        rows = pl.ds(part * PART_Q, PART_Q)
        q0 = pl.multiple_of(j * Q_BLK + part * PART_Q, PART_Q)
        k0 = pl.multiple_of(_part_key_row(j, part) * GRID_W, GRID_W)
        kw = k_scr[pl.ds(k0, K_BLK), :]
        vw = v_scr[pl.ds(k0, K_BLK), :]
        qn = _head_rms(q_ref[rows, :], qkg_ref[0])
        qr = _rope(qn, cos_ref[pl.ds(q0, PART_Q), :], sin_ref[pl.ds(q0, PART_Q), :]).astype(BF16)
        s_win = _dot_nt(qr, kw) * ATTN_SCALE + bias_refs[part][...]
        s_ctx = _dot_nt(qn.astype(BF16), kc_scr[...]) * ATTN_SCALE
        o_ref[rows, :] = _softmax_pv([s_win, s_ctx], [vw, vc_scr[...]]).astype(o_ref.dtype)


BIAS_PARTS = ((0, 0), (N_QBLK // 2, 0), (N_QBLK - 1, Q_SPLIT - 1))


def _bias_pattern(j, part):
    first = (j == 0) & (part == 0)
    last = (j == N_QBLK - 1) & (part == Q_SPLIT - 1)
    return jnp.where(first, 0, jnp.where(last, 2, 1))


def _na_attention(proj, cos_t, sin_t, qkg, bias):
    lat_blk = SEQ // Q_BLK
    ctx_blk0 = N_LAT // CTX_LEN
    hd = NA_HEAD_DIM
    bias_specs = [pl.BlockSpec((None, None, PART_Q, K_BLK), lambda b, h, j, p=p: (h, _bias_pattern(j, p), 0, 0))
                  for p in range(Q_SPLIT)]
    return pl.pallas_call(
        _na_kernel,
        grid=(BATCH, NA_HEADS, N_QBLK),
        in_specs=[
            pl.BlockSpec((Q_BLK, hd), lambda b, h, j: (b * lat_blk + j, COL_Q + h)),
            pl.BlockSpec((SEQ, hd), lambda b, h, j: (b, COL_K + h)),
            pl.BlockSpec((SEQ, hd), lambda b, h, j: (b, COL_V + h)),
            pl.BlockSpec((CTX_LEN, hd), lambda b, h, j: (ctx_blk0 + b, COL_Q + h)),
            pl.BlockSpec((CTX_LEN, hd), lambda b, h, j: (ctx_blk0 + b, COL_K + h)),
            pl.BlockSpec((CTX_LEN, hd), lambda b, h, j: (ctx_blk0 + b, COL_V + h)),
            pl.BlockSpec((SEQ, hd), lambda b, h, j: (0, 0)),
            pl.BlockSpec((SEQ, hd), lambda b, h, j: (0, 0)),
            pl.BlockSpec((2, 1, hd), lambda b, h, j: (0, 0, 0)),
        ] + bias_specs,
        out_specs=[pl.BlockSpec((Q_BLK, hd), lambda b, h, j: (b * lat_blk + j, h)),
                   pl.BlockSpec((CTX_LEN, hd), lambda b, h, j: (b, h))],
        out_shape=[jax.ShapeDtypeStruct((N_LAT, NA_WIDTH), BF16),
                   jax.ShapeDtypeStruct((BATCH * CTX_LEN, NA_WIDTH), BF16)],
        scratch_shapes=[pltpu.VMEM((SEQ, hd), BF16), pltpu.VMEM((SEQ, hd), BF16),
                        pltpu.VMEM((CTX_LEN, hd), BF16), pltpu.VMEM((CTX_LEN, hd), BF16)],
        compiler_params=_cparams(("arbitrary", "arbitrary", "arbitrary")),
        name="na_attention",
    )(proj, proj, proj, proj, proj, proj, cos_t, sin_t, qkg, *([bias] * Q_SPLIT))


def _part_geometry(jb, part):
    rows = Q_ROWS * jb + PART_ROWS * part + np.arange(PART_ROWS)
    k_first = int(np.clip(rows[0] - WIN_R // 2, 0, GRID_ROWS - K_ROWS))
    r0 = np.clip(rows - WIN_R // 2, 0, GRID_ROWS - WIN_R)
    return rows, k_first, r0


def _attention_bias(rpb_j):
    for jb in range(N_QBLK):
        for part in range(Q_SPLIT):
            rows, k_first, r0 = _part_geometry(jb, part)
            pat = 0 if (jb, part) == (0, 0) else 2 if (jb, part) == (N_QBLK - 1, Q_SPLIT - 1) else 1
            rows_p, k_first_p, r0_p = _part_geometry(*BIAS_PARTS[pat])
            assert k_first - rows[0] == k_first_p - rows_p[0] and np.array_equal(r0 - rows, r0_p - rows_p)
    n_dr = 2 * WIN_R - 1
    qcol = np.arange(GRID_W)
    c0 = np.clip(qcol - WIN_C // 2, 0, GRID_W - WIN_C)
    col_ok = (qcol[None, :] >= c0[:, None]) & (qcol[None, :] < c0[:, None] + WIN_C)
    r = rpb_j.astype(F32)
    edge = GRID_W - WIN_C
    ep = jnp.concatenate([jnp.repeat(r[..., :1], edge, axis=-1), r, jnp.repeat(r[..., -1:], edge, axis=-1)],
                         axis=-1)
    t1 = jnp.stack([ep[..., GRID_W - 1 - qc:2 * GRID_W - 1 - qc] for qc in range(GRID_W)], axis=2)
    t1 = jnp.where(jnp.asarray(col_ok)[None, None], t1, -jnp.inf)
    band = t1.transpose(0, 2, 1, 3).reshape(NA_HEADS, GRID_W, n_dr * GRID_W)
    pad_w = K_BLK
    band = jnp.pad(band, ((0, 0), (0, 0), (pad_w, pad_w)), constant_values=-jnp.inf)
    strips, masks = [], []
    for jb, part in BIAS_PARTS:
        rows, k_first, r0 = _part_geometry(jb, part)
        kr = k_first + np.arange(K_ROWS)
        for row, row0 in zip(rows, r0):
            off = pad_w + (k_first - int(row) + WIN_R - 1) * GRID_W
            assert 0 <= off and off + K_BLK <= band.shape[-1]
            strips.append(band[:, :, off:off + K_BLK])
            masks.append(np.repeat((kr >= row0) & (kr < row0 + WIN_R), GRID_W))
    vals = jnp.stack(strips, axis=1)
    mask = np.stack(masks)[None, :, None, :]
    bias = jnp.where(jnp.asarray(mask), vals, -jnp.inf)
    return bias.reshape(NA_HEADS, len(BIAS_PARTS), PART_Q, K_BLK)


def _rope_tables():
    quarter = NA_HEAD_DIM // 4
    inv_freq = ROPE_BASE ** (-jnp.arange(quarter, dtype=F32) / quarter)
    t = jnp.arange(SEQ)
    ang_r = (t // GRID_W).astype(F32)[:, None] * inv_freq[None, :]
    ang_c = (t % GRID_W).astype(F32)[:, None] * inv_freq[None, :]
    cos_t = jnp.concatenate([jnp.cos(ang_r), jnp.cos(ang_r), jnp.cos(ang_c), jnp.cos(ang_c)], axis=-1)
    sin_t = jnp.concatenate([-jnp.sin(ang_r), jnp.sin(ang_r), -jnp.sin(ang_c), jnp.sin(ang_c)], axis=-1)
    return cos_t, sin_t


CONV_HALO = SUBLANE
CONV_BLK = 256


CONV_CW = 512


def _conv_kernel(*refs):
    nc = CONV_CH // CONV_CW
    x_refs, prev_refs, next_refs = refs[:nc], refs[nc:2 * nc], refs[2 * nc:3 * nc]
    w_ref, b_ref, o_ref, pad_scr = refs[3 * nc:]
    t = pl.program_id(0)
    n_lat_blk = N_LAT // CONV_BLK
    is_ctx = t >= n_lat_blk
    seq_blks = jnp.where(is_ctx, CTX_LEN // CONV_BLK, SEQ // CONV_BLK)
    pos = jnp.where(is_ctx, t - n_lat_blk, t) % seq_blks
    for c in range(nc):
        cs = slice(c * CONV_CW, (c + 1) * CONV_CW)
        pad_scr[0:CONV_HALO, :] = jnp.where(pos == 0, 0.0, prev_refs[c][...])
        pad_scr[CONV_HALO:CONV_HALO + CONV_BLK, :] = x_refs[c][...]
        pad_scr[CONV_HALO + CONV_BLK:2 * CONV_HALO + CONV_BLK, :] = jnp.where(pos == seq_blks - 1, 0.0,
                                                                               next_refs[c][...])
        xp = pad_scr[...]
        acc = jnp.zeros((CONV_BLK, CONV_CW), F32) + b_ref[:, cs]
        for k in range(CONV_K):
            lo = CONV_HALO - CONV_K // 2 + k
            acc = acc + w_ref[k:k + 1, cs] * xp[lo:lo + CONV_BLK, :]
        o_ref[:, cs] = _silu(acc)


def _conv_silu(proj, conv_w8, conv_b2, j_even):
    n_rows = proj.shape[0]
    nc = CONV_CH // CONV_CW
    col0 = COL_XBC * LANE // CONV_CW
    halo_per_blk = CONV_BLK // CONV_HALO
    last_halo = n_rows // CONV_HALO - 1
    x_specs = [pl.BlockSpec((CONV_BLK, CONV_CW), lambda t, c=c: (t, col0 + c)) for c in range(nc)]
    prev_specs = [pl.BlockSpec((CONV_HALO, CONV_CW),
                               lambda t, c=c: (jnp.maximum(t * halo_per_blk - 1, 0), col0 + c)) for c in range(nc)]
    next_specs = [pl.BlockSpec((CONV_HALO, CONV_CW),
                               lambda t, c=c: (jnp.minimum((t + 1) * halo_per_blk, last_halo), col0 + c))
                  for c in range(nc)]
    return pl.pallas_call(
        _conv_kernel,
        grid=(n_rows // CONV_BLK,),
        in_specs=x_specs + prev_specs + next_specs + [
            pl.BlockSpec((None, SUBLANE, CONV_CH), lambda t: (j_even, 0, 0)),
            pl.BlockSpec((None, 1, CONV_CH), lambda t: (j_even, 0, 0)),
        ],
        out_specs=pl.BlockSpec((CONV_BLK, CONV_CH), lambda t: (t, 0)),
        out_shape=jax.ShapeDtypeStruct((n_rows, CONV_CH), F32),
        scratch_shapes=[pltpu.VMEM((CONV_BLK + 2 * CONV_HALO, CONV_CW), F32)],
        compiler_params=_cparams(("arbitrary",)),
        name="conv_silu",
    )(*([proj] * (3 * nc)), conv_w8, conv_b2)


def _softplus(x):
    return jnp.maximum(x, 0.0) + jnp.log1p(jnp.exp(-jnp.abs(x)))


def _dot_select(a, sel):
    sel = sel.astype(BF16)
    hi = a.astype(BF16)
    r1 = a - hi.astype(F32)
    mid = r1.astype(BF16)
    lo = (r1 - mid.astype(F32)).astype(BF16)
    return (jnp.dot(hi, sel, preferred_element_type=F32) + jnp.dot(mid, sel, preferred_element_type=F32)
            + jnp.dot(lo, sel, preferred_element_type=F32))


def _select_dot(sel, a):
    sel = sel.astype(BF16)
    hi = a.astype(BF16)
    r1 = a - hi.astype(F32)
    mid = r1.astype(BF16)
    lo = (r1 - mid.astype(F32)).astype(BF16)
    return (jnp.dot(sel, hi, preferred_element_type=F32) + jnp.dot(sel, mid, preferred_element_type=F32)
            + jnp.dot(sel, lo, preferred_element_type=F32))


def _ssd_direction(xs_ref, b_ref, c_ref, dt_ref, dtb_ref, alog_ref, y_ref, h_scr, lane0, forward):
    q = SSD_CHUNK
    gw = SSD_INNER // SSD_GROUPS
    dt = _softplus(dt_ref[...] + dtb_ref[...])
    a = dt * (-jnp.exp(alog_ref[...]))
    ri = lax.broadcasted_iota(jnp.int32, (q, q), 0)
    ci = lax.broadcasted_iota(jnp.int32, (q, q), 1)
    tri = (ci <= ri) if forward else (ci >= ri)
    a_cum = _select_dot(tri.astype(F32), a)
    a_cum_t = a_cum.T
    er = lax.broadcasted_iota(jnp.int32, (LANE, SSD_INNER), 0)
    ec = lax.broadcasted_iota(jnp.int32, (LANE, SSD_INNER), 1)
    expand = ((er - lane0) == (ec // SSD_HEAD_DIM)).astype(F32)
    dt_e = _dot_select(dt, expand)
    ac_e = _dot_select(a_cum, expand)
    end = q - 1 if forward else 0
    a_end_e = ac_e[end:end + 1, :]
    xdt = xs_ref[...] * dt_e
    in_decay = jnp.exp(ac_e)
    out_decay = jnp.exp(a_end_e - ac_e)
    state_decay = jnp.exp(a_end_e)
    lane = lax.broadcasted_iota(jnp.int32, (q, LANE), 1)
    for g in range(SSD_GROUPS):
        gs = slice(g * gw, (g + 1) * gw)
        bg_t = b_ref[:, g * SSD_STATE:(g + 1) * SSD_STATE].T.astype(BF16)
        cg = c_ref[:, g * SSD_STATE:(g + 1) * SSD_STATE].astype(BF16)
        cb = jnp.dot(cg, bg_t, preferred_element_type=F32)
        h_t = h_scr[:, gs]
        y_inter = jnp.dot(cg, h_t.astype(BF16), preferred_element_type=F32) * in_decay[:, gs]
        for pair in range(gw // LANE):
            cs = slice(g * gw + pair * LANE, g * gw + (pair + 1) * LANE)
            x_pair = xdt[:, cs].astype(BF16)
            res = []
            for sub in range(LANE // SSD_HEAD_DIM):
                hl = lane0 + (g * gw + pair * LANE) // SSD_HEAD_DIM + sub
                seg = a_cum[:, hl:hl + 1] - a_cum_t[hl:hl + 1, :]
                decay = jnp.exp(jnp.where(tri, seg, -jnp.inf))
                res.append(jnp.dot((cb * decay).astype(BF16), x_pair, preferred_element_type=F32))
            y_pair = jnp.where(lane < SSD_HEAD_DIM, res[0], res[1])
            y_ref[:, cs] = y_pair + y_inter[:, pair * LANE:(pair + 1) * LANE]
        x_out = (xdt[:, gs] * out_decay[:, gs]).astype(BF16)
        h_scr[:, gs] = state_decay[:, gs] * h_t + jnp.dot(bg_t, x_out, preferred_element_type=F32)


def _ssd_kernel(xs_f, b_f, c_f, dt_f, xs_b, b_b, c_b, dt_b, dtb_ref, alog_ref, yf_ref, yb_ref,
                hf_scr, hb_scr):
    @pl.when(pl.program_id(1) == 0)
    def _():
        hf_scr[...] = jnp.zeros_like(hf_scr)
        hb_scr[...] = jnp.zeros_like(hb_scr)

    _ssd_direction(xs_f, b_f, c_f, dt_f, dtb_ref, alog_ref, yf_ref, hf_scr, 0, True)
    _ssd_direction(xs_b, b_b, c_b, dt_b, dtb_ref, alog_ref, yb_ref, hb_scr, SSD_HEADS, False)


def _ssd_scan(act, proj, dtb, alog, j_even):
    q = SSD_CHUNK
    n_rows = act.shape[0]
    n_ctx = CTX_LEN // q
    n_lat = SEQ // q
    ctx_blk0 = N_LAT // q

    def fwd_blk(b, s):
        return jnp.where(s < n_ctx, ctx_blk0 + b * n_ctx + s, b * n_lat + (s - n_ctx))

    def bwd_blk(b, s):
        return jnp.where(s < n_ctx, ctx_blk0 + b * n_ctx + (n_ctx - 1 - s),
                         b * n_lat + (n_lat - 1 - (s - n_ctx)))

    gn = SSD_GROUPS * SSD_STATE
    b_col = SSD_INNER // gn
    c_col = b_col + 1

    def specs(blk):
        return [
            pl.BlockSpec((q, SSD_INNER), lambda b, s: (blk(b, s), 0)),
            pl.BlockSpec((q, gn), lambda b, s: (blk(b, s), b_col)),
            pl.BlockSpec((q, gn), lambda b, s: (blk(b, s), c_col)),
            pl.BlockSpec((q, LANE), lambda b, s: (blk(b, s), COL_DT)),
        ]

    small = pl.BlockSpec((None, 1, LANE), lambda b, s: (j_even, 0, 0))
    return pl.pallas_call(
        _ssd_kernel,
        grid=(BATCH, n_ctx + n_lat),
        in_specs=specs(fwd_blk) + specs(bwd_blk) + [small, small],
        out_specs=[pl.BlockSpec((q, SSD_INNER), lambda b, s: (fwd_blk(b, s), 0)),
                   pl.BlockSpec((q, SSD_INNER), lambda b, s: (bwd_blk(b, s), 0))],
        out_shape=[jax.ShapeDtypeStruct((n_rows, SSD_INNER), F32)] * 2,
        scratch_shapes=[pltpu.VMEM((SSD_STATE, SSD_INNER), F32)] * 2,
        compiler_params=_cparams(("arbitrary", "arbitrary")),
        name="ssd_scan",
    )(act, act, act, proj, act, act, act, proj, dtb, alog)


def _out_proj_kernel(attn_ref, yf_ref, yb_ref, xs_ref, z0_ref, z1_ref, dsk_ref, ng_ref, w_ref, x_ref, mod_ref,
                     o_ref):
    mix = jnp.dot(attn_ref[...], w_ref[0:NA_WIDTH, :], preferred_element_type=F32)
    gw = SSD_INNER // SSD_GROUPS
    for g, z_ref in enumerate((z0_ref, z1_ref)):
        gs = slice(g * gw, (g + 1) * gw)
        y = yf_ref[:, gs] + yb_ref[:, gs] + dsk_ref[:, gs] * xs_ref[:, gs]
        seg = y * _silu(z_ref[...])
        ms = jnp.mean(seg * seg, axis=-1, keepdims=True)
        yn = (seg * lax.rsqrt(ms + EPS) * ng_ref[:, g * gw:(g + 1) * gw]).astype(BF16)
        mix = mix + jnp.dot(yn, w_ref[NA_WIDTH + g * gw:NA_WIDTH + (g + 1) * gw, :],
                            preferred_element_type=F32)
    o_ref[...] = x_ref[...] + mod_ref[2] * mix


def _out_proj(attn, y_f, y_b, act, proj, dsk, ng, w_out, x, n_rows, mods_i, j_even):
    tm = 512
    mrow = _mod_row(tm)
    gw = SSD_INNER // SSD_GROUPS
    z_col = COL_Z * LANE // gw
    assert SSD_GROUPS == 2
    return pl.pallas_call(
        _out_proj_kernel,
        grid=(n_rows // tm,),
        in_specs=[
            pl.BlockSpec((tm, NA_WIDTH), lambda t: (t, 0)),
            pl.BlockSpec((tm, SSD_INNER), lambda t: (t, 0)),
            pl.BlockSpec((tm, SSD_INNER), lambda t: (t, 0)),
            pl.BlockSpec((tm, SSD_INNER), lambda t: (t, 0)),
            pl.BlockSpec((tm, gw), lambda t: (t, z_col)),
            pl.BlockSpec((tm, gw), lambda t: (t, z_col + 1)),
            pl.BlockSpec((None, 1, SSD_INNER), lambda t: (j_even, 0, 0)),
            pl.BlockSpec((None, 1, SSD_INNER), lambda t: (j_even, 0, 0)),
            pl.BlockSpec((None, D_MODEL, D_MODEL), lambda t: (j_even, 0, 0)),
            pl.BlockSpec((tm, D_MODEL), lambda t: (t, 0)),
            pl.BlockSpec((None, 3, 1, D_MODEL), lambda t: (mrow(t), 1, 0, 0)),
        ],
        out_specs=pl.BlockSpec((tm, D_MODEL), lambda t: (t, 0)),
        out_shape=jax.ShapeDtypeStruct((n_rows, D_MODEL), F32),
        compiler_params=_cparams(("arbitrary",)),
        name="out_proj",
    )(attn, y_f, y_b, act, proj, proj, dsk, ng, w_out, x, mods_i)


FOURIER_TILE = 512


def _fourier_chan_kernel(x_ref, mod_ref, g_ref, cs_ref, pq_ref):
    h = _mod_norm(x_ref[...], g_ref[...], mod_ref[0], mod_ref[1])
    for g in range(F_GROUPS):
        gs = slice(g * F_GROUP_CH, (g + 1) * F_GROUP_CH)
        pq = jnp.dot(h[:, gs].astype(BF16), cs_ref[...], preferred_element_type=F32)
        pq_ref[0, :, gs] = pq[:, :F_GROUP_CH].astype(BF16)
        pq_ref[1, :, gs] = pq[:, F_GROUP_CH:].astype(BF16)


def _fourier_chan(x, n_rows, mods_i, g, cs):
    tm = FOURIER_TILE
    mrow = _mod_row(tm)
    return pl.pallas_call(
        _fourier_chan_kernel,
        grid=(n_rows // tm,),
        in_specs=[
            pl.BlockSpec((tm, D_MODEL), lambda t: (t, 0)),
            pl.BlockSpec((None, 3, 1, D_MODEL), lambda t: (mrow(t), 1, 0, 0)),
            pl.BlockSpec((1, D_MODEL), lambda t: (0, 0)),
            pl.BlockSpec((F_GROUP_CH, 2 * F_GROUP_CH), lambda t: (0, 0)),
        ],
        out_specs=pl.BlockSpec((None, 2, tm, D_MODEL), lambda t: (t, 0, 0, 0)),
        out_shape=jax.ShapeDtypeStruct((n_rows // tm, 2, tm, D_MODEL), BF16),
        compiler_params=_cparams(("arbitrary",)),
        name="fourier_chan",
    )(x, mods_i, g, cs)


DFT_SPLIT = 64


FOURIER_POS_SUB = 2


def _fourier_pos_kernel(*refs, scale, n_sub):
    u_ref, v_ref = refs[:2]
    pq_refs = refs[2:2 + n_sub]
    w_ref, x_ref, mod_ref, o_ref, acc_scr, cs_scr = refs[2 + n_sub:]
    k = pl.program_id(2)
    tb = v_ref.shape[-1] // n_sub

    @pl.when(k == 0)
    def _():
        acc_scr[...] = jnp.zeros_like(acc_scr)

    part = None
    for i, pq_ref in enumerate(pq_refs):
        ls = slice(i * tb, (i + 1) * tb)
        vr, vi = v_ref[0, :, ls], v_ref[1, :, ls]
        for a in range(u_ref.shape[1]):
            ur, ui = u_ref[0, a:a + 1, ls], u_ref[1, a:a + 1, ls]
            rows = slice(a * DFT_SPLIT, (a + 1) * DFT_SPLIT)
            cs_scr[i, rows, 0:tb] = (ur * vr - ui * vi).astype(BF16)
            cs_scr[i, rows, tb:2 * tb] = (-(ui * vr + ur * vi)).astype(BF16)
        pq = pq_ref[...].reshape(2 * tb, pq_ref.shape[-1])
        d = jnp.dot(cs_scr[i], pq, preferred_element_type=F32)
        part = d if part is None else part + d
    acc_scr[...] += part

    @pl.when(k == pl.num_programs(2) - 1)
    def _():
        f = (acc_scr[...] * scale).astype(BF16)
        o_ref[...] = x_ref[...] + mod_ref[2] * jnp.dot(f, w_ref[...], preferred_element_type=F32)


def _fourier_pos(tabs, pq, w, x, mods_i, j_odd, seq_len, row0, is_ctx):
    u_tab, v_tab = tabs
    tm = tb = min(seq_len, FOURIER_TILE)
    n_sub = min(FOURIER_POS_SUB, seq_len // tb)
    tk = n_sub * tb
    nm = seq_len // tm
    blk0 = row0 // tb
    per_tile = FOURIER_TILE // tb
    scale = float((seq_len * F_GROUP_CH) ** -0.5)

    def pq_spec(i):
        def index(b, m, k):
            blk = blk0 + b * nm + k * n_sub + i
            return blk // per_tile, 0, blk % per_tile, 0
        return pl.BlockSpec((None, 2, tb, D_MODEL), index)

    return pl.pallas_call(
        functools.partial(_fourier_pos_kernel, scale=scale, n_sub=n_sub),
        grid=(BATCH, nm, seq_len // tk),
        in_specs=[
            pl.BlockSpec((2, tm // DFT_SPLIT, tk), lambda b, m, k: (0, m, k)),
            pl.BlockSpec((2, DFT_SPLIT, tk), lambda b, m, k: (0, 0, k)),
            *[pq_spec(i) for i in range(n_sub)],
            pl.BlockSpec((None, D_MODEL, D_MODEL), lambda b, m, k: (j_odd, 0, 0)),
            pl.BlockSpec((tm, D_MODEL), lambda b, m, k: (blk0 + b * nm + m, 0)),
            pl.BlockSpec((None, 3, 1, D_MODEL), lambda b, m, k: (BATCH if is_ctx else b, 1, 0, 0)),
        ],
        out_specs=pl.BlockSpec((tm, D_MODEL), lambda b, m, k: (b * nm + m, 0)),
        out_shape=jax.ShapeDtypeStruct((BATCH * seq_len, D_MODEL), F32),
        scratch_shapes=[pltpu.VMEM((tm, D_MODEL), F32), pltpu.VMEM((n_sub, tm, 2 * tb), BF16)],
        compiler_params=_cparams(("arbitrary", "arbitrary", "arbitrary")),
        name="fourier_pos",
    )(u_tab, v_tab, *([pq] * n_sub), w, x, mods_i)


def _phase_table(freq, n):
    l = jnp.arange(n, dtype=jnp.int32)
    ang = ((freq[:, None] * l[None, :]) % n).astype(F32) * np.float32(2.0 * np.pi / n)
    return jnp.stack([jnp.cos(ang), jnp.sin(ang)])


def _dft_tables(n):
    a = jnp.arange(n // DFT_SPLIT, dtype=jnp.int32) * DFT_SPLIT
    b = jnp.arange(DFT_SPLIT, dtype=jnp.int32)
    return _phase_table(a, n), _phase_table(b, n)


def kernel(x, c, ctx, c_ctx, mod_w, mod_b, norm_g, ffn_w1, ffn_w3, ffn_w2, mix_w_in, mix_w_out, qk_g, rpb,
           conv_w, conv_b, dt_bias, a_log, ssd_d, ssd_norm_g, fourier_w):
    n_even = mix_w_in.shape[0]
    xs = jnp.concatenate([x.reshape(N_LAT, D_MODEL), ctx.reshape(BATCH * CTX_LEN, D_MODEL)], axis=0)
    mods = _modulation(c, c_ctx, mod_w, mod_b)

    w1 = ffn_w1.astype(BF16)
    w3 = ffn_w3.astype(BF16)
    w2 = ffn_w2.astype(BF16)
    w_in = jnp.pad(mix_w_in, ((0, 0), (0, 0), (0, IN_W_PAD - IN_W))).astype(BF16)
    w_out = mix_w_out.astype(BF16)
    w_f = fourier_w.astype(BF16)
    qkg = qk_g.reshape(n_even, 2, 1, NA_HEAD_DIM)
    conv_w8 = jnp.pad(conv_w, ((0, 0), (0, SUBLANE - CONV_K), (0, 0)))
    conv_b2 = conv_b.reshape(n_even, 1, CONV_CH)
    pad_l = LANE - 2 * SSD_HEADS
    dtb = jnp.pad(dt_bias.reshape(n_even, 1, 2 * SSD_HEADS), ((0, 0), (0, 0), (0, pad_l)))
    alog = jnp.pad(a_log.reshape(n_even, 1, 2 * SSD_HEADS), ((0, 0), (0, 0), (0, pad_l)))
    dsk = jnp.repeat(ssd_d, SSD_HEAD_DIM, axis=-1).reshape(n_even, 1, SSD_INNER)
    ng = ssd_norm_g.reshape(n_even, 1, SSD_INNER)
    cos_t, sin_t = _rope_tables()
    chan = _phase_table(jnp.arange(F_GROUP_CH, dtype=jnp.int32), F_GROUP_CH)
    cs_chan = jnp.concatenate([chan[0], chan[1]], axis=1).astype(BF16)
    dft_lat = _dft_tables(SEQ)
    dft_ctx = _dft_tables(CTX_LEN)

    last_ctx = ((DEPTH - 1) // 2) * 2
    for i in range(DEPTH):
        use_ctx = i <= last_ctx
        ctx_out = i < last_ctx
        j = i // 2
        mods_i = mods[i]
        g = norm_g[i].reshape(3, 1, D_MODEL)
        rows_in = N_TOK if use_ctx else N_LAT
        rows_out = N_TOK if ctx_out else N_LAT
        xs = _ffn(xs, rows_in, mods_i, 0, g[0], w1, w3, w2, i, 0)
        if i % 2 == 0:
            assert use_ctx, "even mixer layers read the context keys and SSD states"
            proj = _in_proj(xs, rows_in, mods_i, g[1], w_in, j)
            bias = _attention_bias(rpb[j])
            attn, attn_ctx = _na_attention(proj, cos_t, sin_t, qkg[j], bias)
            if ctx_out:
                attn = jnp.concatenate([attn, attn_ctx], axis=0)
            act = _conv_silu(proj, conv_w8, conv_b2, j)
            y_f, y_b = _ssd_scan(act, proj, dtb, alog, j)
            xs = _out_proj(attn, y_f, y_b, act, proj, dsk, ng, w_out, xs, rows_out, mods_i, j)
        else:
            pq = _fourier_chan(xs, rows_out, mods_i, g[1], cs_chan)
            new = _fourier_pos(dft_lat, pq, w_f, xs, mods_i, j, SEQ, 0, False)
            if ctx_out:
                new_ctx = _fourier_pos(dft_ctx, pq, w_f, xs, mods_i, j, CTX_LEN, N_LAT, True)
                new = jnp.concatenate([new, new_ctx], axis=0)
            xs = new
        xs = _ffn(xs, rows_out, mods_i, 2, g[2], w1, w3, w2, i, 1)
    return xs[:N_LAT].reshape(BATCH, SEQ, D_MODEL)
```

```python
import functools

import numpy as np
import jax
import jax.numpy as jnp
from jax import lax
from jax.experimental import pallas as pl
from jax.experimental.pallas import tpu as pltpu

F32 = jnp.float32
BF16 = jnp.bfloat16

D_MODEL = 2048
BATCH = 2
SEQ = 4096
DEPTH = 4
GRID_W = 64
GRID_ROWS = SEQ // GRID_W
CTX_LEN = 256
N_LAT = BATCH * SEQ
N_TOK = N_LAT + BATCH * CTX_LEN
NA_HEADS = 8
NA_HEAD_DIM = 128
NA_WIDTH = NA_HEADS * NA_HEAD_DIM
WIN_R = 8
WIN_C = 16
ROPE_BASE = 10000.0
SSD_HEADS = 16
SSD_HEAD_DIM = 64
SSD_INNER = SSD_HEADS * SSD_HEAD_DIM
SSD_GROUPS = 2
SSD_STATE = 128
SSD_CHUNK = 128
CONV_K = 5
CONV_CH = SSD_INNER + 2 * SSD_GROUPS * SSD_STATE
IN_W = 3 * NA_WIDTH + SSD_INNER + CONV_CH + 2 * SSD_HEADS
LANE = 128
SUBLANE = 8
IN_W_PAD = ((IN_W + LANE - 1) // LANE) * LANE
F_GROUPS = 4
F_GROUP_CH = D_MODEL // F_GROUPS
D_FF = 5632
N_MOD = 9
EPS = 1e-6
ATTN_SCALE = NA_HEAD_DIM ** -0.5

COL_Q = 0
COL_K = NA_WIDTH // LANE
COL_V = 2 * NA_WIDTH // LANE
COL_Z = 3 * NA_WIDTH // LANE
COL_XBC = (3 * NA_WIDTH + SSD_INNER) // LANE
COL_DT = (3 * NA_WIDTH + SSD_INNER + CONV_CH) // LANE

Q_ROWS = 32
Q_BLK = Q_ROWS * GRID_W
Q_SPLIT = 8
PART_ROWS = Q_ROWS // Q_SPLIT
PART_Q = PART_ROWS * GRID_W
K_ROWS = 12
K_BLK = K_ROWS * GRID_W
N_QBLK = GRID_ROWS // Q_ROWS

VMEM_LIMIT = 56 * 1024 * 1024


def _cparams(sem, vmem_limit=VMEM_LIMIT):
    return pltpu.CompilerParams(dimension_semantics=sem, vmem_limit_bytes=vmem_limit)


def _sigmoid(x):
    return 1.0 / (1.0 + jnp.exp(-x))


def _silu(x):
    return x * _sigmoid(x)


def _mod_norm(x, g, shift, scale):
    ms = jnp.mean(x * x, axis=-1, keepdims=True)
    y = x * lax.rsqrt(ms + EPS) * g
    return y * (1.0 + scale) + shift


def _mod_row(tile_rows):
    def f(t):
        return jnp.minimum((t * tile_rows) // SEQ, BATCH)
    return f


MOD_TK = 256


def _mod_kernel(c_ref, w_ref, b_ref, o_ref):
    @pl.when(pl.program_id(1) == 0)
    def _():
        o_ref[...] = jnp.zeros_like(o_ref) + b_ref[...]

    s = _silu(c_ref[...]).astype(BF16)
    o_ref[...] += jnp.dot(s, w_ref[...].astype(BF16), preferred_element_type=F32)


def _modulation(c, c_ctx, mod_w, mod_b):
    rows = SUBLANE
    cvec = jnp.concatenate([c, c_ctx[None, :], jnp.zeros((rows - BATCH - 1, D_MODEL), F32)], axis=0)
    n = N_MOD * D_MODEL
    out = pl.pallas_call(
        _mod_kernel,
        grid=(DEPTH, D_MODEL // MOD_TK),
        in_specs=[
            pl.BlockSpec((rows, MOD_TK), lambda i, k: (0, k)),
            pl.BlockSpec((None, MOD_TK, n), lambda i, k: (i, k, 0)),
            pl.BlockSpec((None, 1, n), lambda i, k: (i, 0, 0)),
        ],
        out_specs=pl.BlockSpec((None, rows, n), lambda i, k: (i, 0, 0)),
        out_shape=jax.ShapeDtypeStruct((DEPTH, rows, n), F32),
        compiler_params=_cparams(("arbitrary", "arbitrary")),
        name="modulation",
    )(cvec, mod_w, mod_b.reshape(DEPTH, 1, n))
    return out[:, :BATCH + 1].reshape(DEPTH, BATCH + 1, N_MOD, 1, D_MODEL)


FFN_TM = 1024
FFN_TF = 512
FFN_VMEM_LIMIT = 63 * 1024 * 1024


def _ffn_accumulate(h_scr, w1_ref, w3_ref, w2_ref, o_ref, rows):
    h = h_scr[0:rows, :]
    a = jnp.dot(h, w1_ref[...], preferred_element_type=F32)
    b = jnp.dot(h, w3_ref[...], preferred_element_type=F32)
    u = (_silu(a) * b).astype(BF16)
    o_ref[0:rows, :] += jnp.dot(u, w2_ref[...], preferred_element_type=F32)


def _ffn_kernel(x_ref, mod_ref, g_ref, w1_ref, w3_ref, w2_ref, o_ref, h_scr, *, tail_rows):
    t = pl.program_id(0)
    j = pl.program_id(1)
    last_t = pl.num_programs(0) - 1

    @pl.when(j == 0)
    def _():
        h_scr[...] = _mod_norm(x_ref[...], g_ref[...], mod_ref[0], mod_ref[1]).astype(BF16)
        o_ref[...] = jnp.zeros_like(o_ref)

    if tail_rows == FFN_TM:
        _ffn_accumulate(h_scr, w1_ref, w3_ref, w2_ref, o_ref, FFN_TM)
    else:
        @pl.when(t < last_t)
        def _():
            _ffn_accumulate(h_scr, w1_ref, w3_ref, w2_ref, o_ref, FFN_TM)

        @pl.when(t == last_t)
        def _():
            _ffn_accumulate(h_scr, w1_ref, w3_ref, w2_ref, o_ref, tail_rows)

    @pl.when(j == pl.num_programs(1) - 1)
    def _():
        o_ref[...] = x_ref[...] + 0.5 * mod_ref[2] * o_ref[...]


def _ffn(x, n_rows, mods_i, sub, g, w1, w3, w2, layer, which):
    tm, tf = FFN_TM, FFN_TF
    mrow = _mod_row(tm)
    nt = pl.cdiv(n_rows, tm)
    tail_rows = n_rows - (nt - 1) * tm
    return pl.pallas_call(
        functools.partial(_ffn_kernel, tail_rows=tail_rows),
        grid=(nt, D_FF // tf),
        in_specs=[
            pl.BlockSpec((tm, D_MODEL), lambda t, j: (t, 0)),
            pl.BlockSpec((None, 3, 1, D_MODEL), lambda t, j: (mrow(t), sub, 0, 0)),
            pl.BlockSpec((1, D_MODEL), lambda t, j: (0, 0)),
            pl.BlockSpec((None, None, D_MODEL, tf), lambda t, j: (layer, which, 0, j)),
            pl.BlockSpec((None, None, D_MODEL, tf), lambda t, j: (layer, which, 0, j)),
            pl.BlockSpec((None, None, tf, D_MODEL), lambda t, j: (layer, which, j, 0)),
        ],
        out_specs=pl.BlockSpec((tm, D_MODEL), lambda t, j: (t, 0)),
        out_shape=jax.ShapeDtypeStruct((n_rows, D_MODEL), F32),
        scratch_shapes=[pltpu.VMEM((tm, D_MODEL), BF16)],
        compiler_params=_cparams(("arbitrary", "arbitrary"), FFN_VMEM_LIMIT),
        name="ffn",
    )(x, mods_i, g, w1, w3, w2)


IN_TM = 512
IN_TN = 1920


def _norm_matmul_kernel(x_ref, mod_ref, g_ref, w_ref, o_ref, h_scr):
    @pl.when(pl.program_id(1) == 0)
    def _():
        h_scr[...] = _mod_norm(x_ref[...], g_ref[...], mod_ref[0], mod_ref[1]).astype(BF16)

    o_ref[...] = jnp.dot(h_scr[...], w_ref[...], preferred_element_type=F32)


def _in_proj(x, n_rows, mods_i, g, w_in, j_even):
    tm, tn = IN_TM, IN_TN
    mrow = _mod_row(tm)
    return pl.pallas_call(
        _norm_matmul_kernel,
        grid=(n_rows // tm, IN_W_PAD // tn),
        in_specs=[
            pl.BlockSpec((tm, D_MODEL), lambda t, j: (t, 0)),
            pl.BlockSpec((None, 3, 1, D_MODEL), lambda t, j: (mrow(t), 1, 0, 0)),
            pl.BlockSpec((1, D_MODEL), lambda t, j: (0, 0)),
            pl.BlockSpec((None, D_MODEL, tn), lambda t, j: (j_even, 0, j)),
        ],
        out_specs=pl.BlockSpec((tm, tn), lambda t, j: (t, j)),
        out_shape=jax.ShapeDtypeStruct((n_rows, IN_W_PAD), F32),
        scratch_shapes=[pltpu.VMEM((tm, D_MODEL), BF16)],
        compiler_params=_cparams(("arbitrary", "arbitrary")),
        name="in_proj",
    )(x, mods_i, g, w_in)


def _head_rms(x, g):
    ms = jnp.mean(x * x, axis=-1, keepdims=True)
    return x * lax.rsqrt(ms + EPS) * g


def _rope(x, cos, sin):
    lane = lax.broadcasted_iota(jnp.int32, x.shape, 1)
    first_half = (lane & (NA_HEAD_DIM // 4)) == 0
    partner = jnp.where(first_half, pltpu.roll(x, NA_HEAD_DIM - NA_HEAD_DIM // 4, 1),
                        pltpu.roll(x, NA_HEAD_DIM // 4, 1))
    return x * cos + partner * sin


def _dot_nt(a, b):
    return lax.dot_general(a, b, (((1,), (1,)), ((), ())), preferred_element_type=F32)


def _softmax_pv(scores, values):
    m = functools.reduce(jnp.maximum, [jnp.max(s, axis=-1, keepdims=True) for s in scores])
    ps = [jnp.exp(s - m) for s in scores]
    denom = functools.reduce(jnp.add, [jnp.sum(p, axis=-1, keepdims=True) for p in ps])
    o = functools.reduce(jnp.add, [jnp.dot(p.astype(BF16), v, preferred_element_type=F32)
                                   for p, v in zip(ps, values)])
    return o / denom


def _part_key_row(j, part):
    return jnp.clip(Q_ROWS * j + PART_ROWS * part - WIN_R // 2, 0, GRID_ROWS - K_ROWS)


def _na_kernel(*refs):
    (q_ref, k_ref, v_ref, qc_ref, kc_ref, vc_ref, cos_ref, sin_ref, qkg_ref), refs = refs[:9], refs[9:]
    bias_refs, (o_ref, oc_ref, k_scr, v_scr, kc_scr, vc_scr) = refs[:Q_SPLIT], refs[Q_SPLIT:]
    j = pl.program_id(2)

    @pl.when(j == 0)
    def _():
        kn = _head_rms(k_ref[...], qkg_ref[1])
        k_scr[...] = _rope(kn, cos_ref[...], sin_ref[...]).astype(BF16)
        v_scr[...] = v_ref[...].astype(BF16)
        kcn = _head_rms(kc_ref[...], qkg_ref[1]).astype(BF16)
        vcb = vc_ref[...].astype(BF16)
        kc_scr[...] = kcn
        vc_scr[...] = vcb
        qcn = _head_rms(qc_ref[...], qkg_ref[0]).astype(BF16)
        oc_ref[...] = _softmax_pv([_dot_nt(qcn, kcn) * ATTN_SCALE], [vcb]).astype(oc_ref.dtype)

    for part in range(Q_SPLIT):
        rows = pl.ds(part * PART_Q, PART_Q)
        q0 = pl.multiple_of(j * Q_BLK + part * PART_Q, PART_Q)
        k0 = pl.multiple_of(_part_key_row(j, part) * GRID_W, GRID_W)
        kw = k_scr[pl.ds(k0, K_BLK), :]
        vw = v_scr[pl.ds(k0, K_BLK), :]
        qn = _head_rms(q_ref[rows, :], qkg_ref[0])
        qr = _rope(qn, cos_ref[pl.ds(q0, PART_Q), :], sin_ref[pl.ds(q0, PART_Q), :]).astype(BF16)
        s_win = _dot_nt(qr, kw) * ATTN_SCALE + bias_refs[part][...]
        s_ctx = _dot_nt(qn.astype(BF16), kc_scr[...]) * ATTN_SCALE
        o_ref[rows, :] = _softmax_pv([s_win, s_ctx], [vw, vc_scr[...]]).astype(o_ref.dtype)


BIAS_PARTS = ((0, 0), (N_QBLK // 2, 0), (N_QBLK - 1, Q_SPLIT - 1))


def _bias_pattern(j, part):
    first = (j == 0) & (part == 0)
    last = (j == N_QBLK - 1) & (part == Q_SPLIT - 1)
    return jnp.where(first, 0, jnp.where(last, 2, 1))


def _na_attention(proj, cos_t, sin_t, qkg, bias):
    lat_blk = SEQ // Q_BLK
    ctx_blk0 = N_LAT // CTX_LEN
    hd = NA_HEAD_DIM
    bias_specs = [pl.BlockSpec((None, None, PART_Q, K_BLK), lambda b, h, j, p=p: (h, _bias_pattern(j, p), 0, 0))
                  for p in range(Q_SPLIT)]
    return pl.pallas_call(
        _na_kernel,
        grid=(BATCH, NA_HEADS, N_QBLK),
        in_specs=[
            pl.BlockSpec((Q_BLK, hd), lambda b, h, j: (b * lat_blk + j, COL_Q + h)),
            pl.BlockSpec((SEQ, hd), lambda b, h, j: (b, COL_K + h)),
            pl.BlockSpec((SEQ, hd), lambda b, h, j: (b, COL_V + h)),
            pl.BlockSpec((CTX_LEN, hd), lambda b, h, j: (ctx_blk0 + b, COL_Q + h)),
            pl.BlockSpec((CTX_LEN, hd), lambda b, h, j: (ctx_blk0 + b, COL_K + h)),
            pl.BlockSpec((CTX_LEN, hd), lambda b, h, j: (ctx_blk0 + b, COL_V + h)),
            pl.BlockSpec((SEQ, hd), lambda b, h, j: (0, 0)),
            pl.BlockSpec((SEQ, hd), lambda b, h, j: (0, 0)),
            pl.BlockSpec((2, 1, hd), lambda b, h, j: (0, 0, 0)),
        ] + bias_specs,
        out_specs=[pl.BlockSpec((Q_BLK, hd), lambda b, h, j: (b * lat_blk + j, h)),
                   pl.BlockSpec((CTX_LEN, hd), lambda b, h, j: (b, h))],
        out_shape=[jax.ShapeDtypeStruct((N_LAT, NA_WIDTH), BF16),
                   jax.ShapeDtypeStruct((BATCH * CTX_LEN, NA_WIDTH), BF16)],
        scratch_shapes=[pltpu.VMEM((SEQ, hd), BF16), pltpu.VMEM((SEQ, hd), BF16),
                        pltpu.VMEM((CTX_LEN, hd), BF16), pltpu.VMEM((CTX_LEN, hd), BF16)],
        compiler_params=_cparams(("arbitrary", "arbitrary", "arbitrary")),
        name="na_attention",
    )(proj, proj, proj, proj, proj, proj, cos_t, sin_t, qkg, *([bias] * Q_SPLIT))


def _part_geometry(jb, part):
    rows = Q_ROWS * jb + PART_ROWS * part + np.arange(PART_ROWS)
    k_first = int(np.clip(rows[0] - WIN_R // 2, 0, GRID_ROWS - K_ROWS))
    r0 = np.clip(rows - WIN_R // 2, 0, GRID_ROWS - WIN_R)
    return rows, k_first, r0


def _attention_bias(rpb_j):
    for jb in range(N_QBLK):
        for part in range(Q_SPLIT):
            rows, k_first, r0 = _part_geometry(jb, part)
            pat = 0 if (jb, part) == (0, 0) else 2 if (jb, part) == (N_QBLK - 1, Q_SPLIT - 1) else 1
            rows_p, k_first_p, r0_p = _part_geometry(*BIAS_PARTS[pat])
            assert k_first - rows[0] == k_first_p - rows_p[0] and np.array_equal(r0 - rows, r0_p - rows_p)
    n_dr = 2 * WIN_R - 1
    qcol = np.arange(GRID_W)
    c0 = np.clip(qcol - WIN_C // 2, 0, GRID_W - WIN_C)
    col_ok = (qcol[None, :] >= c0[:, None]) & (qcol[None, :] < c0[:, None] + WIN_C)
    r = rpb_j.astype(F32)
    edge = GRID_W - WIN_C
    ep = jnp.concatenate([jnp.repeat(r[..., :1], edge, axis=-1), r, jnp.repeat(r[..., -1:], edge, axis=-1)],
                         axis=-1)
    t1 = jnp.stack([ep[..., GRID_W - 1 - qc:2 * GRID_W - 1 - qc] for qc in range(GRID_W)], axis=2)
    t1 = jnp.where(jnp.asarray(col_ok)[None, None], t1, -jnp.inf)
    band = t1.transpose(0, 2, 1, 3).reshape(NA_HEADS, GRID_W, n_dr * GRID_W)
    pad_w = K_BLK
    band = jnp.pad(band, ((0, 0), (0, 0), (pad_w, pad_w)), constant_values=-jnp.inf)
    strips, masks = [], []
    for jb, part in BIAS_PARTS:
        rows, k_first, r0 = _part_geometry(jb, part)
        kr = k_first + np.arange(K_ROWS)
        for row, row0 in zip(rows, r0):
            off = pad_w + (k_first - int(row) + WIN_R - 1) * GRID_W
            assert 0 <= off and off + K_BLK <= band.shape[-1]
            strips.append(band[:, :, off:off + K_BLK])
            masks.append(np.repeat((kr >= row0) & (kr < row0 + WIN_R), GRID_W))
    vals = jnp.stack(strips, axis=1)
    mask = np.stack(masks)[None, :, None, :]
    bias = jnp.where(jnp.asarray(mask), vals, -jnp.inf)
    return bias.reshape(NA_HEADS, len(BIAS_PARTS), PART_Q, K_BLK)


def _rope_tables():
    quarter = NA_HEAD_DIM // 4
    inv_freq = ROPE_BASE ** (-jnp.arange(quarter, dtype=F32) / quarter)
    t = jnp.arange(SEQ)
    ang_r = (t // GRID_W).astype(F32)[:, None] * inv_freq[None, :]
    ang_c = (t % GRID_W).astype(F32)[:, None] * inv_freq[None, :]
    cos_t = jnp.concatenate([jnp.cos(ang_r), jnp.cos(ang_r), jnp.cos(ang_c), jnp.cos(ang_c)], axis=-1)
    sin_t = jnp.concatenate([-jnp.sin(ang_r), jnp.sin(ang_r), -jnp.sin(ang_c), jnp.sin(ang_c)], axis=-1)
    return cos_t, sin_t


CONV_HALO = SUBLANE
CONV_BLK = 256


CONV_CW = 512


def _conv_kernel(*refs):
    nc = CONV_CH // CONV_CW
    x_refs, prev_refs, next_refs = refs[:nc], refs[nc:2 * nc], refs[2 * nc:3 * nc]
    w_ref, b_ref, o_ref, pad_scr = refs[3 * nc:]
    t = pl.program_id(0)
    n_lat_blk = N_LAT // CONV_BLK
    is_ctx = t >= n_lat_blk
    seq_blks = jnp.where(is_ctx, CTX_LEN // CONV_BLK, SEQ // CONV_BLK)
    pos = jnp.where(is_ctx, t - n_lat_blk, t) % seq_blks
    for c in range(nc):
        cs = slice(c * CONV_CW, (c + 1) * CONV_CW)
        pad_scr[0:CONV_HALO, :] = jnp.where(pos == 0, 0.0, prev_refs[c][...])
        pad_scr[CONV_HALO:CONV_HALO + CONV_BLK, :] = x_refs[c][...]
        pad_scr[CONV_HALO + CONV_BLK:2 * CONV_HALO + CONV_BLK, :] = jnp.where(pos == seq_blks - 1, 0.0,
                                                                               next_refs[c][...])
        xp = pad_scr[...]
        acc = jnp.zeros((CONV_BLK, CONV_CW), F32) + b_ref[:, cs]
        for k in range(CONV_K):
            lo = CONV_HALO - CONV_K // 2 + k
            acc = acc + w_ref[k:k + 1, cs] * xp[lo:lo + CONV_BLK, :]
        o_ref[:, cs] = _silu(acc)


def _conv_silu(proj, conv_w8, conv_b2, j_even):
    n_rows = proj.shape[0]
    nc = CONV_CH // CONV_CW
    col0 = COL_XBC * LANE // CONV_CW
    halo_per_blk = CONV_BLK // CONV_HALO
    last_halo = n_rows // CONV_HALO - 1
    x_specs = [pl.BlockSpec((CONV_BLK, CONV_CW), lambda t, c=c: (t, col0 + c)) for c in range(nc)]
    prev_specs = [pl.BlockSpec((CONV_HALO, CONV_CW),
                               lambda t, c=c: (jnp.maximum(t * halo_per_blk - 1, 0), col0 + c)) for c in range(nc)]
    next_specs = [pl.BlockSpec((CONV_HALO, CONV_CW),
                               lambda t, c=c: (jnp.minimum((t + 1) * halo_per_blk, last_halo), col0 + c))
                  for c in range(nc)]
    return pl.pallas_call(
        _conv_kernel,
        grid=(n_rows // CONV_BLK,),
        in_specs=x_specs + prev_specs + next_specs + [
            pl.BlockSpec((None, SUBLANE, CONV_CH), lambda t: (j_even, 0, 0)),
            pl.BlockSpec((None, 1, CONV_CH), lambda t: (j_even, 0, 0)),
        ],
        out_specs=pl.BlockSpec((CONV_BLK, CONV_CH), lambda t: (t, 0)),
        out_shape=jax.ShapeDtypeStruct((n_rows, CONV_CH), F32),
        scratch_shapes=[pltpu.VMEM((CONV_BLK + 2 * CONV_HALO, CONV_CW), F32)],
        compiler_params=_cparams(("arbitrary",)),
        name="conv_silu",
    )(*([proj] * (3 * nc)), conv_w8, conv_b2)


def _softplus(x):
    return jnp.maximum(x, 0.0) + jnp.log1p(jnp.exp(-jnp.abs(x)))


def _dot_select(a, sel):
    sel = sel.astype(BF16)
    hi = a.astype(BF16)
    r1 = a - hi.astype(F32)
    mid = r1.astype(BF16)
    lo = (r1 - mid.astype(F32)).astype(BF16)
    return (jnp.dot(hi, sel, preferred_element_type=F32) + jnp.dot(mid, sel, preferred_element_type=F32)
            + jnp.dot(lo, sel, preferred_element_type=F32))


def _select_dot(sel, a):
    sel = sel.astype(BF16)
    hi = a.astype(BF16)
    r1 = a - hi.astype(F32)
    mid = r1.astype(BF16)
    lo = (r1 - mid.astype(F32)).astype(BF16)
    return (jnp.dot(sel, hi, preferred_element_type=F32) + jnp.dot(sel, mid, preferred_element_type=F32)
            + jnp.dot(sel, lo, preferred_element_type=F32))


def _ssd_direction(xs_ref, b_ref, c_ref, dt_ref, dtb_ref, alog_ref, y_ref, h_scr, lane0, forward):
    q = SSD_CHUNK
    gw = SSD_INNER // SSD_GROUPS
    dt = _softplus(dt_ref[...] + dtb_ref[...])
    a = dt * (-jnp.exp(alog_ref[...]))
    ri = lax.broadcasted_iota(jnp.int32, (q, q), 0)
    ci = lax.broadcasted_iota(jnp.int32, (q, q), 1)
    tri = (ci <= ri) if forward else (ci >= ri)
    a_cum = _select_dot(tri.astype(F32), a)
    a_cum_t = a_cum.T
    er = lax.broadcasted_iota(jnp.int32, (LANE, SSD_INNER), 0)
    ec = lax.broadcasted_iota(jnp.int32, (LANE, SSD_INNER), 1)
    expand = ((er - lane0) == (ec // SSD_HEAD_DIM)).astype(F32)
    dt_e = _dot_select(dt, expand)
    ac_e = _dot_select(a_cum, expand)
    end = q - 1 if forward else 0
    a_end_e = ac_e[end:end + 1, :]
    xdt = xs_ref[...] * dt_e
    in_decay = jnp.exp(ac_e)
    out_decay = jnp.exp(a_end_e - ac_e)
    state_decay = jnp.exp(a_end_e)
    lane = lax.broadcasted_iota(jnp.int32, (q, LANE), 1)
    for g in range(SSD_GROUPS):
        gs = slice(g * gw, (g + 1) * gw)
        bg_t = b_ref[:, g * SSD_STATE:(g + 1) * SSD_STATE].T.astype(BF16)
        cg = c_ref[:, g * SSD_STATE:(g + 1) * SSD_STATE].astype(BF16)
        cb = jnp.dot(cg, bg_t, preferred_element_type=F32)
        h_t = h_scr[:, gs]
        y_inter = jnp.dot(cg, h_t.astype(BF16), preferred_element_type=F32) * in_decay[:, gs]
        for pair in range(gw // LANE):
            cs = slice(g * gw + pair * LANE, g * gw + (pair + 1) * LANE)
            x_pair = xdt[:, cs].astype(BF16)
            res = []
            for sub in range(LANE // SSD_HEAD_DIM):
                hl = lane0 + (g * gw + pair * LANE) // SSD_HEAD_DIM + sub
                seg = a_cum[:, hl:hl + 1] - a_cum_t[hl:hl + 1, :]
                decay = jnp.exp(jnp.where(tri, seg, -jnp.inf))
                res.append(jnp.dot((cb * decay).astype(BF16), x_pair, preferred_element_type=F32))
            y_pair = jnp.where(lane < SSD_HEAD_DIM, res[0], res[1])
            y_ref[:, cs] = y_pair + y_inter[:, pair * LANE:(pair + 1) * LANE]
        x_out = (xdt[:, gs] * out_decay[:, gs]).astype(BF16)
        h_scr[:, gs] = state_decay[:, gs] * h_t + jnp.dot(bg_t, x_out, preferred_element_type=F32)


def _ssd_kernel(xs_f, b_f, c_f, dt_f, xs_b, b_b, c_b, dt_b, dtb_ref, alog_ref, yf_ref, yb_ref,
                hf_scr, hb_scr):
    @pl.when(pl.program_id(1) == 0)
    def _():
        hf_scr[...] = jnp.zeros_like(hf_scr)
        hb_scr[...] = jnp.zeros_like(hb_scr)

    _ssd_direction(xs_f, b_f, c_f, dt_f, dtb_ref, alog_ref, yf_ref, hf_scr, 0, True)
    _ssd_direction(xs_b, b_b, c_b, dt_b, dtb_ref, alog_ref, yb_ref, hb_scr, SSD_HEADS, False)


def _ssd_scan(act, proj, dtb, alog, j_even):
    q = SSD_CHUNK
    n_rows = act.shape[0]
    n_ctx = CTX_LEN // q
    n_lat = SEQ // q
    ctx_blk0 = N_LAT // q

    def fwd_blk(b, s):
        return jnp.where(s < n_ctx, ctx_blk0 + b * n_ctx + s, b * n_lat + (s - n_ctx))

    def bwd_blk(b, s):
        return jnp.where(s < n_ctx, ctx_blk0 + b * n_ctx + (n_ctx - 1 - s),
                         b * n_lat + (n_lat - 1 - (s - n_ctx)))

    gn = SSD_GROUPS * SSD_STATE
    b_col = SSD_INNER // gn
    c_col = b_col + 1

    def specs(blk):
        return [
            pl.BlockSpec((q, SSD_INNER), lambda b, s: (blk(b, s), 0)),
            pl.BlockSpec((q, gn), lambda b, s: (blk(b, s), b_col)),
            pl.BlockSpec((q, gn), lambda b, s: (blk(b, s), c_col)),
            pl.BlockSpec((q, LANE), lambda b, s: (blk(b, s), COL_DT)),
        ]

    small = pl.BlockSpec((None, 1, LANE), lambda b, s: (j_even, 0, 0))
    return pl.pallas_call(
        _ssd_kernel,
        grid=(BATCH, n_ctx + n_lat),
        in_specs=specs(fwd_blk) + specs(bwd_blk) + [small, small],
        out_specs=[pl.BlockSpec((q, SSD_INNER), lambda b, s: (fwd_blk(b, s), 0)),
                   pl.BlockSpec((q, SSD_INNER), lambda b, s: (bwd_blk(b, s), 0))],
        out_shape=[jax.ShapeDtypeStruct((n_rows, SSD_INNER), F32)] * 2,
        scratch_shapes=[pltpu.VMEM((SSD_STATE, SSD_INNER), F32)] * 2,
        compiler_params=_cparams(("arbitrary", "arbitrary")),
        name="ssd_scan",
    )(act, act, act, proj, act, act, act, proj, dtb, alog)


def _out_proj_kernel(attn_ref, yf_ref, yb_ref, xs_ref, z0_ref, z1_ref, dsk_ref, ng_ref, w_ref, x_ref, mod_ref,
                     o_ref):
    mix = jnp.dot(attn_ref[...], w_ref[0:NA_WIDTH, :], preferred_element_type=F32)
    gw = SSD_INNER // SSD_GROUPS
    for g, z_ref in enumerate((z0_ref, z1_ref)):
        gs = slice(g * gw, (g + 1) * gw)
        y = yf_ref[:, gs] + yb_ref[:, gs] + dsk_ref[:, gs] * xs_ref[:, gs]
        seg = y * _silu(z_ref[...])
        ms = jnp.mean(seg * seg, axis=-1, keepdims=True)
        yn = (seg * lax.rsqrt(ms + EPS) * ng_ref[:, g * gw:(g + 1) * gw]).astype(BF16)
        mix = mix + jnp.dot(yn, w_ref[NA_WIDTH + g * gw:NA_WIDTH + (g + 1) * gw, :],
                            preferred_element_type=F32)
    o_ref[...] = x_ref[...] + mod_ref[2] * mix


def _out_proj(attn, y_f, y_b, act, proj, dsk, ng, w_out, x, n_rows, mods_i, j_even):
    tm = 512
    mrow = _mod_row(tm)
    gw = SSD_INNER // SSD_GROUPS
    z_col = COL_Z * LANE // gw
    assert SSD_GROUPS == 2
    return pl.pallas_call(
        _out_proj_kernel,
        grid=(n_rows // tm,),
        in_specs=[
            pl.BlockSpec((tm, NA_WIDTH), lambda t: (t, 0)),
            pl.BlockSpec((tm, SSD_INNER), lambda t: (t, 0)),
            pl.BlockSpec((tm, SSD_INNER), lambda t: (t, 0)),
            pl.BlockSpec((tm, SSD_INNER), lambda t: (t, 0)),
            pl.BlockSpec((tm, gw), lambda t: (t, z_col)),
            pl.BlockSpec((tm, gw), lambda t: (t, z_col + 1)),
            pl.BlockSpec((None, 1, SSD_INNER), lambda t: (j_even, 0, 0)),
            pl.BlockSpec((None, 1, SSD_INNER), lambda t: (j_even, 0, 0)),
            pl.BlockSpec((None, D_MODEL, D_MODEL), lambda t: (j_even, 0, 0)),
            pl.BlockSpec((tm, D_MODEL), lambda t: (t, 0)),
            pl.BlockSpec((None, 3, 1, D_MODEL), lambda t: (mrow(t), 1, 0, 0)),
        ],
        out_specs=pl.BlockSpec((tm, D_MODEL), lambda t: (t, 0)),
        out_shape=jax.ShapeDtypeStruct((n_rows, D_MODEL), F32),
        compiler_params=_cparams(("arbitrary",)),
        name="out_proj",
    )(attn, y_f, y_b, act, proj, proj, dsk, ng, w_out, x, mods_i)


FOURIER_TILE = 512


def _fourier_chan_kernel(x_ref, mod_ref, g_ref, cs_ref, pq_ref):
    h = _mod_norm(x_ref[...], g_ref[...], mod_ref[0], mod_ref[1])
    for g in range(F_GROUPS):
        gs = slice(g * F_GROUP_CH, (g + 1) * F_GROUP_CH)
        pq = jnp.dot(h[:, gs].astype(BF16), cs_ref[...], preferred_element_type=F32)
        pq_ref[0, :, gs] = pq[:, :F_GROUP_CH].astype(BF16)
        pq_ref[1, :, gs] = pq[:, F_GROUP_CH:].astype(BF16)


def _fourier_chan(x, n_rows, mods_i, g, cs):
    tm = FOURIER_TILE
    mrow = _mod_row(tm)
    return pl.pallas_call(
        _fourier_chan_kernel,
        grid=(n_rows // tm,),
        in_specs=[
            pl.BlockSpec((tm, D_MODEL), lambda t: (t, 0)),
            pl.BlockSpec((None, 3, 1, D_MODEL), lambda t: (mrow(t), 1, 0, 0)),
            pl.BlockSpec((1, D_MODEL), lambda t: (0, 0)),
            pl.BlockSpec((F_GROUP_CH, 2 * F_GROUP_CH), lambda t: (0, 0)),
        ],
        out_specs=pl.BlockSpec((None, 2, tm, D_MODEL), lambda t: (t, 0, 0, 0)),
        out_shape=jax.ShapeDtypeStruct((n_rows // tm, 2, tm, D_MODEL), BF16),
        compiler_params=_cparams(("arbitrary",)),
        name="fourier_chan",
    )(x, mods_i, g, cs)


DFT_SPLIT = 64


FOURIER_POS_SUB = 2


def _fourier_pos_kernel(*refs, scale, n_sub):
    u_ref, v_ref = refs[:2]
    pq_refs = refs[2:2 + n_sub]
    w_ref, x_ref, mod_ref, o_ref, acc_scr, cs_scr = refs[2 + n_sub:]
    k = pl.program_id(2)
    tb = v_ref.shape[-1] // n_sub

    @pl.when(k == 0)
    def _():
        acc_scr[...] = jnp.zeros_like(acc_scr)

    part = None
    for i, pq_ref in enumerate(pq_refs):
        ls = slice(i * tb, (i + 1) * tb)
        vr, vi = v_ref[0, :, ls], v_ref[1, :, ls]
        for a in range(u_ref.shape[1]):
            ur, ui = u_ref[0, a:a + 1, ls], u_ref[1, a:a + 1, ls]
            rows = slice(a * DFT_SPLIT, (a + 1) * DFT_SPLIT)
            cs_scr[i, rows, 0:tb] = (ur * vr - ui * vi).astype(BF16)
            cs_scr[i, rows, tb:2 * tb] = (-(ui * vr + ur * vi)).astype(BF16)
        pq = pq_ref[...].reshape(2 * tb, pq_ref.shape[-1])
        d = jnp.dot(cs_scr[i], pq, preferred_element_type=F32)
        part = d if part is None else part + d
    acc_scr[...] += part

    @pl.when(k == pl.num_programs(2) - 1)
    def _():
        f = (acc_scr[...] * scale).astype(BF16)
        o_ref[...] = x_ref[...] + mod_ref[2] * jnp.dot(f, w_ref[...], preferred_element_type=F32)


def _fourier_pos(tabs, pq, w, x, mods_i, j_odd, seq_len, row0, is_ctx):
    u_tab, v_tab = tabs
    tm = tb = min(seq_len, FOURIER_TILE)
    n_sub = min(FOURIER_POS_SUB, seq_len // tb)
    tk = n_sub * tb
    nm = seq_len // tm
    blk0 = row0 // tb
    per_tile = FOURIER_TILE // tb
    scale = float((seq_len * F_GROUP_CH) ** -0.5)

    def pq_spec(i):
        def index(b, m, k):
            blk = blk0 + b * nm + k * n_sub + i
            return blk // per_tile, 0, blk % per_tile, 0
        return pl.BlockSpec((None, 2, tb, D_MODEL), index)

    return pl.pallas_call(
        functools.partial(_fourier_pos_kernel, scale=scale, n_sub=n_sub),
        grid=(BATCH, nm, seq_len // tk),
        in_specs=[
            pl.BlockSpec((2, tm // DFT_SPLIT, tk), lambda b, m, k: (0, m, k)),
            pl.BlockSpec((2, DFT_SPLIT, tk), lambda b, m, k: (0, 0, k)),
            *[pq_spec(i) for i in range(n_sub)],
            pl.BlockSpec((None, D_MODEL, D_MODEL), lambda b, m, k: (j_odd, 0, 0)),
            pl.BlockSpec((tm, D_MODEL), lambda b, m, k: (blk0 + b * nm + m, 0)),
            pl.BlockSpec((None, 3, 1, D_MODEL), lambda b, m, k: (BATCH if is_ctx else b, 1, 0, 0)),
        ],
        out_specs=pl.BlockSpec((tm, D_MODEL), lambda b, m, k: (b * nm + m, 0)),
        out_shape=jax.ShapeDtypeStruct((BATCH * seq_len, D_MODEL), F32),
        scratch_shapes=[pltpu.VMEM((tm, D_MODEL), F32), pltpu.VMEM((n_sub, tm, 2 * tb), BF16)],
        compiler_params=_cparams(("arbitrary", "arbitrary", "arbitrary")),
        name="fourier_pos",
    )(u_tab, v_tab, *([pq] * n_sub), w, x, mods_i)


def _phase_table(freq, n):
    l = jnp.arange(n, dtype=jnp.int32)
    ang = ((freq[:, None] * l[None, :]) % n).astype(F32) * np.float32(2.0 * np.pi / n)
    return jnp.stack([jnp.cos(ang), jnp.sin(ang)])


def _dft_tables(n):
    a = jnp.arange(n // DFT_SPLIT, dtype=jnp.int32) * DFT_SPLIT
    b = jnp.arange(DFT_SPLIT, dtype=jnp.int32)
    return _phase_table(a, n), _phase_table(b, n)


def kernel(x, c, ctx, c_ctx, mod_w, mod_b, norm_g, ffn_w1, ffn_w3, ffn_w2, mix_w_in, mix_w_out, qk_g, rpb,
           conv_w, conv_b, dt_bias, a_log, ssd_d, ssd_norm_g, fourier_w):
    n_even = mix_w_in.shape[0]
    xs = jnp.concatenate([x.reshape(N_LAT, D_MODEL), ctx.reshape(BATCH * CTX_LEN, D_MODEL)], axis=0)
    mods = _modulation(c, c_ctx, mod_w, mod_b)

    w1 = ffn_w1.astype(BF16)
    w3 = ffn_w3.astype(BF16)
    w2 = ffn_w2.astype(BF16)
    w_in = jnp.pad(mix_w_in, ((0, 0), (0, 0), (0, IN_W_PAD - IN_W))).astype(BF16)
    w_out = mix_w_out.astype(BF16)
    w_f = fourier_w.astype(BF16)
    qkg = qk_g.reshape(n_even, 2, 1, NA_HEAD_DIM)
    conv_w8 = jnp.pad(conv_w, ((0, 0), (0, SUBLANE - CONV_K), (0, 0)))
    conv_b2 = conv_b.reshape(n_even, 1, CONV_CH)
    pad_l = LANE - 2 * SSD_HEADS
    dtb = jnp.pad(dt_bias.reshape(n_even, 1, 2 * SSD_HEADS), ((0, 0), (0, 0), (0, pad_l)))
    alog = jnp.pad(a_log.reshape(n_even, 1, 2 * SSD_HEADS), ((0, 0), (0, 0), (0, pad_l)))
    dsk = jnp.repeat(ssd_d, SSD_HEAD_DIM, axis=-1).reshape(n_even, 1, SSD_INNER)
    ng = ssd_norm_g.reshape(n_even, 1, SSD_INNER)
    cos_t, sin_t = _rope_tables()
    chan = _phase_table(jnp.arange(F_GROUP_CH, dtype=jnp.int32), F_GROUP_CH)
    cs_chan = jnp.concatenate([chan[0], chan[1]], axis=1).astype(BF16)
    dft_lat = _dft_tables(SEQ)
    dft_ctx = _dft_tables(CTX_LEN)

    last_ctx = ((DEPTH - 1) // 2) * 2
    for i in range(DEPTH):
        use_ctx = i <= last_ctx
        ctx_out = i < last_ctx
        j = i // 2
        mods_i = mods[i]
        g = norm_g[i].reshape(3, 1, D_MODEL)
        rows_in = N_TOK if use_ctx else N_LAT
        rows_out = N_TOK if ctx_out else N_LAT
        xs = _ffn(xs, rows_in, mods_i, 0, g[0], w1, w3, w2, i, 0)
        if i % 2 == 0:
            assert use_ctx, "even mixer layers read the context keys and SSD states"
            proj = _in_proj(xs, rows_in, mods_i, g[1], w_in, j)
            bias = _attention_bias(rpb[j])
            attn, attn_ctx = _na_attention(proj, cos_t, sin_t, qkg[j], bias)
            if ctx_out:
                attn = jnp.concatenate([attn, attn_ctx], axis=0)
            act = _conv_silu(proj, conv_w8, conv_b2, j)
            y_f, y_b = _ssd_scan(act, proj, dtb, alog, j)
            xs = _out_proj(attn, y_f, y_b, act, proj, dsk, ng, w_out, xs, rows_out, mods_i, j)
        else:
            pq = _fourier_chan(xs, rows_out, mods_i, g[1], cs_chan)
            new = _fourier_pos(dft_lat, pq, w_f, xs, mods_i, j, SEQ, 0, False)
            if ctx_out:
                new_ctx = _fourier_pos(dft_ctx, pq, w_f, xs, mods_i, j, CTX_LEN, N_LAT, True)
                new = jnp.concatenate([new, new_ctx], axis=0)
            xs = new
        xs = _ffn(xs, rows_out, mods_i, 2, g[2], w1, w3, w2, i, 1)
    return xs[:N_LAT].reshape(BATCH, SEQ, D_MODEL)
```

```python
import functools

import numpy as np
import jax
import jax.numpy as jnp
from jax import lax
from jax.experimental import pallas as pl
from jax.experimental.pallas import tpu as pltpu

F32 = jnp.float32
BF16 = jnp.bfloat16

D_MODEL = 2048
BATCH = 2
SEQ = 4096
DEPTH = 4
GRID_W = 64
GRID_ROWS = SEQ // GRID_W
CTX_LEN = 256
N_LAT = BATCH * SEQ
N_TOK = N_LAT + BATCH * CTX_LEN
NA_HEADS = 8
NA_HEAD_DIM = 128
NA_WIDTH = NA_HEADS * NA_HEAD_DIM
WIN_R = 8
WIN_C = 16
ROPE_BASE = 10000.0
SSD_HEADS = 16
SSD_HEAD_DIM = 64
SSD_INNER = SSD_HEADS * SSD_HEAD_DIM
SSD_GROUPS = 2
SSD_STATE = 128
SSD_CHUNK = 128
CONV_K = 5
CONV_CH = SSD_INNER + 2 * SSD_GROUPS * SSD_STATE
IN_W = 3 * NA_WIDTH + SSD_INNER + CONV_CH + 2 * SSD_HEADS
LANE = 128
SUBLANE = 8
IN_W_PAD = ((IN_W + LANE - 1) // LANE) * LANE
F_GROUPS = 4
F_GROUP_CH = D_MODEL // F_GROUPS
D_FF = 5632
N_MOD = 9
EPS = 1e-6
ATTN_SCALE = NA_HEAD_DIM ** -0.5

COL_Q = 0
COL_K = NA_WIDTH // LANE
COL_V = 2 * NA_WIDTH // LANE
COL_Z = 3 * NA_WIDTH // LANE
COL_XBC = (3 * NA_WIDTH + SSD_INNER) // LANE
COL_DT = (3 * NA_WIDTH + SSD_INNER + CONV_CH) // LANE

Q_ROWS = 32
Q_BLK = Q_ROWS * GRID_W
Q_SPLIT = 8
PART_ROWS = Q_ROWS // Q_SPLIT
PART_Q = PART_ROWS * GRID_W
K_ROWS = 12
K_BLK = K_ROWS * GRID_W
N_QBLK = GRID_ROWS // Q_ROWS

VMEM_LIMIT = 56 * 1024 * 1024


def _cparams(sem, vmem_limit=VMEM_LIMIT):
    return pltpu.CompilerParams(dimension_semantics=sem, vmem_limit_bytes=vmem_limit)


def _sigmoid(x):
    return 1.0 / (1.0 + jnp.exp(-x))


def _silu(x):
    return x * _sigmoid(x)


def _mod_norm(x, g, shift, scale):
    ms = jnp.mean(x * x, axis=-1, keepdims=True)
    y = x * lax.rsqrt(ms + EPS) * g
    return y * (1.0 + scale) + shift


def _mod_row(tile_rows):
    def f(t):
        return jnp.minimum((t * tile_rows) // SEQ, BATCH)
    return f


def _mod_kernel(c_ref, w_ref, b_ref, o_ref):
    s = _silu(c_ref[...]).astype(BF16)
    o_ref[...] = jnp.dot(s, w_ref[...].astype(BF16), preferred_element_type=F32) + b_ref[...]


def _modulation(c, c_ctx, mod_w, mod_b):
    rows = SUBLANE
    cvec = jnp.concatenate([c, c_ctx[None, :], jnp.zeros((rows - BATCH - 1, D_MODEL), F32)], axis=0)
    n = N_MOD * D_MODEL
    tn = 2048
    out = pl.pallas_call(
        _mod_kernel,
        grid=(DEPTH, n // tn),
        in_specs=[
            pl.BlockSpec((rows, D_MODEL), lambda i, j: (0, 0)),
            pl.BlockSpec((None, D_MODEL, tn), lambda i, j: (i, 0, j)),
            pl.BlockSpec((None, 1, tn), lambda i, j: (i, 0, j)),
        ],
        out_specs=pl.BlockSpec((None, rows, tn), lambda i, j: (i, 0, j)),
        out_shape=jax.ShapeDtypeStruct((DEPTH, rows, n), F32),
        compiler_params=_cparams(("arbitrary", "arbitrary")),
        name="modulation",
    )(cvec, mod_w, mod_b.reshape(DEPTH, 1, n))
    return out[:, :BATCH + 1].reshape(DEPTH, BATCH + 1, N_MOD, 1, D_MODEL)


FFN_TM = 1024
FFN_TF = 512
FFN_VMEM_LIMIT = 63 * 1024 * 1024


def _ffn_accumulate(h_scr, w1_ref, w3_ref, w2_ref, o_ref, rows):
    h = h_scr[0:rows, :]
    a = jnp.dot(h, w1_ref[...], preferred_element_type=F32)
    b = jnp.dot(h, w3_ref[...], preferred_element_type=F32)
    u = (_silu(a) * b).astype(BF16)
    o_ref[0:rows, :] += jnp.dot(u, w2_ref[...], preferred_element_type=F32)


def _ffn_kernel(x_ref, mod_ref, g_ref, w1_ref, w3_ref, w2_ref, o_ref, h_scr, *, tail_rows):
    t = pl.program_id(0)
    j = pl.program_id(1)
    last_t = pl.num_programs(0) - 1

    @pl.when(j == 0)
    def _():
        h_scr[...] = _mod_norm(x_ref[...], g_ref[...], mod_ref[0], mod_ref[1]).astype(BF16)
        o_ref[...] = jnp.zeros_like(o_ref)

    if tail_rows == FFN_TM:
        _ffn_accumulate(h_scr, w1_ref, w3_ref, w2_ref, o_ref, FFN_TM)
    else:
        @pl.when(t < last_t)
        def _():
            _ffn_accumulate(h_scr, w1_ref, w3_ref, w2_ref, o_ref, FFN_TM)

        @pl.when(t == last_t)
        def _():
            _ffn_accumulate(h_scr, w1_ref, w3_ref, w2_ref, o_ref, tail_rows)

    @pl.when(j == pl.num_programs(1) - 1)
    def _():
        o_ref[...] = x_ref[...] + 0.5 * mod_ref[2] * o_ref[...]


def _ffn(x, n_rows, mods_i, sub, g, w1, w3, w2, layer, which):
    tm, tf = FFN_TM, FFN_TF
    mrow = _mod_row(tm)
    nt = pl.cdiv(n_rows, tm)
    tail_rows = n_rows - (nt - 1) * tm
    return pl.pallas_call(
        functools.partial(_ffn_kernel, tail_rows=tail_rows),
        grid=(nt, D_FF // tf),
        in_specs=[
            pl.BlockSpec((tm, D_MODEL), lambda t, j: (t, 0)),
            pl.BlockSpec((None, 3, 1, D_MODEL), lambda t, j: (mrow(t), sub, 0, 0)),
            pl.BlockSpec((1, D_MODEL), lambda t, j: (0, 0)),
            pl.BlockSpec((None, None, D_MODEL, tf), lambda t, j: (layer, which, 0, j)),
            pl.BlockSpec((None, None, D_MODEL, tf), lambda t, j: (layer, which, 0, j)),
            pl.BlockSpec((None, None, tf, D_MODEL), lambda t, j: (layer, which, j, 0)),
        ],
        out_specs=pl.BlockSpec((tm, D_MODEL), lambda t, j: (t, 0)),
        out_shape=jax.ShapeDtypeStruct((n_rows, D_MODEL), F32),
        scratch_shapes=[pltpu.VMEM((tm, D_MODEL), BF16)],
        compiler_params=_cparams(("arbitrary", "arbitrary"), FFN_VMEM_LIMIT),
        name="ffn",
    )(x, mods_i, g, w1, w3, w2)


IN_TM = 1024
IN_TN = 1920


def _norm_matmul_kernel(x_ref, mod_ref, g_ref, w_ref, o_ref, h_scr, *, tail_rows):
    t = pl.program_id(0)
    last_t = pl.num_programs(0) - 1

    @pl.when(pl.program_id(1) == 0)
    def _():
        h_scr[...] = _mod_norm(x_ref[...], g_ref[...], mod_ref[0], mod_ref[1]).astype(BF16)

    if tail_rows == IN_TM:
        o_ref[...] = jnp.dot(h_scr[...], w_ref[...], preferred_element_type=F32)
    else:
        @pl.when(t < last_t)
        def _():
            o_ref[...] = jnp.dot(h_scr[...], w_ref[...], preferred_element_type=F32)

        @pl.when(t == last_t)
        def _():
            o_ref[0:tail_rows, :] = jnp.dot(h_scr[0:tail_rows, :], w_ref[...], preferred_element_type=F32)


def _in_proj(x, n_rows, mods_i, g, w_in, j_even):
    tm, tn = IN_TM, IN_TN
    mrow = _mod_row(tm)
    nt = pl.cdiv(n_rows, tm)
    return pl.pallas_call(
        functools.partial(_norm_matmul_kernel, tail_rows=n_rows - (nt - 1) * tm),
        grid=(nt, IN_W_PAD // tn),
        in_specs=[
            pl.BlockSpec((tm, D_MODEL), lambda t, j: (t, 0)),
            pl.BlockSpec((None, 3, 1, D_MODEL), lambda t, j: (mrow(t), 1, 0, 0)),
            pl.BlockSpec((1, D_MODEL), lambda t, j: (0, 0)),
            pl.BlockSpec((None, D_MODEL, tn), lambda t, j: (j_even, 0, j)),
        ],
        out_specs=pl.BlockSpec((tm, tn), lambda t, j: (t, j)),
        out_shape=jax.ShapeDtypeStruct((n_rows, IN_W_PAD), F32),
        scratch_shapes=[pltpu.VMEM((tm, D_MODEL), BF16)],
        compiler_params=_cparams(("arbitrary", "arbitrary"), FFN_VMEM_LIMIT),
        name="in_proj",
    )(x, mods_i, g, w_in)


def _head_rms(x, g):
    ms = jnp.mean(x * x, axis=-1, keepdims=True)
    return x * lax.rsqrt(ms + EPS) * g


def _rope(x, cos, sin):
    lane = lax.broadcasted_iota(jnp.int32, x.shape, 1)
    first_half = (lane & (NA_HEAD_DIM // 4)) == 0
    partner = jnp.where(first_half, pltpu.roll(x, NA_HEAD_DIM - NA_HEAD_DIM // 4, 1),
                        pltpu.roll(x, NA_HEAD_DIM // 4, 1))
    return x * cos + partner * sin


def _dot_nt(a, b):
    return lax.dot_general(a, b, (((1,), (1,)), ((), ())), preferred_element_type=F32)


def _softmax_pv(scores, values):
    m = functools.reduce(jnp.maximum, [jnp.max(s, axis=-1, keepdims=True) for s in scores])
    ps = [jnp.exp(s - m) for s in scores]
    denom = functools.reduce(jnp.add, [jnp.sum(p, axis=-1, keepdims=True) for p in ps])
    o = functools.reduce(jnp.add, [jnp.dot(p.astype(BF16), v, preferred_element_type=F32)
                                   for p, v in zip(ps, values)])
    return o / denom


def _part_key_row(j, part):
    return jnp.clip(Q_ROWS * j + PART_ROWS * part - WIN_R // 2, 0, GRID_ROWS - K_ROWS)


def _na_kernel(*refs):
    (q_ref, k_ref, v_ref, qc_ref, kc_ref, vc_ref, cos_ref, sin_ref, qkg_ref), refs = refs[:9], refs[9:]
    bias_refs, (o_ref, oc_ref, k_scr, v_scr, kc_scr, vc_scr) = refs[:Q_SPLIT], refs[Q_SPLIT:]
    j = pl.program_id(2)

    @pl.when(j == 0)
    def _():
        kn = _head_rms(k_ref[...], qkg_ref[1])
        k_scr[...] = _rope(kn, cos_ref[...], sin_ref[...]).astype(BF16)
        v_scr[...] = v_ref[...].astype(BF16)
        kcn = _head_rms(kc_ref[...], qkg_ref[1]).astype(BF16)
        vcb = vc_ref[...].astype(BF16)
        kc_scr[...] = kcn
        vc_scr[...] = vcb
        qcn = _head_rms(qc_ref[...], qkg_ref[0]).astype(BF16)
        oc_ref[...] = _softmax_pv([_dot_nt(qcn, kcn) * ATTN_SCALE], [vcb]).astype(oc_ref.dtype)

    for part in range(Q_SPLIT):
        rows = pl.ds(part * PART_Q, PART_Q)
        q0 = pl.multiple_of(j * Q_BLK + part * PART_Q, PART_Q)
        k0 = pl.multiple_of(_part_key_row(j, part) * GRID_W, GRID_W)
        kw = k_scr[pl.ds(k0, K_BLK), :]
        vw = v_scr[pl.ds(k0, K_BLK), :]
        qn = _head_rms(q_ref[rows, :], qkg_ref[0])
        qr = _rope(qn, cos_ref[pl.ds(q0, PART_Q), :], sin_ref[pl.ds(q0, PART_Q), :]).astype(BF16)
        s_win = _dot_nt(qr, kw) * ATTN_SCALE + bias_refs[part][...]
        s_ctx = _dot_nt(qn.astype(BF16), kc_scr[...]) * ATTN_SCALE
        o_ref[rows, :] = _softmax_pv([s_win, s_ctx], [vw, vc_scr[...]]).astype(o_ref.dtype)


BIAS_PARTS = ((0, 0), (N_QBLK // 2, 0), (N_QBLK - 1, Q_SPLIT - 1))


def _bias_pattern(j, part):
    first = (j == 0) & (part == 0)
    last = (j == N_QBLK - 1) & (part == Q_SPLIT - 1)
    return jnp.where(first, 0, jnp.where(last, 2, 1))


def _na_attention(proj, cos_t, sin_t, qkg, bias):
    lat_blk = SEQ // Q_BLK
    ctx_blk0 = N_LAT // CTX_LEN
    hd = NA_HEAD_DIM
    bias_specs = [pl.BlockSpec((None, None, PART_Q, K_BLK), lambda b, h, j, p=p: (h, _bias_pattern(j, p), 0, 0))
                  for p in range(Q_SPLIT)]
    return pl.pallas_call(
        _na_kernel,
        grid=(BATCH, NA_HEADS, N_QBLK),
        in_specs=[
            pl.BlockSpec((Q_BLK, hd), lambda b, h, j: (b * lat_blk + j, COL_Q + h)),
            pl.BlockSpec((SEQ, hd), lambda b, h, j: (b, COL_K + h)),
            pl.BlockSpec((SEQ, hd), lambda b, h, j: (b, COL_V + h)),
            pl.BlockSpec((CTX_LEN, hd), lambda b, h, j: (ctx_blk0 + b, COL_Q + h)),
            pl.BlockSpec((CTX_LEN, hd), lambda b, h, j: (ctx_blk0 + b, COL_K + h)),
            pl.BlockSpec((CTX_LEN, hd), lambda b, h, j: (ctx_blk0 + b, COL_V + h)),
            pl.BlockSpec((SEQ, hd), lambda b, h, j: (0, 0)),
            pl.BlockSpec((SEQ, hd), lambda b, h, j: (0, 0)),
            pl.BlockSpec((2, 1, hd), lambda b, h, j: (0, 0, 0)),
        ] + bias_specs,
        out_specs=[pl.BlockSpec((Q_BLK, hd), lambda b, h, j: (b * lat_blk + j, h)),
                   pl.BlockSpec((CTX_LEN, hd), lambda b, h, j: (b, h))],
        out_shape=[jax.ShapeDtypeStruct((N_LAT, NA_WIDTH), BF16),
                   jax.ShapeDtypeStruct((BATCH * CTX_LEN, NA_WIDTH), BF16)],
        scratch_shapes=[pltpu.VMEM((SEQ, hd), BF16), pltpu.VMEM((SEQ, hd), BF16),
                        pltpu.VMEM((CTX_LEN, hd), BF16), pltpu.VMEM((CTX_LEN, hd), BF16)],
        compiler_params=_cparams(("arbitrary", "arbitrary", "arbitrary")),
        name="na_attention",
    )(proj, proj, proj, proj, proj, proj, cos_t, sin_t, qkg, *([bias] * Q_SPLIT))


def _part_geometry(jb, part):
    rows = Q_ROWS * jb + PART_ROWS * part + np.arange(PART_ROWS)
    k_first = int(np.clip(rows[0] - WIN_R // 2, 0, GRID_ROWS - K_ROWS))
    r0 = np.clip(rows - WIN_R // 2, 0, GRID_ROWS - WIN_R)
    return rows, k_first, r0


def _attention_bias(rpb_j):
    for jb in range(N_QBLK):
        for part in range(Q_SPLIT):
            rows, k_first, r0 = _part_geometry(jb, part)
            pat = 0 if (jb, part) == (0, 0) else 2 if (jb, part) == (N_QBLK - 1, Q_SPLIT - 1) else 1
            rows_p, k_first_p, r0_p = _part_geometry(*BIAS_PARTS[pat])
            assert k_first - rows[0] == k_first_p - rows_p[0] and np.array_equal(r0 - rows, r0_p - rows_p)
    n_dr = 2 * WIN_R - 1
    qcol = np.arange(GRID_W)
    c0 = np.clip(qcol - WIN_C // 2, 0, GRID_W - WIN_C)
    col_ok = (qcol[None, :] >= c0[:, None]) & (qcol[None, :] < c0[:, None] + WIN_C)
    r = rpb_j.astype(F32)
    edge = GRID_W - WIN_C
    ep = jnp.concatenate([jnp.repeat(r[..., :1], edge, axis=-1), r, jnp.repeat(r[..., -1:], edge, axis=-1)],
                         axis=-1)
    t1 = jnp.stack([ep[..., GRID_W - 1 - qc:2 * GRID_W - 1 - qc] for qc in range(GRID_W)], axis=2)
    t1 = jnp.where(jnp.asarray(col_ok)[None, None], t1, -jnp.inf)
    band = t1.transpose(0, 2, 1, 3).reshape(NA_HEADS, GRID_W, n_dr * GRID_W)
    pad_w = K_BLK
    band = jnp.pad(band, ((0, 0), (0, 0), (pad_w, pad_w)), constant_values=-jnp.inf)
    strips, masks = [], []
    for jb, part in BIAS_PARTS:
        rows, k_first, r0 = _part_geometry(jb, part)
        kr = k_first + np.arange(K_ROWS)
        for row, row0 in zip(rows, r0):
            off = pad_w + (k_first - int(row) + WIN_R - 1) * GRID_W
            assert 0 <= off and off + K_BLK <= band.shape[-1]
            strips.append(band[:, :, off:off + K_BLK])
            masks.append(np.repeat((kr >= row0) & (kr < row0 + WIN_R), GRID_W))
    vals = jnp.stack(strips, axis=1)
    mask = np.stack(masks)[None, :, None, :]
    bias = jnp.where(jnp.asarray(mask), vals, -jnp.inf)
    return bias.reshape(NA_HEADS, len(BIAS_PARTS), PART_Q, K_BLK)


def _rope_tables():
    quarter = NA_HEAD_DIM // 4
    inv_freq = ROPE_BASE ** (-jnp.arange(quarter, dtype=F32) / quarter)
    t = jnp.arange(SEQ)
    ang_r = (t // GRID_W).astype(F32)[:, None] * inv_freq[None, :]
    ang_c = (t % GRID_W).astype(F32)[:, None] * inv_freq[None, :]
    cos_t = jnp.concatenate([jnp.cos(ang_r), jnp.cos(ang_r), jnp.cos(ang_c), jnp.cos(ang_c)], axis=-1)
    sin_t = jnp.concatenate([-jnp.sin(ang_r), jnp.sin(ang_r), -jnp.sin(ang_c), jnp.sin(ang_c)], axis=-1)
    return cos_t, sin_t


CONV_HALO = SUBLANE
CONV_BLK = 256


CONV_CW = 512


def _conv_kernel(*refs):
    nc = CONV_CH // CONV_CW
    x_refs, prev_refs, next_refs = refs[:nc], refs[nc:2 * nc], refs[2 * nc:3 * nc]
    w_ref, b_ref, o_ref, pad_scr = refs[3 * nc:]
    t = pl.program_id(0)
    n_lat_blk = N_LAT // CONV_BLK
    is_ctx = t >= n_lat_blk
    seq_blks = jnp.where(is_ctx, CTX_LEN // CONV_BLK, SEQ // CONV_BLK)
    pos = jnp.where(is_ctx, t - n_lat_blk, t) % seq_blks
    for c in range(nc):
        cs = slice(c * CONV_CW, (c + 1) * CONV_CW)
        pad_scr[0:CONV_HALO, :] = jnp.where(pos == 0, 0.0, prev_refs[c][...])
        pad_scr[CONV_HALO:CONV_HALO + CONV_BLK, :] = x_refs[c][...]
        pad_scr[CONV_HALO + CONV_BLK:2 * CONV_HALO + CONV_BLK, :] = jnp.where(pos == seq_blks - 1, 0.0,
                                                                               next_refs[c][...])
        xp = pad_scr[...]
        acc = jnp.zeros((CONV_BLK, CONV_CW), F32) + b_ref[:, cs]
        for k in range(CONV_K):
            lo = CONV_HALO - CONV_K // 2 + k
            acc = acc + w_ref[k:k + 1, cs] * xp[lo:lo + CONV_BLK, :]
        o_ref[:, cs] = _silu(acc)


def _conv_silu(proj, conv_w8, conv_b2, j_even):
    n_rows = proj.shape[0]
    nc = CONV_CH // CONV_CW
    col0 = COL_XBC * LANE // CONV_CW
    halo_per_blk = CONV_BLK // CONV_HALO
    last_halo = n_rows // CONV_HALO - 1
    x_specs = [pl.BlockSpec((CONV_BLK, CONV_CW), lambda t, c=c: (t, col0 + c)) for c in range(nc)]
    prev_specs = [pl.BlockSpec((CONV_HALO, CONV_CW),
                               lambda t, c=c: (jnp.maximum(t * halo_per_blk - 1, 0), col0 + c)) for c in range(nc)]
    next_specs = [pl.BlockSpec((CONV_HALO, CONV_CW),
                               lambda t, c=c: (jnp.minimum((t + 1) * halo_per_blk, last_halo), col0 + c))
                  for c in range(nc)]
    return pl.pallas_call(
        _conv_kernel,
        grid=(n_rows // CONV_BLK,),
        in_specs=x_specs + prev_specs + next_specs + [
            pl.BlockSpec((None, SUBLANE, CONV_CH), lambda t: (j_even, 0, 0)),
            pl.BlockSpec((None, 1, CONV_CH), lambda t: (j_even, 0, 0)),
        ],
        out_specs=pl.BlockSpec((CONV_BLK, CONV_CH), lambda t: (t, 0)),
        out_shape=jax.ShapeDtypeStruct((n_rows, CONV_CH), F32),
        scratch_shapes=[pltpu.VMEM((CONV_BLK + 2 * CONV_HALO, CONV_CW), F32)],
        compiler_params=_cparams(("arbitrary",)),
        name="conv_silu",
    )(*([proj] * (3 * nc)), conv_w8, conv_b2)


def _softplus(x):
    return jnp.maximum(x, 0.0) + jnp.log1p(jnp.exp(-jnp.abs(x)))


def _dot_select(a, sel):
    sel = sel.astype(BF16)
    hi = a.astype(BF16)
    r1 = a - hi.astype(F32)
    mid = r1.astype(BF16)
    lo = (r1 - mid.astype(F32)).astype(BF16)
    return (jnp.dot(hi, sel, preferred_element_type=F32) + jnp.dot(mid, sel, preferred_element_type=F32)
            + jnp.dot(lo, sel, preferred_element_type=F32))


def _select_dot(sel, a):
    sel = sel.astype(BF16)
    hi = a.astype(BF16)
    r1 = a - hi.astype(F32)
    mid = r1.astype(BF16)
    lo = (r1 - mid.astype(F32)).astype(BF16)
    return (jnp.dot(sel, hi, preferred_element_type=F32) + jnp.dot(sel, mid, preferred_element_type=F32)
            + jnp.dot(sel, lo, preferred_element_type=F32))


def _ssd_direction(xs_ref, b_ref, c_ref, dt_ref, dtb_ref, alog_ref, y_ref, h_scr, lane0, forward):
    q = SSD_CHUNK
    gw = SSD_INNER // SSD_GROUPS
    dt = _softplus(dt_ref[...] + dtb_ref[...])
    a = dt * (-jnp.exp(alog_ref[...]))
    ri = lax.broadcasted_iota(jnp.int32, (q, q), 0)
    ci = lax.broadcasted_iota(jnp.int32, (q, q), 1)
    tri = (ci <= ri) if forward else (ci >= ri)
    a_cum = _select_dot(tri.astype(F32), a)
    a_cum_t = a_cum.T
    er = lax.broadcasted_iota(jnp.int32, (LANE, SSD_INNER), 0)
    ec = lax.broadcasted_iota(jnp.int32, (LANE, SSD_INNER), 1)
    expand = ((er - lane0) == (ec // SSD_HEAD_DIM)).astype(F32)
    dt_e = _dot_select(dt, expand)
    ac_e = _dot_select(a_cum, expand)
    end = q - 1 if forward else 0
    a_end_e = ac_e[end:end + 1, :]
    xdt = xs_ref[...] * dt_e
    in_decay = jnp.exp(ac_e)
    out_decay = jnp.exp(a_end_e - ac_e)
    state_decay = jnp.exp(a_end_e)
    lane = lax.broadcasted_iota(jnp.int32, (q, LANE), 1)
    for g in range(SSD_GROUPS):
        gs = slice(g * gw, (g + 1) * gw)
        bg_t = b_ref[:, g * SSD_STATE:(g + 1) * SSD_STATE].T.astype(BF16)
        cg = c_ref[:, g * SSD_STATE:(g + 1) * SSD_STATE].astype(BF16)
        cb = jnp.dot(cg, bg_t, preferred_element_type=F32)
        h_t = h_scr[:, gs]
        y_inter = jnp.dot(cg, h_t.astype(BF16), preferred_element_type=F32) * in_decay[:, gs]
        for pair in range(gw // LANE):
            cs = slice(g * gw + pair * LANE, g * gw + (pair + 1) * LANE)
            x_pair = xdt[:, cs].astype(BF16)
            res = []
            for sub in range(LANE // SSD_HEAD_DIM):
                hl = lane0 + (g * gw + pair * LANE) // SSD_HEAD_DIM + sub
                seg = a_cum[:, hl:hl + 1] - a_cum_t[hl:hl + 1, :]
                decay = jnp.exp(jnp.where(tri, seg, -jnp.inf))
                res.append(jnp.dot((cb * decay).astype(BF16), x_pair, preferred_element_type=F32))
            y_pair = jnp.where(lane < SSD_HEAD_DIM, res[0], res[1])
            y_ref[:, cs] = y_pair + y_inter[:, pair * LANE:(pair + 1) * LANE]
        x_out = (xdt[:, gs] * out_decay[:, gs]).astype(BF16)
        h_scr[:, gs] = state_decay[:, gs] * h_t + jnp.dot(bg_t, x_out, preferred_element_type=F32)


def _ssd_kernel(xs_f, b_f, c_f, dt_f, xs_b, b_b, c_b, dt_b, dtb_ref, alog_ref, yf_ref, yb_ref,
                hf_scr, hb_scr):
    @pl.when(pl.program_id(1) == 0)
    def _():
        hf_scr[...] = jnp.zeros_like(hf_scr)
        hb_scr[...] = jnp.zeros_like(hb_scr)

    _ssd_direction(xs_f, b_f, c_f, dt_f, dtb_ref, alog_ref, yf_ref, hf_scr, 0, True)
    _ssd_direction(xs_b, b_b, c_b, dt_b, dtb_ref, alog_ref, yb_ref, hb_scr, SSD_HEADS, False)


def _ssd_scan(act, proj, dtb, alog, j_even):
    q = SSD_CHUNK
    n_rows = act.shape[0]
    n_ctx = CTX_LEN // q
    n_lat = SEQ // q
    ctx_blk0 = N_LAT // q

    def fwd_blk(b, s):
        return jnp.where(s < n_ctx, ctx_blk0 + b * n_ctx + s, b * n_lat + (s - n_ctx))

    def bwd_blk(b, s):
        return jnp.where(s < n_ctx, ctx_blk0 + b * n_ctx + (n_ctx - 1 - s),
                         b * n_lat + (n_lat - 1 - (s - n_ctx)))

    gn = SSD_GROUPS * SSD_STATE
    b_col = SSD_INNER // gn
    c_col = b_col + 1

    def specs(blk):
        return [
            pl.BlockSpec((q, SSD_INNER), lambda b, s: (blk(b, s), 0)),
            pl.BlockSpec((q, gn), lambda b, s: (blk(b, s), b_col)),
            pl.BlockSpec((q, gn), lambda b, s: (blk(b, s), c_col)),
            pl.BlockSpec((q, LANE), lambda b, s: (blk(b, s), COL_DT)),
        ]

    small = pl.BlockSpec((None, 1, LANE), lambda b, s: (j_even, 0, 0))
    return pl.pallas_call(
        _ssd_kernel,
        grid=(BATCH, n_ctx + n_lat),
        in_specs=specs(fwd_blk) + specs(bwd_blk) + [small, small],
        out_specs=[pl.BlockSpec((q, SSD_INNER), lambda b, s: (fwd_blk(b, s), 0)),
                   pl.BlockSpec((q, SSD_INNER), lambda b, s: (bwd_blk(b, s), 0))],
        out_shape=[jax.ShapeDtypeStruct((n_rows, SSD_INNER), F32)] * 2,
        scratch_shapes=[pltpu.VMEM((SSD_STATE, SSD_INNER), F32)] * 2,
        compiler_params=_cparams(("arbitrary", "arbitrary")),
        name="ssd_scan",
    )(act, act, act, proj, act, act, act, proj, dtb, alog)


def _out_proj_kernel(attn_ref, yf_ref, yb_ref, xs_ref, z0_ref, z1_ref, dsk_ref, ng_ref, w_ref, x_ref, mod_ref,
                     o_ref):
    mix = jnp.dot(attn_ref[...], w_ref[0:NA_WIDTH, :], preferred_element_type=F32)
    gw = SSD_INNER // SSD_GROUPS
    for g, z_ref in enumerate((z0_ref, z1_ref)):
        gs = slice(g * gw, (g + 1) * gw)
        y = yf_ref[:, gs] + yb_ref[:, gs] + dsk_ref[:, gs] * xs_ref[:, gs]
        seg = y * _silu(z_ref[...])
        ms = jnp.mean(seg * seg, axis=-1, keepdims=True)
        yn = (seg * lax.rsqrt(ms + EPS) * ng_ref[:, g * gw:(g + 1) * gw]).astype(BF16)
        mix = mix + jnp.dot(yn, w_ref[NA_WIDTH + g * gw:NA_WIDTH + (g + 1) * gw, :],
                            preferred_element_type=F32)
    o_ref[...] = x_ref[...] + mod_ref[2] * mix


def _out_proj(attn, y_f, y_b, act, proj, dsk, ng, w_out, x, n_rows, mods_i, j_even):
    tm = 512
    mrow = _mod_row(tm)
    gw = SSD_INNER // SSD_GROUPS
    z_col = COL_Z * LANE // gw
    assert SSD_GROUPS == 2
    return pl.pallas_call(
        _out_proj_kernel,
        grid=(n_rows // tm,),
        in_specs=[
            pl.BlockSpec((tm, NA_WIDTH), lambda t: (t, 0)),
            pl.BlockSpec((tm, SSD_INNER), lambda t: (t, 0)),
            pl.BlockSpec((tm, SSD_INNER), lambda t: (t, 0)),
            pl.BlockSpec((tm, SSD_INNER), lambda t: (t, 0)),
            pl.BlockSpec((tm, gw), lambda t: (t, z_col)),
            pl.BlockSpec((tm, gw), lambda t: (t, z_col + 1)),
            pl.BlockSpec((None, 1, SSD_INNER), lambda t: (j_even, 0, 0)),
            pl.BlockSpec((None, 1, SSD_INNER), lambda t: (j_even, 0, 0)),
            pl.BlockSpec((None, D_MODEL, D_MODEL), lambda t: (j_even, 0, 0)),
            pl.BlockSpec((tm, D_MODEL), lambda t: (t, 0)),
            pl.BlockSpec((None, 3, 1, D_MODEL), lambda t: (mrow(t), 1, 0, 0)),
        ],
        out_specs=pl.BlockSpec((tm, D_MODEL), lambda t: (t, 0)),
        out_shape=jax.ShapeDtypeStruct((n_rows, D_MODEL), F32),
        compiler_params=_cparams(("arbitrary",)),
        name="out_proj",
    )(attn, y_f, y_b, act, proj, proj, dsk, ng, w_out, x, mods_i)


FOURIER_TILE = 512


def _fourier_chan_kernel(x_ref, mod_ref, g_ref, cs_ref, pq_ref):
    h = _mod_norm(x_ref[...], g_ref[...], mod_ref[0], mod_ref[1])
    for g in range(F_GROUPS):
        gs = slice(g * F_GROUP_CH, (g + 1) * F_GROUP_CH)
        pq = jnp.dot(h[:, gs].astype(BF16), cs_ref[...], preferred_element_type=F32)
        pq_ref[0, :, gs] = pq[:, :F_GROUP_CH].astype(BF16)
        pq_ref[1, :, gs] = pq[:, F_GROUP_CH:].astype(BF16)


def _fourier_chan(x, n_rows, mods_i, g, cs):
    tm = FOURIER_TILE
    mrow = _mod_row(tm)
    return pl.pallas_call(
        _fourier_chan_kernel,
        grid=(n_rows // tm,),
        in_specs=[
            pl.BlockSpec((tm, D_MODEL), lambda t: (t, 0)),
            pl.BlockSpec((None, 3, 1, D_MODEL), lambda t: (mrow(t), 1, 0, 0)),
            pl.BlockSpec((1, D_MODEL), lambda t: (0, 0)),
            pl.BlockSpec((F_GROUP_CH, 2 * F_GROUP_CH), lambda t: (0, 0)),
        ],
        out_specs=pl.BlockSpec((None, 2, tm, D_MODEL), lambda t: (t, 0, 0, 0)),
        out_shape=jax.ShapeDtypeStruct((n_rows // tm, 2, tm, D_MODEL), BF16),
        compiler_params=_cparams(("arbitrary",)),
        name="fourier_chan",
    )(x, mods_i, g, cs)


DFT_SPLIT = 64


FOURIER_POS_SUB = 2


def _fourier_pos_kernel(*refs, scale, n_sub):
    u_ref, v_ref = refs[:2]
    pq_refs = refs[2:2 + n_sub]
    w_ref, x_ref, mod_ref, o_ref, acc_scr, cs_scr = refs[2 + n_sub:]
    k = pl.program_id(2)
    tb = v_ref.shape[-1] // n_sub

    @pl.when(k == 0)
    def _():
        acc_scr[...] = jnp.zeros_like(acc_scr)

    part = None
    for i, pq_ref in enumerate(pq_refs):
        ls = slice(i * tb, (i + 1) * tb)
        vr, vi = v_ref[0, :, ls], v_ref[1, :, ls]
        for a in range(u_ref.shape[1]):
            ur, ui = u_ref[0, a:a + 1, ls], u_ref[1, a:a + 1, ls]
            rows = slice(a * DFT_SPLIT, (a + 1) * DFT_SPLIT)
            cs_scr[i, rows, 0:tb] = (ur * vr - ui * vi).astype(BF16)
            cs_scr[i, rows, tb:2 * tb] = (-(ui * vr + ur * vi)).astype(BF16)
        pq = pq_ref[...].reshape(2 * tb, pq_ref.shape[-1])
        d = jnp.dot(cs_scr[i], pq, preferred_element_type=F32)
        part = d if part is None else part + d
    acc_scr[...] += part

    @pl.when(k == pl.num_programs(2) - 1)
    def _():
        f = (acc_scr[...] * scale).astype(BF16)
        o_ref[...] = x_ref[...] + mod_ref[2] * jnp.dot(f, w_ref[...], preferred_element_type=F32)


def _fourier_pos(tabs, pq, w, x, mods_i, j_odd, seq_len, row0, is_ctx):
    u_tab, v_tab = tabs
    tm = tb = min(seq_len, FOURIER_TILE)
    n_sub = min(FOURIER_POS_SUB, seq_len // tb)
    tk = n_sub * tb
    nm = seq_len // tm
    blk0 = row0 // tb
    per_tile = FOURIER_TILE // tb
    scale = float((seq_len * F_GROUP_CH) ** -0.5)

    def pq_spec(i):
        def index(b, m, k):
            blk = blk0 + b * nm + k * n_sub + i
            return blk // per_tile, 0, blk % per_tile, 0
        return pl.BlockSpec((None, 2, tb, D_MODEL), index)

    return pl.pallas_call(
        functools.partial(_fourier_pos_kernel, scale=scale, n_sub=n_sub),
        grid=(BATCH, nm, seq_len // tk),
        in_specs=[
            pl.BlockSpec((2, tm // DFT_SPLIT, tk), lambda b, m, k: (0, m, k)),
            pl.BlockSpec((2, DFT_SPLIT, tk), lambda b, m, k: (0, 0, k)),
            *[pq_spec(i) for i in range(n_sub)],
            pl.BlockSpec((None, D_MODEL, D_MODEL), lambda b, m, k: (j_odd, 0, 0)),
            pl.BlockSpec((tm, D_MODEL), lambda b, m, k: (blk0 + b * nm + m, 0)),
            pl.BlockSpec((None, 3, 1, D_MODEL), lambda b, m, k: (BATCH if is_ctx else b, 1, 0, 0)),
        ],
        out_specs=pl.BlockSpec((tm, D_MODEL), lambda b, m, k: (b * nm + m, 0)),
        out_shape=jax.ShapeDtypeStruct((BATCH * seq_len, D_MODEL), F32),
        scratch_shapes=[pltpu.VMEM((tm, D_MODEL), F32), pltpu.VMEM((n_sub, tm, 2 * tb), BF16)],
        compiler_params=_cparams(("arbitrary", "arbitrary", "arbitrary")),
        name="fourier_pos",
    )(u_tab, v_tab, *([pq] * n_sub), w, x, mods_i)


def _phase_table(freq, n):
    l = jnp.arange(n, dtype=jnp.int32)
    ang = ((freq[:, None] * l[None, :]) % n).astype(F32) * np.float32(2.0 * np.pi / n)
    return jnp.stack([jnp.cos(ang), jnp.sin(ang)])


def _dft_tables(n):
    a = jnp.arange(n // DFT_SPLIT, dtype=jnp.int32) * DFT_SPLIT
    b = jnp.arange(DFT_SPLIT, dtype=jnp.int32)
    return _phase_table(a, n), _phase_table(b, n)


def kernel(x, c, ctx, c_ctx, mod_w, mod_b, norm_g, ffn_w1, ffn_w3, ffn_w2, mix_w_in, mix_w_out, qk_g, rpb,
           conv_w, conv_b, dt_bias, a_log, ssd_d, ssd_norm_g, fourier_w):
    n_even = mix_w_in.shape[0]
    xs = jnp.concatenate([x.reshape(N_LAT, D_MODEL), ctx.reshape(BATCH * CTX_LEN, D_MODEL)], axis=0)
    mods = _modulation(c, c_ctx, mod_w, mod_b)

    w1 = ffn_w1.astype(BF16)
    w3 = ffn_w3.astype(BF16)
    w2 = ffn_w2.astype(BF16)
    w_in = jnp.pad(mix_w_in, ((0, 0), (0, 0), (0, IN_W_PAD - IN_W))).astype(BF16)
    w_out = mix_w_out.astype(BF16)
    w_f = fourier_w.astype(BF16)
    qkg = qk_g.reshape(n_even, 2, 1, NA_HEAD_DIM)
    conv_w8 = jnp.pad(conv_w, ((0, 0), (0, SUBLANE - CONV_K), (0, 0)))
    conv_b2 = conv_b.reshape(n_even, 1, CONV_CH)
    pad_l = LANE - 2 * SSD_HEADS
    dtb = jnp.pad(dt_bias.reshape(n_even, 1, 2 * SSD_HEADS), ((0, 0), (0, 0), (0, pad_l)))
    alog = jnp.pad(a_log.reshape(n_even, 1, 2 * SSD_HEADS), ((0, 0), (0, 0), (0, pad_l)))
    dsk = jnp.repeat(ssd_d, SSD_HEAD_DIM, axis=-1).reshape(n_even, 1, SSD_INNER)
    ng = ssd_norm_g.reshape(n_even, 1, SSD_INNER)
    cos_t, sin_t = _rope_tables()
    chan = _phase_table(jnp.arange(F_GROUP_CH, dtype=jnp.int32), F_GROUP_CH)
    cs_chan = jnp.concatenate([chan[0], chan[1]], axis=1).astype(BF16)
    dft_lat = _dft_tables(SEQ)
    dft_ctx = _dft_tables(CTX_LEN)

    last_ctx = ((DEPTH - 1) // 2) * 2
    for i in range(DEPTH):
        use_ctx = i <= last_ctx
        ctx_out = i < last_ctx
        j = i // 2
        mods_i = mods[i]
        g = norm_g[i].reshape(3, 1, D_MODEL)
        rows_in = N_TOK if use_ctx else N_LAT
        rows_out = N_TOK if ctx_out else N_LAT
        xs = _ffn(xs, rows_in, mods_i, 0, g[0], w1, w3, w2, i, 0)
        if i % 2 == 0:
            assert use_ctx, "even mixer layers read the context keys and SSD states"
            proj = _in_proj(xs, rows_in, mods_i, g[1], w_in, j)
            bias = _attention_bias(rpb[j])
            attn, attn_ctx = _na_attention(proj, cos_t, sin_t, qkg[j], bias)
            if ctx_out:
                attn = jnp.concatenate([attn, attn_ctx], axis=0)
            act = _conv_silu(proj, conv_w8, conv_b2, j)
            y_f, y_b = _ssd_scan(act, proj, dtb, alog, j)
            xs = _out_proj(attn, y_f, y_b, act, proj, dsk, ng, w_out, xs, rows_out, mods_i, j)
        else:
            pq = _fourier_chan(xs, rows_out, mods_i, g[1], cs_chan)
            new = _fourier_pos(dft_lat, pq, w_f, xs, mods_i, j, SEQ, 0, False)
            if ctx_out:
                new_ctx = _fourier_pos(dft_ctx, pq, w_f, xs, mods_i, j, CTX_LEN, N_LAT, True)
                new = jnp.concatenate([new, new_ctx], axis=0)
            xs = new
        xs = _ffn(xs, rows_out, mods_i, 2, g[2], w1, w3, w2, i, 1)
    return xs[:N_LAT].reshape(BATCH, SEQ, D_MODEL)
```
